```python
import math
import jax, jax.numpy as jnp
from jax import lax
import numpy as np

D_MODEL = 1024
BATCH = 1
SEQ = 16384
DEPTH = 4

N_MIXERS = 4
RMS_EPS = 1e-6
D_FF = 4 * D_MODEL

CHUNK = 128
GMLP_HEADS = 8
GMLP_DIM = D_MODEL
GMLP_HEAD_DIM = GMLP_DIM // GMLP_HEADS

POOL_WINDOWS = (2, 4, 8, 16)
POOL_GROUP = D_MODEL // len(POOL_WINDOWS)

MLA_HEADS = 8
Q_LORA = 384
KV_LORA = 256
QK_NOPE = 128
QK_ROPE = 64
V_HEAD = 128
ROPE_THETA = 10000.0
Q_BLOCK = 128

S5_GROUP_CH = 16
S5_GROUPS = D_MODEL // S5_GROUP_CH
S5_STATE = 64
DT_MIN = 1e-3
DT_MAX = 1e-1

N_A = (DEPTH + 3) // 4
N_B = (DEPTH + 2) // 4
N_C = (DEPTH + 1) // 4
N_D = DEPTH // 4

kernel_name = "hybrid_interleaved_gmlp_pool_mla_s5"


def rmsnorm(x, g):
    xf = x.astype(jnp.float32)
    y = xf * lax.rsqrt(jnp.mean(xf * xf, axis=-1, keepdims=True) + RMS_EPS)
    return (y * g.astype(jnp.float32)).astype(x.dtype)


def layernorm(x, g, b):
    xf = x.astype(jnp.float32)
    mu = jnp.mean(xf, axis=-1, keepdims=True)
    xc = xf - mu
    y = xc * lax.rsqrt(jnp.mean(xc * xc, axis=-1, keepdims=True) + RMS_EPS)
    return (y * g.astype(jnp.float32) + b.astype(jnp.float32)).astype(x.dtype)


def gmlp_mixer(h, w_in, b_in, g_v, b_v, w_s, b_s, w_out):
    B_, S, _ = h.shape
    uv = jax.nn.gelu(h @ w_in + b_in)
    u, v = jnp.split(uv, 2, axis=-1)
    v = layernorm(v, g_v, b_v)
    v = v.reshape(B_, S // CHUNK, CHUNK, GMLP_HEADS, GMLP_HEAD_DIM)
    mask = jnp.tril(jnp.ones((CHUNK, CHUNK), dtype=bool))
    ws = jnp.where(mask[None], w_s, jnp.zeros_like(w_s))
    sv = jnp.einsum('hts,bcshd->bcthd', ws, v) + b_s.T[None, None, :, :, None]
    y = u * sv.reshape(B_, S, GMLP_DIM)
    return y @ w_out


def pool_mixer(h, w_grp, scale):
    B_, S, _ = h.shape
    hf = h.astype(jnp.float32)
    cs_all = jnp.cumsum(hf, axis=1)
    count = jnp.arange(1, S + 1, dtype=jnp.float32)[None, :, None]
    outs = []
    for g, w in enumerate(POOL_WINDOWS):
        sl = slice(g * POOL_GROUP, (g + 1) * POOL_GROUP)
        xg = hf[..., sl]
        cs = cs_all[..., sl]
        cs_lag = jnp.pad(cs, ((0, 0), (w, 0), (0, 0)))[:, :S]
        mean = (cs - cs_lag) / jnp.minimum(count, float(w))
        outs.append(jnp.einsum('bsc,cd->bsd', (mean - xg).astype(h.dtype), w_grp[g]))
    return jnp.concatenate(outs, axis=-1) * scale


def rope(x, cos, sin):
    x1, x2 = jnp.split(x, 2, axis=-1)
    return jnp.concatenate([x1 * cos - x2 * sin, x1 * sin + x2 * cos], axis=-1)


def mla_mixer(h, positions, w_dq, g_q, w_uq, w_dkv, g_kv, w_uk, w_uv, w_o):
    B_, S, _ = h.shape
    inv_freq = ROPE_THETA ** (-jnp.arange(0, QK_ROPE, 2, dtype=jnp.float32) / QK_ROPE)
    ang = positions.astype(jnp.float32)[..., None] * inv_freq
    cos = jnp.cos(ang).astype(h.dtype)
    sin = jnp.sin(ang).astype(h.dtype)

    q = (rmsnorm(h @ w_dq, g_q) @ w_uq).reshape(B_, S, MLA_HEADS, QK_NOPE + QK_ROPE)
    q_nope, q_rope = q[..., :QK_NOPE], q[..., QK_NOPE:]
    q_rope = rope(q_rope, cos[:, :, None, :], sin[:, :, None, :])

    ckv = h @ w_dkv
    c = rmsnorm(ckv[..., :KV_LORA], g_kv)
    k_rope = rope(ckv[..., KV_LORA:], cos, sin)
    k_nope = (c @ w_uk).reshape(B_, S, MLA_HEADS, QK_NOPE)
    v = (c @ w_uv).reshape(B_, S, MLA_HEADS, V_HEAD)

    scale = (QK_NOPE + QK_ROPE) ** -0.5
    nb = S // Q_BLOCK
    qn_b = q_nope.reshape(B_, nb, Q_BLOCK, MLA_HEADS, QK_NOPE).transpose(1, 0, 2, 3, 4)
    qr_b = q_rope.reshape(B_, nb, Q_BLOCK, MLA_HEADS, QK_ROPE).transpose(1, 0, 2, 3, 4)
    k_pos = jnp.arange(S)

    def block(args):
        i, qn, qr = args
        s = (jnp.einsum('bqhd,bkhd->bhqk', qn, k_nope)
             + jnp.einsum('bqhd,bkd->bhqk', qr, k_rope)).astype(jnp.float32) * scale
        q_pos = i * Q_BLOCK + jnp.arange(Q_BLOCK)
        s = jnp.where(k_pos[None, :] <= q_pos[:, None], s, jnp.finfo(jnp.float32).min)
        p = jax.nn.softmax(s, axis=-1).astype(v.dtype)
        return jnp.einsum('bhqk,bkhd->bqhd', p, v)

    o = lax.map(block, (jnp.arange(nb), qn_b, qr_b))
    o = o.transpose(1, 0, 2, 3, 4).reshape(B_, S, MLA_HEADS * V_HEAD)
    return o @ w_o


def s5_mixer(h, lam_re, lam_im, log_dt, b_re, b_im, c_re, c_im, d_skip, w_glu_a, w_glu_b):
    B_, S, _ = h.shape
    u = h.astype(jnp.float32).reshape(B_, S, S5_GROUPS, S5_GROUP_CH)
    dt = jnp.exp(log_dt.astype(jnp.float32))[:, None]
    lr = lam_re.astype(jnp.float32)
    li = lam_im.astype(jnp.float32)
    mag = jnp.exp(lr * dt)
    ab_re = mag * jnp.cos(li * dt)
    ab_im = mag * jnp.sin(li * dt)
    den = lr * lr + li * li
    f_re = ((ab_re - 1.0) * lr + ab_im * li) / den
    f_im = (ab_im * lr - (ab_re - 1.0) * li) / den
    br = b_re.astype(jnp.float32)
    bi = b_im.astype(jnp.float32)
    bb_re = f_re[..., None] * br - f_im[..., None] * bi
    bb_im = f_re[..., None] * bi + f_im[..., None] * br
    bu_re = jnp.einsum('bsgc,gpc->bsgp', u, bb_re)
    bu_im = jnp.einsum('bsgc,gpc->bsgp', u, bb_im)
    a_re = jnp.broadcast_to(ab_re, bu_re.shape)
    a_im = jnp.broadcast_to(ab_im, bu_im.shape)

    def combine(e1, e2):
        a1r, a1i, b1r, b1i = e1
        a2r, a2i, b2r, b2i = e2
        return (a2r * a1r - a2i * a1i,
                a2r * a1i + a2i * a1r,
                a2r * b1r - a2i * b1i + b2r,
                a2r * b1i + a2i * b1r + b2i)

    _, _, xr, xi = lax.associative_scan(combine, (a_re, a_im, bu_re, bu_im), axis=1)
    y = (jnp.einsum('bsgp,gcp->bsgc', xr, c_re.astype(jnp.float32))
         - jnp.einsum('bsgp,gcp->bsgc', xi, c_im.astype(jnp.float32)))
    y = y + d_skip.astype(jnp.float32) * u
    y = jax.nn.gelu(y.reshape(B_, S, D_MODEL)).astype(h.dtype)
    return (y @ w_glu_a) * jax.nn.sigmoid(y @ w_glu_b)


def setup_inputs(seed: int = 0) -> dict:
    key = jax.random.key(seed)
    ks = iter(jax.random.split(key, 40))
    nrm = lambda shape, s: jax.random.normal(next(ks), shape, jnp.float32) * s
    d = D_MODEL
    inp = {}
    inp["x"] = nrm((BATCH, SEQ, d), 1.0)
    inp["positions"] = jnp.broadcast_to(jnp.arange(SEQ, dtype=jnp.int32), (BATCH, SEQ))
    inp["norm_g"] = 1.0 + nrm((DEPTH, 4, d), 0.02)
    inp["w_up"] = nrm((DEPTH, d, D_FF), d ** -0.5)
    inp["w_down"] = nrm((DEPTH, D_FF, d), D_FF ** -0.5)
    inp["a_w_in"] = nrm((N_A, d, 2 * GMLP_DIM), d ** -0.5)
    inp["a_b_in"] = nrm((N_A, 2 * GMLP_DIM), 0.01)
    inp["a_g_v"] = 1.0 + nrm((N_A, GMLP_DIM), 0.02)
    inp["a_b_v"] = nrm((N_A, GMLP_DIM), 0.01)
    inp["a_w_s"] = nrm((N_A, GMLP_HEADS, CHUNK, CHUNK), CHUNK ** -0.5)
    inp["a_b_s"] = 1.0 + nrm((N_A, GMLP_HEADS, CHUNK), 0.01)
    inp["a_w_out"] = nrm((N_A, GMLP_DIM, d), GMLP_DIM ** -0.5)
    inp["b_w_grp"] = nrm((N_B, len(POOL_WINDOWS), POOL_GROUP, POOL_GROUP), POOL_GROUP ** -0.5)
    inp["b_scale"] = 1.0 + nrm((N_B, d), 0.1)
    inp["c_w_dq"] = nrm((N_C, d, Q_LORA), d ** -0.5)
    inp["c_g_q"] = 1.0 + nrm((N_C, Q_LORA), 0.02)
    inp["c_w_uq"] = nrm((N_C, Q_LORA, MLA_HEADS * (QK_NOPE + QK_ROPE)), Q_LORA ** -0.5)
    inp["c_w_dkv"] = nrm((N_C, d, KV_LORA + QK_ROPE), d ** -0.5)
    inp["c_g_kv"] = 1.0 + nrm((N_C, KV_LORA), 0.02)
    inp["c_w_uk"] = nrm((N_C, KV_LORA, MLA_HEADS * QK_NOPE), KV_LORA ** -0.5)
    inp["c_w_uv"] = nrm((N_C, KV_LORA, MLA_HEADS * V_HEAD), KV_LORA ** -0.5)
    inp["c_w_o"] = nrm((N_C, MLA_HEADS * V_HEAD, d), (MLA_HEADS * V_HEAD) ** -0.5)
    G, P, C = S5_GROUPS, S5_STATE, S5_GROUP_CH
    inp["d_lam_re"] = -0.5 + nrm((N_D, G, P), 0.01)
    inp["d_lam_im"] = jnp.pi * jnp.arange(P, dtype=jnp.float32) + nrm((N_D, G, P), 0.01)
    inp["d_log_dt"] = jax.random.uniform(next(ks), (N_D, G), jnp.float32,
                                          math.log(DT_MIN), math.log(DT_MAX))
    inp["d_b_re"] = nrm((N_D, G, P, C), (2 * C) ** -0.5)
    inp["d_b_im"] = nrm((N_D, G, P, C), (2 * C) ** -0.5)
    inp["d_c_re"] = nrm((N_D, G, C, P), (2 * P) ** -0.5)
    inp["d_c_im"] = nrm((N_D, G, C, P), (2 * P) ** -0.5)
    inp["d_skip"] = nrm((N_D, G, C), 1.0)
    inp["d_w_glu_a"] = nrm((N_D, d, d), d ** -0.5)
    inp["d_w_glu_b"] = nrm((N_D, d, d), d ** -0.5)
    return inp


def reference(x, positions, norm_g, w_up, w_down,
              a_w_in, a_b_in, a_g_v, a_b_v, a_w_s, a_b_s, a_w_out,
              b_w_grp, b_scale,
              c_w_dq, c_g_q, c_w_uq, c_w_dkv, c_g_kv, c_w_uk, c_w_uv, c_w_o,
              d_lam_re, d_lam_im, d_log_dt, d_b_re, d_b_im, d_c_re, d_c_im, d_skip,
              d_w_glu_a, d_w_glu_b):
    h = x
    for i in range(DEPTH):
        m, j = i % N_MIXERS, i // N_MIXERS
        z = rmsnorm(h, norm_g[i, 0])
        if m == 0:
            z = gmlp_mixer(z, a_w_in[j], a_b_in[j], a_g_v[j], a_b_v[j], a_w_s[j], a_b_s[j], a_w_out[j])
        elif m == 1:
            z = pool_mixer(z, b_w_grp[j], b_scale[j])
        elif m == 2:
            z = mla_mixer(z, positions, c_w_dq[j], c_g_q[j], c_w_uq[j], c_w_dkv[j], c_g_kv[j],
                          c_w_uk[j], c_w_uv[j], c_w_o[j])
        else:
            z = s5_mixer(z, d_lam_re[j], d_lam_im[j], d_log_dt[j], d_b_re[j], d_b_im[j],
                         d_c_re[j], d_c_im[j], d_skip[j], d_w_glu_a[j], d_w_glu_b[j])
        h = h + rmsnorm(z, norm_g[i, 1])
        z = rmsnorm(h, norm_g[i, 2])
        z = jnp.square(jax.nn.relu(z @ w_up[i])) @ w_down[i]
        h = h + rmsnorm(z, norm_g[i, 3])
    return h
```

```python
import functools
import math

import jax
import jax.numpy as jnp
from jax import lax
from jax.experimental import pallas as pl
from jax.experimental.pallas import tpu as pltpu

F32 = jnp.float32
BF16 = jnp.bfloat16

D_MODEL = 1024
D_FF = 4 * D_MODEL
RMS_EPS = 1e-6
LANES = 128
SUBLANES = 8
VMEM_LIMIT = 56 * 1024 * 1024

CHUNK = 128
GMLP_HEADS = 8
POOL_WINDOWS = (2, 4, 8, 16)
POOL_GROUP = D_MODEL // len(POOL_WINDOWS)
POOL_HALO = 16
MLA_HEADS = 8
Q_LORA = 384
KV_LORA = 256
QK_NOPE = 128
QK_ROPE = 64
V_HEAD = 128
ROPE_THETA = 10000.0
QK_PAD = 256
S5_GROUP_CH = 16
S5_GROUPS = D_MODEL // S5_GROUP_CH
S5_STATE = 64
S5_BLOCK_GROUPS = LANES // S5_GROUP_CH
S5_BLOCKS = S5_GROUPS // S5_BLOCK_GROUPS
S5_BLOCK_STATE = S5_BLOCK_GROUPS * S5_STATE


def _rms(x, g):
    return x * lax.rsqrt(jnp.mean(x * x, axis=-1, keepdims=True) + RMS_EPS) * g


def _gelu(x):
    c = math.sqrt(2.0 / math.pi)
    return 0.5 * x * (1.0 + jnp.tanh(c * (x + 0.044715 * (x * x * x))))


def _dot(a, b):
    return jnp.dot(a, b, preferred_element_type=F32)


def _const_spec(shape):
    nd = len(shape)
    return pl.BlockSpec(shape, lambda *_: (0,) * nd, pipeline_mode=pl.Buffered(1))


def _row_spec(tm, width):
    return pl.BlockSpec((tm, width), lambda i: (i, 0))


def _params(*sem):
    return pltpu.CompilerParams(dimension_semantics=sem, vmem_limit_bytes=VMEM_LIMIT)


def _mlp_body(h_ref, g_ref, wu_ref, wd_ref, o_ref, *, tf):
    h = h_ref[...]
    z = _rms(h, g_ref[2:3, :]).astype(BF16)
    acc = jnp.zeros(h.shape, F32)
    for c in range(D_FF // tf):
        a = _dot(z, wu_ref[:, c * tf:(c + 1) * tf])
        a = jnp.square(jnp.maximum(a, 0.0)).astype(BF16)
        acc = acc + _dot(a, wd_ref[c * tf:(c + 1) * tf, :])
    o_ref[...] = h + _rms(acc, g_ref[3:4, :])


def _mlp(h, g, w_up, w_down, *, tm=512, tf=1024):
    s = h.shape[0]
    return pl.pallas_call(
        functools.partial(_mlp_body, tf=tf),
        grid=(s // tm,),
        in_specs=[_row_spec(tm, D_MODEL), _const_spec(g.shape),
                  _const_spec(w_up.shape), _const_spec(w_down.shape)],
        out_specs=_row_spec(tm, D_MODEL),
        out_shape=jax.ShapeDtypeStruct(h.shape, F32),
        compiler_params=_params("parallel"),
        name="mlp",
    )(h, g, w_up, w_down)


def _gmlp_body(h_ref, g_ref, win_ref, bin_ref, gv_ref, bv_ref, ws_ref, bs_ref, wout_ref, o_ref,
               sv_ref, *, tm):
    h = h_ref[...]
    z = _rms(h, g_ref[0:1, :]).astype(BF16)
    uv = _gelu(_dot(z, win_ref[...]) + bin_ref[...])
    u = uv[:, :D_MODEL]
    v = uv[:, D_MODEL:]
    mu = jnp.mean(v, axis=-1, keepdims=True)
    vc = v - mu
    v = vc * lax.rsqrt(jnp.mean(vc * vc, axis=-1, keepdims=True) + RMS_EPS)
    v = (v * gv_ref[...] + bv_ref[...]).astype(BF16)
    row = lax.broadcasted_iota(jnp.int32, (CHUNK, CHUNK), 0)
    col = lax.broadcasted_iota(jnp.int32, (CHUNK, CHUNK), 1)
    nchunk = tm // CHUNK
    for hd in range(GMLP_HEADS):
        ws = jnp.where(col <= row, ws_ref[hd], 0.0).astype(BF16)
        cols = slice(hd * LANES, (hd + 1) * LANES)
        rhs = jnp.concatenate([v[c * CHUNK:(c + 1) * CHUNK, cols] for c in range(nchunk)], axis=1)
        sv = _dot(ws, rhs)
        for c in range(nchunk):
            sv_ref[c * CHUNK:(c + 1) * CHUNK, cols] = (
                sv[:, c * LANES:(c + 1) * LANES] + bs_ref[:, cols])
    y = (u * sv_ref[...]).astype(BF16)
    o_ref[...] = h + _rms(_dot(y, wout_ref[...]), g_ref[1:2, :])


def _gmlp(h, g, w_in, b_in, g_v, b_v, w_s, b_s_full, w_out, *, tm=256):
    s = h.shape[0]
    return pl.pallas_call(
        functools.partial(_gmlp_body, tm=tm),
        grid=(s // tm,),
        in_specs=[_row_spec(tm, D_MODEL), _const_spec(g.shape), _const_spec(w_in.shape),
                  _const_spec(b_in.shape), _const_spec(g_v.shape), _const_spec(b_v.shape),
                  _const_spec(w_s.shape), _const_spec(b_s_full.shape), _const_spec(w_out.shape)],
        out_specs=_row_spec(tm, D_MODEL),
        out_shape=jax.ShapeDtypeStruct(h.shape, F32),
        scratch_shapes=[pltpu.VMEM((tm, D_MODEL), F32)],
        compiler_params=_params("parallel"),
        name="gmlp",
    )(h, g, w_in, b_in, g_v, b_v, w_s, b_s_full, w_out)


def _pool_body(h_ref, halo_ref, g_ref, w_ref, scale_ref, o_ref, zz_ref, *, tm):
    i = pl.program_id(0)
    h = h_ref[...]
    z = _rms(h, g_ref[0:1, :])
    zh = _rms(halo_ref[...], g_ref[0:1, :])
    zz_ref[0:POOL_HALO, :] = jnp.where(i > 0, zh, 0.0)
    zz_ref[POOL_HALO:, :] = z
    pos = i * tm + lax.broadcasted_iota(jnp.int32, (tm, 1), 0)
    outs = []
    for gi, w in enumerate(POOL_WINDOWS):
        cols = slice(gi * POOL_GROUP, (gi + 1) * POOL_GROUP)
        xg = z[:, cols]
        win = xg
        for k in range(1, w):
            win = win + zz_ref[POOL_HALO - k:POOL_HALO - k + tm, cols]
        count = jnp.minimum(pos + 1, w).astype(F32)
        diff = (win / count - xg).astype(BF16)
        outs.append(_dot(diff, w_ref[gi]))
    out = jnp.concatenate(outs, axis=-1) * scale_ref[...]
    o_ref[...] = h + _rms(out, g_ref[1:2, :])


def _pool(h, g, w_grp, scale, *, tm=512):
    s = h.shape[0]
    ratio = tm // POOL_HALO
    return pl.pallas_call(
        functools.partial(_pool_body, tm=tm),
        grid=(s // tm,),
        in_specs=[_row_spec(tm, D_MODEL),
                  pl.BlockSpec((POOL_HALO, D_MODEL), lambda i: (jnp.maximum(i * ratio - 1, 0), 0)),
                  _const_spec(g.shape), _const_spec(w_grp.shape), _const_spec(scale.shape)],
        out_specs=_row_spec(tm, D_MODEL),
        out_shape=jax.ShapeDtypeStruct(h.shape, F32),
        scratch_shapes=[pltpu.VMEM((tm + POOL_HALO, D_MODEL), F32)],
        compiler_params=_params("parallel"),
        name="pool",
    )(h, h, g, w_grp, scale)


def _rope_tile(x, cos, sin_lo, sin_hi):
    n = x.shape[1] // LANES
    outs = []
    for t in range(n):
        xt = x[:, t * LANES:(t + 1) * LANES]
        up = pltpu.roll(xt, 32, axis=1)
        down = pltpu.roll(xt, LANES - 32, axis=1)
        outs.append(xt * cos + down * sin_lo + up * sin_hi)
    return outs


def _mla_proj_body(h_ref, pos_ref, g_ref, wdq_ref, gq_ref, wuqn_ref, wuqr_ref, wdkv_ref, gkv_ref,
                   wuk_ref, wuv_ref, invf_ref, q_ref, k_ref, v_ref, *, scale):
    h = h_ref[...]
    z = _rms(h, g_ref[0:1, :]).astype(BF16)
    ang = pos_ref[...].astype(F32) * invf_ref[...]
    lane = lax.broadcasted_iota(jnp.int32, ang.shape, 1)
    cos = jnp.where(lane < QK_ROPE, jnp.cos(ang), 0.0)
    sin = jnp.sin(ang)
    sin_lo = jnp.where(lane < QK_ROPE // 2, -sin, 0.0)
    sin_hi = jnp.where((lane >= QK_ROPE // 2) & (lane < QK_ROPE), sin, 0.0)

    ql = _rms(_dot(z, wdq_ref[...]), gq_ref[...]).astype(BF16)
    qn = _dot(ql, wuqn_ref[...]) * scale
    qr = _rope_tile(_dot(ql, wuqr_ref[...]) * scale, cos, sin_lo, sin_hi)
    ckv = _dot(z, wdkv_ref[...])
    c = _rms(ckv[:, :KV_LORA], gkv_ref[...]).astype(BF16)
    kr = _rope_tile(ckv[:, KV_LORA:], cos, sin_lo, sin_hi)[0].astype(BF16)
    kn = _dot(c, wuk_ref[...])
    for hd in range(MLA_HEADS):
        cols = slice(hd * LANES, (hd + 1) * LANES)
        q_ref[:, hd * QK_PAD:hd * QK_PAD + LANES] = qn[:, cols].astype(BF16)
        q_ref[:, hd * QK_PAD + LANES:(hd + 1) * QK_PAD] = qr[hd].astype(BF16)
        k_ref[:, hd * QK_PAD:hd * QK_PAD + LANES] = kn[:, cols].astype(BF16)
        k_ref[:, hd * QK_PAD + LANES:(hd + 1) * QK_PAD] = kr
    v_ref[...] = _dot(c, wuv_ref[...]).astype(BF16)


def _mla_proj(h, pos, g, w_dq, g_q, w_uq_n, w_uq_r, w_dkv, g_kv, w_uk, w_uv, invf, *, tm=512):
    s = h.shape[0]
    scale = (QK_NOPE + QK_ROPE) ** -0.5
    consts = (g, w_dq, g_q, w_uq_n, w_uq_r, w_dkv, g_kv, w_uk, w_uv, invf)
    return pl.pallas_call(
        functools.partial(_mla_proj_body, scale=scale),
        grid=(s // tm,),
        in_specs=[_row_spec(tm, D_MODEL), _row_spec(tm, 1)] + [_const_spec(a.shape) for a in consts],
        out_specs=[_row_spec(tm, MLA_HEADS * QK_PAD), _row_spec(tm, MLA_HEADS * QK_PAD),
                   _row_spec(tm, MLA_HEADS * V_HEAD)],
        out_shape=[jax.ShapeDtypeStruct((s, MLA_HEADS * QK_PAD), BF16),
                   jax.ShapeDtypeStruct((s, MLA_HEADS * QK_PAD), BF16),
                   jax.ShapeDtypeStruct((s, MLA_HEADS * V_HEAD), BF16)],
        compiler_params=_params("parallel"),
        name="mla_proj",
    )(h, pos, *consts)


def _attn_body(q_ref, k_ref, v_ref, o_ref, *, tq, tk):
    qi = pl.program_id(1)
    q = q_ref[...]
    neg = jnp.finfo(F32).min

    def step(ki, carry, masked):
        m, l, acc = carry
        start = pl.multiple_of(ki * tk, tk)
        k = k_ref[pl.ds(start, tk), :]
        s = lax.dot_general(q, k, (((1,), (1,)), ((), ())), preferred_element_type=F32)
        if masked:
            qpos = qi * tq + lax.broadcasted_iota(jnp.int32, (tq, tk), 0)
            kpos = ki * tk + lax.broadcasted_iota(jnp.int32, (tq, tk), 1)
            s = jnp.where(kpos <= qpos, s, neg)
        m_new = jnp.maximum(m, jnp.max(s, axis=1, keepdims=True))
        alpha = jnp.exp(m - m_new)
        p = jnp.exp(s - m_new)
        l = alpha * l + jnp.sum(p, axis=1, keepdims=True)
        acc = alpha * acc + _dot(p.astype(BF16), v_ref[pl.ds(start, tk), :])
        return m_new, l, acc

    init = (jnp.full((tq, 1), neg, F32), jnp.zeros((tq, 1), F32), jnp.zeros((tq, V_HEAD), F32))
    nfull = qi * (tq // tk)
    carry = lax.fori_loop(0, nfull, lambda ki, c: step(ki, c, False), init)
    for j in range(tq // tk):
        carry = step(nfull + j, carry, True)
    _, l, acc = carry
    o_ref[...] = (acc / l).astype(o_ref.dtype)


def _attention(q, k, v, *, tq=512, tk=512):
    s = q.shape[0]
    return pl.pallas_call(
        functools.partial(_attn_body, tq=tq, tk=tk),
        grid=(MLA_HEADS, s // tq),
        in_specs=[pl.BlockSpec((tq, QK_PAD), lambda hd, i: (i, hd)),
                  pl.BlockSpec((s, QK_PAD), lambda hd, i: (0, hd)),
                  pl.BlockSpec((s, V_HEAD), lambda hd, i: (0, hd))],
        out_specs=pl.BlockSpec((tq, V_HEAD), lambda hd, i: (i, hd)),
        out_shape=jax.ShapeDtypeStruct((s, MLA_HEADS * V_HEAD), BF16),
        compiler_params=_params("parallel", "parallel"),
        name="mla_attn",
    )(q, k, v)


def _out_proj_body(h_ref, o_ref_in, g_ref, wo_ref, o_ref):
    o_ref[...] = h_ref[...] + _rms(_dot(o_ref_in[...], wo_ref[...]), g_ref[1:2, :])


def _out_proj(h, o, g, w_o, *, tm=512):
    s = h.shape[0]
    return pl.pallas_call(
        _out_proj_body,
        grid=(s // tm,),
        in_specs=[_row_spec(tm, D_MODEL), _row_spec(tm, D_MODEL),
                  _const_spec(g.shape), _const_spec(w_o.shape)],
        out_specs=_row_spec(tm, D_MODEL),
        out_shape=jax.ShapeDtypeStruct(h.shape, F32),
        compiler_params=_params("parallel"),
        name="mla_out",
    )(h, o, g, w_o)


def _s5_disc_body(lr_ref, li_ref, ldt_ref, br_ref, bi_ref, abr_ref, abi_ref, bbr_ref, bbi_ref):
    dt = jnp.exp(ldt_ref[...])
    lr = lr_ref[...]
    li = li_ref[...]
    mag = jnp.exp(lr * dt)
    ab_re = mag * jnp.cos(li * dt)
    ab_im = mag * jnp.sin(li * dt)
    den = lr * lr + li * li
    f_re = ((ab_re - 1.0) * lr + ab_im * li) / den
    f_im = (ab_im * lr - (ab_re - 1.0) * li) / den
    abr_ref[...] = ab_re
    abi_ref[...] = ab_im
    br = br_ref[...]
    bi = bi_ref[...]
    bbr_ref[...] = f_re[:, None, :] * br - f_im[:, None, :] * bi
    bbi_ref[...] = f_re[:, None, :] * bi + f_im[:, None, :] * br


def _s5_discretise(lam_re, lam_im, log_dt, b_re_t, b_im_t):
    g, p = lam_re.shape
    c = b_re_t.shape[1]
    return pl.pallas_call(
        _s5_disc_body,
        out_shape=[jax.ShapeDtypeStruct((g, p), F32), jax.ShapeDtypeStruct((g, p), F32),
                   jax.ShapeDtypeStruct((g, c, p), F32), jax.ShapeDtypeStruct((g, c, p), F32)],
        name="s5_disc",
    )(lam_re, lam_im, log_dt.reshape(g, 1), b_re_t, b_im_t)


def _s5_body(h_ref, g_ref, ar_ref, ai_ref, bblk_ref, cblk_ref, dskip_ref, wa_ref, wb_ref, o_ref,
             zs_ref, zp_ref, bur_ref, bui_ref, xc_ref, yp_ref, y_ref, pr_ref, pi_ref, cr_ref,
             ci_ref, *, tm):
    seg = tm // SUBLANES
    i0 = pl.program_id(0)
    nst = S5_BLOCK_STATE

    @pl.when(i0 == 0)
    def _():
        cr_ref[...] = jnp.zeros(cr_ref.shape, F32)
        ci_ref[...] = jnp.zeros(ci_ref.shape, F32)
        ar = ar_ref[...]
        ai = ai_ref[...]
        pr_ref[0:1, :] = ar
        pi_ref[0:1, :] = ai

        def pw(j, carry):
            r, im = carry
            r, im = r * ar - im * ai, r * ai + im * ar
            pr_ref[pl.ds(j, 1), :] = r
            pi_ref[pl.ds(j, 1), :] = im
            return r, im

        lax.fori_loop(1, seg, pw, (ar, ai))

    h = h_ref[...]
    z = _rms(h, g_ref[0:1, :])
    for cb in range(S5_BLOCKS):
        zs_ref[cb] = z[:, cb * LANES:(cb + 1) * LANES]

    def regroup(i, _):
        for cb in range(S5_BLOCKS):
            zp_ref[cb, pl.ds(pl.multiple_of(i * SUBLANES, SUBLANES), SUBLANES), :] = (
                zs_ref[cb, pl.ds(i, SUBLANES, stride=seg), :])
        return 0

    lax.fori_loop(0, seg, regroup, 0)

    for cb in range(S5_BLOCKS):
        st = slice(cb * nst, (cb + 1) * nst)
        ub = zp_ref[cb].astype(BF16)
        bu = _dot(ub, bblk_ref[cb])
        bur_ref[...] = bu[:, :nst]
        bui_ref[...] = bu[:, nst:]
        ar = jnp.broadcast_to(ar_ref[:, st], (SUBLANES, nst))
        ai = jnp.broadcast_to(ai_ref[:, st], (SUBLANES, nst))

        def scan(i, carry):
            xr, xi = carry
            rows = pl.ds(pl.multiple_of(i * SUBLANES, SUBLANES), SUBLANES)
            nr = ar * xr - ai * xi + bur_ref[rows, :]
            ni = ar * xi + ai * xr + bui_ref[rows, :]
            bur_ref[rows, :] = nr
            bui_ref[rows, :] = ni
            return nr, ni

        zero = jnp.zeros((SUBLANES, nst), F32)
        er, ei = lax.fori_loop(0, seg, scan, (zero, zero))
        alr = pr_ref[seg - 1:seg, st]
        ali = pi_ref[seg - 1:seg, st]
        c_r = cr_ref[:, st]
        c_i = ci_ref[:, st]
        rows_r, rows_i = [], []
        for sgm in range(SUBLANES):
            rows_r.append(c_r)
            rows_i.append(c_i)
            e_r = er[sgm:sgm + 1, :]
            e_i = ei[sgm:sgm + 1, :]
            c_r, c_i = alr * c_r - ali * c_i + e_r, alr * c_i + ali * c_r + e_i
        cr_ref[:, st] = c_r
        ci_ref[:, st] = c_i
        cin_r = jnp.concatenate(rows_r, axis=0)
        cin_i = jnp.concatenate(rows_i, axis=0)

        def fix(i, _):
            rows = pl.ds(pl.multiple_of(i * SUBLANES, SUBLANES), SUBLANES)
            p_r = pr_ref[pl.ds(i, 1), st]
            p_i = pi_ref[pl.ds(i, 1), st]
            xr = bur_ref[rows, :] + (p_r * cin_r - p_i * cin_i)
            xi = bui_ref[rows, :] + (p_r * cin_i + p_i * cin_r)
            xc_ref[rows, 0:nst] = xr
            xc_ref[rows, nst:2 * nst] = xi
            return 0

        lax.fori_loop(0, seg, fix, 0)
        yp_ref[cb] = _dot(xc_ref[...].astype(BF16), cblk_ref[cb])

    for sgm in range(SUBLANES):
        def ungroup(j, _):
            dst = pl.ds(pl.multiple_of(sgm * seg + j * SUBLANES, SUBLANES), SUBLANES)
            src = pl.ds(j * SUBLANES * SUBLANES + sgm, SUBLANES, stride=SUBLANES)
            for cb in range(S5_BLOCKS):
                y_ref[dst, cb * LANES:(cb + 1) * LANES] = yp_ref[cb, src, :]
            return 0

        lax.fori_loop(0, seg // SUBLANES, ungroup, 0)

    y = _gelu(y_ref[...] + dskip_ref[...] * z).astype(BF16)
    out = _dot(y, wa_ref[...]) * jax.nn.sigmoid(_dot(y, wb_ref[...]))
    o_ref[...] = h + _rms(out, g_ref[1:2, :])


def _s5(h, g, ab_re, ab_im, bblk, cblk, dskip, w_a, w_b, *, tm=256):
    s = h.shape[0]
    seg = tm // SUBLANES
    nstate = S5_GROUPS * S5_STATE
    consts = (g, ab_re, ab_im, bblk, cblk, dskip, w_a, w_b)
    return pl.pallas_call(
        functools.partial(_s5_body, tm=tm),
        grid=(s // tm,),
        in_specs=[_row_spec(tm, D_MODEL)] + [_const_spec(a.shape) for a in consts],
        out_specs=_row_spec(tm, D_MODEL),
        out_shape=jax.ShapeDtypeStruct(h.shape, F32),
        scratch_shapes=[
            pltpu.VMEM((S5_BLOCKS, tm, LANES), F32),
            pltpu.VMEM((S5_BLOCKS, tm, LANES), F32),
            pltpu.VMEM((tm, S5_BLOCK_STATE), F32),
            pltpu.VMEM((tm, S5_BLOCK_STATE), F32),
            pltpu.VMEM((tm, 2 * S5_BLOCK_STATE), F32),
            pltpu.VMEM((S5_BLOCKS, tm, LANES), F32),
            pltpu.VMEM((tm, D_MODEL), F32),
            pltpu.VMEM((seg, nstate), F32),
            pltpu.VMEM((seg, nstate), F32),
            pltpu.VMEM((1, nstate), F32),
            pltpu.VMEM((1, nstate), F32),
        ],
        compiler_params=_params("arbitrary"),
        name="s5",
    )(h, *consts)


def _s5_block_matrices(bb_re, bb_im, c_re, c_im):
    nb, gb, c, p = S5_BLOCKS, S5_BLOCK_GROUPS, S5_GROUP_CH, S5_STATE
    eye = jnp.eye(gb, dtype=F32)

    def in_map(bb):
        t = bb.reshape(nb, gb, c, p)
        return jnp.einsum('bgcp,gk->bgckp', t, eye).reshape(nb, gb * c, gb * p)

    def out_map(cc):
        t = cc.reshape(nb, gb, c, p)
        return jnp.einsum('bgcp,gk->bgpkc', t, eye).reshape(nb, gb * p, gb * c)

    bblk = jnp.concatenate([in_map(bb_re), in_map(bb_im)], axis=2).astype(BF16)
    cblk = jnp.concatenate([out_map(c_re), out_map(-c_im)], axis=1).astype(BF16)
    return bblk, cblk


def _mla_weights(w_uq, w_dkv):
    wq = w_uq.reshape(Q_LORA, MLA_HEADS, QK_NOPE + QK_ROPE)
    w_n = wq[:, :, :QK_NOPE].reshape(Q_LORA, MLA_HEADS * QK_NOPE)
    w_r = jnp.pad(wq[:, :, QK_NOPE:], ((0, 0), (0, 0), (0, LANES - QK_ROPE)))
    w_r = w_r.reshape(Q_LORA, MLA_HEADS * LANES)
    w_dkv_p = jnp.pad(w_dkv, ((0, 0), (0, LANES - QK_ROPE)))
    return w_n.astype(BF16), w_r.astype(BF16), w_dkv_p.astype(BF16)


def kernel(x, positions, norm_g, w_up, w_down, a_w_in, a_b_in, a_g_v, a_b_v, a_w_s, a_b_s, a_w_out, b_w_grp, b_scale, c_w_dq, c_g_q, c_w_uq, c_w_dkv, c_g_kv, c_w_uk, c_w_uv, c_w_o, d_lam_re, d_lam_im, d_log_dt, d_b_re, d_b_im, d_c_re, d_c_im, d_skip, d_w_glu_a, d_w_glu_b):
    bsz, s, d = x.shape
    assert bsz == 1 and d == D_MODEL
    h = x.reshape(s, d)
    row = lambda a: a.reshape(1, -1)
    w_up_b = w_up.astype(BF16)
    w_down_b = w_down.astype(BF16)

    b_s_full = jnp.repeat(a_b_s[0].T, D_MODEL // GMLP_HEADS, axis=1)
    h = _gmlp(h, norm_g[0], a_w_in[0].astype(BF16), row(a_b_in[0]), row(a_g_v[0]), row(a_b_v[0]),
              a_w_s[0], b_s_full, a_w_out[0].astype(BF16))
    h = _mlp(h, norm_g[0], w_up_b[0], w_down_b[0])

    h = _pool(h, norm_g[1], b_w_grp[0].astype(BF16), row(b_scale[0]))
    h = _mlp(h, norm_g[1], w_up_b[1], w_down_b[1])

    w_uq_n, w_uq_r, w_dkv_p = _mla_weights(c_w_uq[0], c_w_dkv[0])
    inv_freq = ROPE_THETA ** (-jnp.arange(0, QK_ROPE, 2, dtype=F32) / QK_ROPE)
    invf = jnp.concatenate([inv_freq, inv_freq, jnp.zeros((LANES - QK_ROPE,), F32)]).reshape(1, LANES)
    q, k, v = _mla_proj(h, positions.reshape(s, 1), norm_g[2], c_w_dq[0].astype(BF16), row(c_g_q[0]),
                        w_uq_n, w_uq_r, w_dkv_p, row(c_g_kv[0]), c_w_uk[0].astype(BF16),
                        c_w_uv[0].astype(BF16), invf)
    o = _attention(q, k, v)
    h = _out_proj(h, o, norm_g[2], c_w_o[0].astype(BF16))
    h = _mlp(h, norm_g[2], w_up_b[2], w_down_b[2])

    tr = lambda a: jnp.swapaxes(a, 1, 2)
    ab_re, ab_im, bb_re, bb_im = _s5_discretise(d_lam_re[0], d_lam_im[0], d_log_dt[0],
                                                tr(d_b_re[0]), tr(d_b_im[0]))
    bblk, cblk = _s5_block_matrices(bb_re, bb_im, d_c_re[0], d_c_im[0])
    h = _s5(h, norm_g[3], row(ab_re), row(ab_im), bblk, cblk, row(d_skip[0]),
            d_w_glu_a[0].astype(BF16), d_w_glu_b[0].astype(BF16))
    h = _mlp(h, norm_g[3], w_up_b[3], w_down_b[3])
    return h.reshape(bsz, s, d)
```

```python
import functools
import math

import jax
import jax.numpy as jnp
from jax import lax
from jax.experimental import pallas as pl
from jax.experimental.pallas import tpu as pltpu

F32 = jnp.float32
BF16 = jnp.bfloat16

D_MODEL = 1024
D_FF = 4 * D_MODEL
RMS_EPS = 1e-6
LANES = 128
SUBLANES = 8
VMEM_LIMIT = 56 * 1024 * 1024

CHUNK = 128
GMLP_HEADS = 8
POOL_WINDOWS = (2, 4, 8, 16)
POOL_GROUP = D_MODEL // len(POOL_WINDOWS)
POOL_HALO = 16
MLA_HEADS = 8
Q_LORA = 384
KV_LORA = 256
QK_NOPE = 128
QK_ROPE = 64
V_HEAD = 128
ROPE_THETA = 10000.0
QK_PAD = 256
S5_GROUP_CH = 16
S5_GROUPS = D_MODEL // S5_GROUP_CH
S5_STATE = 64
S5_BLOCK_GROUPS = LANES // S5_GROUP_CH
S5_BLOCKS = S5_GROUPS // S5_BLOCK_GROUPS
S5_BLOCK_STATE = S5_BLOCK_GROUPS * S5_STATE


def _rms(x, g):
    return x * lax.rsqrt(jnp.mean(x * x, axis=-1, keepdims=True) + RMS_EPS) * g


def _gelu(x):
    c = math.sqrt(2.0 / math.pi)
    return 0.5 * x * (1.0 + jnp.tanh(c * (x + 0.044715 * (x * x * x))))


def _dot(a, b):
    return jnp.dot(a, b, preferred_element_type=F32)


def _const_spec(shape):
    nd = len(shape)
    return pl.BlockSpec(shape, lambda *_: (0,) * nd, pipeline_mode=pl.Buffered(1))


def _row_spec(tm, width):
    return pl.BlockSpec((tm, width), lambda i: (i, 0))


def _params(*sem):
    return pltpu.CompilerParams(dimension_semantics=sem, vmem_limit_bytes=VMEM_LIMIT)


def _mlp_body(h_ref, g_ref, wu_ref, wd_ref, o_ref, *, tf):
    h = h_ref[...]
    z = _rms(h, g_ref[2:3, :]).astype(BF16)
    acc = jnp.zeros(h.shape, F32)
    for c in range(D_FF // tf):
        a = _dot(z, wu_ref[:, c * tf:(c + 1) * tf])
        a = jnp.square(jnp.maximum(a, 0.0)).astype(BF16)
        acc = acc + _dot(a, wd_ref[c * tf:(c + 1) * tf, :])
    o_ref[...] = h + _rms(acc, g_ref[3:4, :])


def _mlp(h, g, w_up, w_down, *, tm=512, tf=1024):
    s = h.shape[0]
    return pl.pallas_call(
        functools.partial(_mlp_body, tf=tf),
        grid=(s // tm,),
        in_specs=[_row_spec(tm, D_MODEL), _const_spec(g.shape),
                  _const_spec(w_up.shape), _const_spec(w_down.shape)],
        out_specs=_row_spec(tm, D_MODEL),
        out_shape=jax.ShapeDtypeStruct(h.shape, F32),
        compiler_params=_params("parallel"),
        name="mlp",
    )(h, g, w_up, w_down)


def _gmlp_body(h_ref, g_ref, win_ref, bin_ref, gv_ref, bv_ref, ws_ref, bs_ref, wout_ref, o_ref,
               sv_ref, *, tm):
    h = h_ref[...]
    z = _rms(h, g_ref[0:1, :]).astype(BF16)
    uv = _gelu(_dot(z, win_ref[...]) + bin_ref[...])
    u = uv[:, :D_MODEL]
    v = uv[:, D_MODEL:]
    mu = jnp.mean(v, axis=-1, keepdims=True)
    vc = v - mu
    v = vc * lax.rsqrt(jnp.mean(vc * vc, axis=-1, keepdims=True) + RMS_EPS)
    v = (v * gv_ref[...] + bv_ref[...]).astype(BF16)
    row = lax.broadcasted_iota(jnp.int32, (CHUNK, CHUNK), 0)
    col = lax.broadcasted_iota(jnp.int32, (CHUNK, CHUNK), 1)
    nchunk = tm // CHUNK
    for hd in range(GMLP_HEADS):
        ws = jnp.where(col <= row, ws_ref[hd], 0.0).astype(BF16)
        cols = slice(hd * LANES, (hd + 1) * LANES)
        rhs = jnp.concatenate([v[c * CHUNK:(c + 1) * CHUNK, cols] for c in range(nchunk)], axis=1)
        sv = _dot(ws, rhs)
        for c in range(nchunk):
            sv_ref[c * CHUNK:(c + 1) * CHUNK, cols] = (
                sv[:, c * LANES:(c + 1) * LANES] + bs_ref[:, cols])
    y = (u * sv_ref[...]).astype(BF16)
    o_ref[...] = h + _rms(_dot(y, wout_ref[...]), g_ref[1:2, :])


def _gmlp(h, g, w_in, b_in, g_v, b_v, w_s, b_s_full, w_out, *, tm=256):
    s = h.shape[0]
    return pl.pallas_call(
        functools.partial(_gmlp_body, tm=tm),
        grid=(s // tm,),
        in_specs=[_row_spec(tm, D_MODEL), _const_spec(g.shape), _const_spec(w_in.shape),
                  _const_spec(b_in.shape), _const_spec(g_v.shape), _const_spec(b_v.shape),
                  _const_spec(w_s.shape), _const_spec(b_s_full.shape), _const_spec(w_out.shape)],
        out_specs=_row_spec(tm, D_MODEL),
        out_shape=jax.ShapeDtypeStruct(h.shape, F32),
        scratch_shapes=[pltpu.VMEM((tm, D_MODEL), F32)],
        compiler_params=_params("parallel"),
        name="gmlp",
    )(h, g, w_in, b_in, g_v, b_v, w_s, b_s_full, w_out)


def _pool_body(h_ref, halo_ref, g_ref, w_ref, scale_ref, o_ref, zz_ref, *, tm):
    i = pl.program_id(0)
    h = h_ref[...]
    z = _rms(h, g_ref[0:1, :])
    zh = _rms(halo_ref[...], g_ref[0:1, :])
    zz_ref[0:POOL_HALO, :] = jnp.where(i > 0, zh, 0.0)
    zz_ref[POOL_HALO:, :] = z
    pos = i * tm + lax.broadcasted_iota(jnp.int32, (tm, 1), 0)
    outs = []
    for gi, w in enumerate(POOL_WINDOWS):
        cols = slice(gi * POOL_GROUP, (gi + 1) * POOL_GROUP)
        xg = z[:, cols]
        win = xg
        for k in range(1, w):
            win = win + zz_ref[POOL_HALO - k:POOL_HALO - k + tm, cols]
        count = jnp.minimum(pos + 1, w).astype(F32)
        diff = (win / count - xg).astype(BF16)
        outs.append(_dot(diff, w_ref[gi]))
    out = jnp.concatenate(outs, axis=-1) * scale_ref[...]
    o_ref[...] = h + _rms(out, g_ref[1:2, :])


def _pool(h, g, w_grp, scale, *, tm=512):
    s = h.shape[0]
    ratio = tm // POOL_HALO
    return pl.pallas_call(
        functools.partial(_pool_body, tm=tm),
        grid=(s // tm,),
        in_specs=[_row_spec(tm, D_MODEL),
                  pl.BlockSpec((POOL_HALO, D_MODEL), lambda i: (jnp.maximum(i * ratio - 1, 0), 0)),
                  _const_spec(g.shape), _const_spec(w_grp.shape), _const_spec(scale.shape)],
        out_specs=_row_spec(tm, D_MODEL),
        out_shape=jax.ShapeDtypeStruct(h.shape, F32),
        scratch_shapes=[pltpu.VMEM((tm + POOL_HALO, D_MODEL), F32)],
        compiler_params=_params("parallel"),
        name="pool",
    )(h, h, g, w_grp, scale)


def _rope_tile(x, cos, sin_lo, sin_hi):
    n = x.shape[1] // LANES
    outs = []
    for t in range(n):
        xt = x[:, t * LANES:(t + 1) * LANES]
        up = pltpu.roll(xt, 32, axis=1)
        down = pltpu.roll(xt, LANES - 32, axis=1)
        outs.append(xt * cos + down * sin_lo + up * sin_hi)
    return outs


def _mla_proj_body(h_ref, pos_ref, g_ref, wdq_ref, gq_ref, wuqn_ref, wuqr_ref, wdkv_ref, gkv_ref,
                   wuk_ref, wuv_ref, invf_ref, q_ref, k_ref, v_ref, *, scale):
    h = h_ref[...]
    z = _rms(h, g_ref[0:1, :]).astype(BF16)
    ang = pos_ref[...].astype(F32) * invf_ref[...]
    lane = lax.broadcasted_iota(jnp.int32, ang.shape, 1)
    cos = jnp.where(lane < QK_ROPE, jnp.cos(ang), 0.0)
    sin = jnp.sin(ang)
    sin_lo = jnp.where(lane < QK_ROPE // 2, -sin, 0.0)
    sin_hi = jnp.where((lane >= QK_ROPE // 2) & (lane < QK_ROPE), sin, 0.0)

    ql = _rms(_dot(z, wdq_ref[...]), gq_ref[...]).astype(BF16)
    qn = _dot(ql, wuqn_ref[...]) * scale
    qr = _rope_tile(_dot(ql, wuqr_ref[...]) * scale, cos, sin_lo, sin_hi)
    ckv = _dot(z, wdkv_ref[...])
    c = _rms(ckv[:, :KV_LORA], gkv_ref[...]).astype(BF16)
    kr = _rope_tile(ckv[:, KV_LORA:], cos, sin_lo, sin_hi)[0].astype(BF16)
    kn = _dot(c, wuk_ref[...])
    for hd in range(MLA_HEADS):
        cols = slice(hd * LANES, (hd + 1) * LANES)
        q_ref[:, hd * QK_PAD:hd * QK_PAD + LANES] = qn[:, cols].astype(BF16)
        q_ref[:, hd * QK_PAD + LANES:(hd + 1) * QK_PAD] = qr[hd].astype(BF16)
        k_ref[:, hd * QK_PAD:hd * QK_PAD + LANES] = kn[:, cols].astype(BF16)
        k_ref[:, hd * QK_PAD + LANES:(hd + 1) * QK_PAD] = kr
    v_ref[...] = _dot(c, wuv_ref[...]).T.astype(BF16)


def _mla_proj(h, pos, g, w_dq, g_q, w_uq_n, w_uq_r, w_dkv, g_kv, w_uk, w_uv, invf, *, tm=512):
    s = h.shape[0]
    scale = (QK_NOPE + QK_ROPE) ** -0.5 * math.log2(math.e)
    consts = (g, w_dq, g_q, w_uq_n, w_uq_r, w_dkv, g_kv, w_uk, w_uv, invf)
    return pl.pallas_call(
        functools.partial(_mla_proj_body, scale=scale),
        grid=(s // tm,),
        in_specs=[_row_spec(tm, D_MODEL), _row_spec(tm, 1)] + [_const_spec(a.shape) for a in consts],
        out_specs=[_row_spec(tm, MLA_HEADS * QK_PAD), _row_spec(tm, MLA_HEADS * QK_PAD),
                   pl.BlockSpec((MLA_HEADS * V_HEAD, tm), lambda i: (0, i))],
        out_shape=[jax.ShapeDtypeStruct((s, MLA_HEADS * QK_PAD), BF16),
                   jax.ShapeDtypeStruct((s, MLA_HEADS * QK_PAD), BF16),
                   jax.ShapeDtypeStruct((MLA_HEADS * V_HEAD, s), BF16)],
        compiler_params=_params("parallel"),
        name="mla_proj",
    )(h, pos, *consts)


def _attn_body(q_ref, k_ref, vt_ref, o_ref, qt_ref, s0_ref, s1_ref, p0_ref, p1_ref, a0_ref, a1_ref,
               m_ref, l_ref, acc_ref, *, t):
    qi = pl.program_id(1)
    nfull = qi
    s_refs = (s0_ref, s1_ref)
    p_refs = (p0_ref, p1_ref)
    a_refs = (a0_ref, a1_ref)
    neg = -1e30
    rows = 2 * SUBLANES

    qt_ref[...] = q_ref[...].astype(F32).T.astype(BF16)
    m_ref[...] = jnp.full(m_ref.shape, neg, F32)
    l_ref[...] = jnp.zeros(l_ref.shape, F32)
    acc_ref[...] = jnp.zeros(acc_ref.shape, F32)

    def qk(tile, slot):
        start = pl.multiple_of(tile * t, t)
        s_refs[slot][...] = _dot(k_ref[pl.ds(start, t), :], qt_ref[...])

    def softmax(slot, masked):
        s_ref = s_refs[slot]
        p_ref = p_refs[slot]
        nslab = t // rows
        if masked:
            kpos = lax.broadcasted_iota(jnp.int32, (rows, t), 0)
            qpos = lax.broadcasted_iota(jnp.int32, (rows, t), 1)
            for r in range(nslab):
                blk = s_ref[r * rows:(r + 1) * rows, :]
                s_ref[r * rows:(r + 1) * rows, :] = jnp.where(kpos + r * rows <= qpos, blk, neg)
        nacc = 4
        mx = [s_ref[r * rows:(r + 1) * rows, :] for r in range(nacc)]
        for r in range(nacc, nslab):
            mx[r % nacc] = jnp.maximum(mx[r % nacc], s_ref[r * rows:(r + 1) * rows, :])
        mt = jnp.maximum(jnp.maximum(mx[0], mx[1]), jnp.maximum(mx[2], mx[3]))
        m_old = m_ref[...]
        m_new = jnp.maximum(m_old, jnp.max(mt, axis=0, keepdims=True))
        alpha = jnp.exp2(m_old - m_new)
        m_ref[...] = m_new
        a_refs[slot][...] = alpha
        mb = jnp.broadcast_to(m_new, (rows, t))
        ls = [None] * nacc
        for r in range(nslab):
            p = jnp.exp2(s_ref[r * rows:(r + 1) * rows, :] - mb)
            ls[r % nacc] = p if ls[r % nacc] is None else ls[r % nacc] + p
            p_ref[r * rows:(r + 1) * rows, :] = p.astype(BF16)
        lt = (ls[0] + ls[1]) + (ls[2] + ls[3])
        l_ref[...] = alpha * l_ref[...] + jnp.sum(lt, axis=0, keepdims=True)

    def pv(tile, slot):
        start = pl.multiple_of(tile * t, t)
        acc_ref[...] = acc_ref[...] * a_refs[slot][...] + _dot(vt_ref[:, pl.ds(start, t)],
                                                                p_refs[slot][...])

    last = jnp.maximum(nfull - 1, 0)
    qk(qi, 0)
    qk(0, 1)
    softmax(0, True)

    def pair(jj, _):
        j = 1 + 2 * jj
        qk(jnp.minimum(j, last), 0)
        softmax(1, False)
        pv(jnp.where(jj == 0, qi, j - 2), 0)
        qk(jnp.minimum(j + 1, last), 1)
        softmax(0, False)
        pv(j - 1, 1)
        return 0

    lax.fori_loop(0, nfull // 2, pair, 0)

    @pl.when(nfull % 2 == 1)
    def _():
        softmax(1, False)
        pv(jnp.where(nfull == 1, qi, nfull - 2), 0)
        pv(nfull - 1, 1)

    @pl.when(nfull % 2 == 0)
    def _():
        pv(jnp.where(nfull == 0, qi, nfull - 1), 0)

    o_t = acc_ref[...] * (1.0 / l_ref[...])
    o_ref[...] = o_t.T.astype(o_ref.dtype)


def _attention(q, k, vt, *, t=512):
    s = q.shape[0]
    return pl.pallas_call(
        functools.partial(_attn_body, t=t),
        grid=(MLA_HEADS, s // t),
        in_specs=[pl.BlockSpec((t, QK_PAD), lambda hd, i: (i, hd)),
                  pl.BlockSpec((s, QK_PAD), lambda hd, i: (0, hd)),
                  pl.BlockSpec((V_HEAD, s), lambda hd, i: (hd, 0))],
        out_specs=pl.BlockSpec((t, V_HEAD), lambda hd, i: (i, hd)),
        out_shape=jax.ShapeDtypeStruct((s, MLA_HEADS * V_HEAD), BF16),
        scratch_shapes=[
            pltpu.VMEM((QK_PAD, t), BF16),
            pltpu.VMEM((t, t), F32), pltpu.VMEM((t, t), F32),
            pltpu.VMEM((t, t), BF16), pltpu.VMEM((t, t), BF16),
            pltpu.VMEM((1, t), F32), pltpu.VMEM((1, t), F32),
            pltpu.VMEM((1, t), F32),
            pltpu.VMEM((1, t), F32),
            pltpu.VMEM((V_HEAD, t), F32),
        ],
        compiler_params=_params("parallel", "parallel"),
        name="mla_attn",
    )(q, k, vt)


def _out_proj_body(h_ref, o_ref_in, g_ref, wo_ref, o_ref):
    o_ref[...] = h_ref[...] + _rms(_dot(o_ref_in[...], wo_ref[...]), g_ref[1:2, :])


def _out_proj(h, o, g, w_o, *, tm=512):
    s = h.shape[0]
    return pl.pallas_call(
        _out_proj_body,
        grid=(s // tm,),
        in_specs=[_row_spec(tm, D_MODEL), _row_spec(tm, D_MODEL),
                  _const_spec(g.shape), _const_spec(w_o.shape)],
        out_specs=_row_spec(tm, D_MODEL),
        out_shape=jax.ShapeDtypeStruct(h.shape, F32),
        compiler_params=_params("parallel"),
        name="mla_out",
    )(h, o, g, w_o)


def _s5_disc_body(lr_ref, li_ref, ldt_ref, br_ref, bi_ref, abr_ref, abi_ref, bbr_ref, bbi_ref):
    dt = jnp.exp(ldt_ref[...])
    lr = lr_ref[...]
    li = li_ref[...]
    mag = jnp.exp(lr * dt)
    ab_re = mag * jnp.cos(li * dt)
    ab_im = mag * jnp.sin(li * dt)
    den = lr * lr + li * li
    f_re = ((ab_re - 1.0) * lr + ab_im * li) / den
    f_im = (ab_im * lr - (ab_re - 1.0) * li) / den
    abr_ref[...] = ab_re
    abi_ref[...] = ab_im
    br = br_ref[...]
    bi = bi_ref[...]
    bbr_ref[...] = f_re[:, None, :] * br - f_im[:, None, :] * bi
    bbi_ref[...] = f_re[:, None, :] * bi + f_im[:, None, :] * br


def _s5_discretise(lam_re, lam_im, log_dt, b_re_t, b_im_t):
    g, p = lam_re.shape
    c = b_re_t.shape[1]
    return pl.pallas_call(
        _s5_disc_body,
        out_shape=[jax.ShapeDtypeStruct((g, p), F32), jax.ShapeDtypeStruct((g, p), F32),
                   jax.ShapeDtypeStruct((g, c, p), F32), jax.ShapeDtypeStruct((g, c, p), F32)],
        name="s5_disc",
    )(lam_re, lam_im, log_dt.reshape(g, 1), b_re_t, b_im_t)


def _s5_body(h_ref, g_ref, ar_ref, ai_ref, bblk_ref, cblk_ref, dskip_ref, wa_ref, wb_ref, o_ref,
             zs_ref, zp_ref, bur_ref, bui_ref, xc_ref, yp_ref, y_ref, pr_ref, pi_ref, cr_ref,
             ci_ref, *, tm):
    seg = tm // SUBLANES
    i0 = pl.program_id(0)
    nst = S5_BLOCK_STATE

    @pl.when(i0 == 0)
    def _():
        cr_ref[...] = jnp.zeros(cr_ref.shape, F32)
        ci_ref[...] = jnp.zeros(ci_ref.shape, F32)
        ar = ar_ref[...]
        ai = ai_ref[...]
        pr_ref[0:1, :] = ar
        pi_ref[0:1, :] = ai

        def pw(j, carry):
            r, im = carry
            r, im = r * ar - im * ai, r * ai + im * ar
            pr_ref[pl.ds(j, 1), :] = r
            pi_ref[pl.ds(j, 1), :] = im
            return r, im

        lax.fori_loop(1, seg, pw, (ar, ai))

    h = h_ref[...]
    z = _rms(h, g_ref[0:1, :])
    for cb in range(S5_BLOCKS):
        zs_ref[cb] = z[:, cb * LANES:(cb + 1) * LANES]

    def regroup(i, _):
        for cb in range(S5_BLOCKS):
            zp_ref[cb, pl.ds(pl.multiple_of(i * SUBLANES, SUBLANES), SUBLANES), :] = (
                zs_ref[cb, pl.ds(i, SUBLANES, stride=seg), :])
        return 0

    lax.fori_loop(0, seg, regroup, 0)

    for cb in range(S5_BLOCKS):
        st = slice(cb * nst, (cb + 1) * nst)
        ub = zp_ref[cb].astype(BF16)
        bu = _dot(ub, bblk_ref[cb])
        bur_ref[...] = bu[:, :nst]
        bui_ref[...] = bu[:, nst:]
        ar = jnp.broadcast_to(ar_ref[:, st], (SUBLANES, nst))
        ai = jnp.broadcast_to(ai_ref[:, st], (SUBLANES, nst))

        def scan(i, carry):
            xr, xi = carry
            rows = pl.ds(pl.multiple_of(i * SUBLANES, SUBLANES), SUBLANES)
            nr = ar * xr - ai * xi + bur_ref[rows, :]
            ni = ar * xi + ai * xr + bui_ref[rows, :]
            bur_ref[rows, :] = nr
            bui_ref[rows, :] = ni
            return nr, ni

        zero = jnp.zeros((SUBLANES, nst), F32)
        er, ei = lax.fori_loop(0, seg, scan, (zero, zero))
        alr = pr_ref[seg - 1:seg, st]
        ali = pi_ref[seg - 1:seg, st]
        c_r = cr_ref[:, st]
        c_i = ci_ref[:, st]
        rows_r, rows_i = [], []
        for sgm in range(SUBLANES):
            rows_r.append(c_r)
            rows_i.append(c_i)
            e_r = er[sgm:sgm + 1, :]
            e_i = ei[sgm:sgm + 1, :]
            c_r, c_i = alr * c_r - ali * c_i + e_r, alr * c_i + ali * c_r + e_i
        cr_ref[:, st] = c_r
        ci_ref[:, st] = c_i
        cin_r = jnp.concatenate(rows_r, axis=0)
        cin_i = jnp.concatenate(rows_i, axis=0)

        def fix(i, _):
            rows = pl.ds(pl.multiple_of(i * SUBLANES, SUBLANES), SUBLANES)
            p_r = pr_ref[pl.ds(i, 1), st]
            p_i = pi_ref[pl.ds(i, 1), st]
            xr = bur_ref[rows, :] + (p_r * cin_r - p_i * cin_i)
            xi = bui_ref[rows, :] + (p_r * cin_i + p_i * cin_r)
            xc_ref[rows, 0:nst] = xr
            xc_ref[rows, nst:2 * nst] = xi
            return 0

        lax.fori_loop(0, seg, fix, 0)
        yp_ref[cb] = _dot(xc_ref[...].astype(BF16), cblk_ref[cb])

    for sgm in range(SUBLANES):
        def ungroup(j, _):
            dst = pl.ds(pl.multiple_of(sgm * seg + j * SUBLANES, SUBLANES), SUBLANES)
            src = pl.ds(j * SUBLANES * SUBLANES + sgm, SUBLANES, stride=SUBLANES)
            for cb in range(S5_BLOCKS):
                y_ref[dst, cb * LANES:(cb + 1) * LANES] = yp_ref[cb, src, :]
            return 0

        lax.fori_loop(0, seg // SUBLANES, ungroup, 0)

    y = _gelu(y_ref[...] + dskip_ref[...] * z).astype(BF16)
    out = _dot(y, wa_ref[...]) * jax.nn.sigmoid(_dot(y, wb_ref[...]))
    o_ref[...] = h + _rms(out, g_ref[1:2, :])


def _s5(h, g, ab_re, ab_im, bblk, cblk, dskip, w_a, w_b, *, tm=256):
    s = h.shape[0]
    seg = tm // SUBLANES
    nstate = S5_GROUPS * S5_STATE
    consts = (g, ab_re, ab_im, bblk, cblk, dskip, w_a, w_b)
    return pl.pallas_call(
        functools.partial(_s5_body, tm=tm),
        grid=(s // tm,),
        in_specs=[_row_spec(tm, D_MODEL)] + [_const_spec(a.shape) for a in consts],
        out_specs=_row_spec(tm, D_MODEL),
        out_shape=jax.ShapeDtypeStruct(h.shape, F32),
        scratch_shapes=[
            pltpu.VMEM((S5_BLOCKS, tm, LANES), F32),
            pltpu.VMEM((S5_BLOCKS, tm, LANES), F32),
            pltpu.VMEM((tm, S5_BLOCK_STATE), F32),
            pltpu.VMEM((tm, S5_BLOCK_STATE), F32),
            pltpu.VMEM((tm, 2 * S5_BLOCK_STATE), F32),
            pltpu.VMEM((S5_BLOCKS, tm, LANES), F32),
            pltpu.VMEM((tm, D_MODEL), F32),
            pltpu.VMEM((seg, nstate), F32),
            pltpu.VMEM((seg, nstate), F32),
            pltpu.VMEM((1, nstate), F32),
            pltpu.VMEM((1, nstate), F32),
        ],
        compiler_params=_params("arbitrary"),
        name="s5",
    )(h, *consts)


def _s5_block_matrices(bb_re, bb_im, c_re, c_im):
    nb, gb, c, p = S5_BLOCKS, S5_BLOCK_GROUPS, S5_GROUP_CH, S5_STATE
    eye = jnp.eye(gb, dtype=F32)

    def in_map(bb):
        t = bb.reshape(nb, gb, c, p)
        return jnp.einsum('bgcp,gk->bgckp', t, eye).reshape(nb, gb * c, gb * p)

    def out_map(cc):
        t = cc.reshape(nb, gb, c, p)
        return jnp.einsum('bgcp,gk->bgpkc', t, eye).reshape(nb, gb * p, gb * c)

    bblk = jnp.concatenate([in_map(bb_re), in_map(bb_im)], axis=2).astype(BF16)
    cblk = jnp.concatenate([out_map(c_re), out_map(-c_im)], axis=1).astype(BF16)
    return bblk, cblk


def _mla_weights(w_uq, w_dkv):
    wq = w_uq.reshape(Q_LORA, MLA_HEADS, QK_NOPE + QK_ROPE)
    w_n = wq[:, :, :QK_NOPE].reshape(Q_LORA, MLA_HEADS * QK_NOPE)
    w_r = jnp.pad(wq[:, :, QK_NOPE:], ((0, 0), (0, 0), (0, LANES - QK_ROPE)))
    w_r = w_r.reshape(Q_LORA, MLA_HEADS * LANES)
    w_dkv_p = jnp.pad(w_dkv, ((0, 0), (0, LANES - QK_ROPE)))
    return w_n.astype(BF16), w_r.astype(BF16), w_dkv_p.astype(BF16)


def kernel(x, positions, norm_g, w_up, w_down, a_w_in, a_b_in, a_g_v, a_b_v, a_w_s, a_b_s, a_w_out, b_w_grp, b_scale, c_w_dq, c_g_q, c_w_uq, c_w_dkv, c_g_kv, c_w_uk, c_w_uv, c_w_o, d_lam_re, d_lam_im, d_log_dt, d_b_re, d_b_im, d_c_re, d_c_im, d_skip, d_w_glu_a, d_w_glu_b):
    bsz, s, d = x.shape
    assert bsz == 1 and d == D_MODEL
    h = x.reshape(s, d)
    row = lambda a: a.reshape(1, -1)
    w_up_b = w_up.astype(BF16)
    w_down_b = w_down.astype(BF16)

    b_s_full = jnp.repeat(a_b_s[0].T, D_MODEL // GMLP_HEADS, axis=1)
    h = _gmlp(h, norm_g[0], a_w_in[0].astype(BF16), row(a_b_in[0]), row(a_g_v[0]), row(a_b_v[0]),
              a_w_s[0], b_s_full, a_w_out[0].astype(BF16))
    h = _mlp(h, norm_g[0], w_up_b[0], w_down_b[0])

    h = _pool(h, norm_g[1], b_w_grp[0].astype(BF16), row(b_scale[0]))
    h = _mlp(h, norm_g[1], w_up_b[1], w_down_b[1])

    w_uq_n, w_uq_r, w_dkv_p = _mla_weights(c_w_uq[0], c_w_dkv[0])
    inv_freq = ROPE_THETA ** (-jnp.arange(0, QK_ROPE, 2, dtype=F32) / QK_ROPE)
    invf = jnp.concatenate([inv_freq, inv_freq, jnp.zeros((LANES - QK_ROPE,), F32)]).reshape(1, LANES)
    q, k, v = _mla_proj(h, positions.reshape(s, 1), norm_g[2], c_w_dq[0].astype(BF16), row(c_g_q[0]),
                        w_uq_n, w_uq_r, w_dkv_p, row(c_g_kv[0]), c_w_uk[0].astype(BF16),
                        c_w_uv[0].astype(BF16), invf)
    o = _attention(q, k, v)
    h = _out_proj(h, o, norm_g[2], c_w_o[0].astype(BF16))
    h = _mlp(h, norm_g[2], w_up_b[2], w_down_b[2])

    tr = lambda a: jnp.swapaxes(a, 1, 2)
    ab_re, ab_im, bb_re, bb_im = _s5_discretise(d_lam_re[0], d_lam_im[0], d_log_dt[0],
                                                tr(d_b_re[0]), tr(d_b_im[0]))
    bblk, cblk = _s5_block_matrices(bb_re, bb_im, d_c_re[0], d_c_im[0])
    h = _s5(h, norm_g[3], row(ab_re), row(ab_im), bblk, cblk, row(d_skip[0]),
            d_w_glu_a[0].astype(BF16), d_w_glu_b[0].astype(BF16))
    h = _mlp(h, norm_g[3], w_up_b[3], w_down_b[3])
    return h.reshape(bsz, s, d)
```

```python
import functools
import math

import jax
import jax.numpy as jnp
from jax import lax
from jax.experimental import pallas as pl
from jax.experimental.pallas import tpu as pltpu

F32 = jnp.float32
BF16 = jnp.bfloat16

D_MODEL = 1024
D_FF = 4 * D_MODEL
RMS_EPS = 1e-6
LANES = 128
SUBLANES = 8
VMEM_LIMIT = 56 * 1024 * 1024

CHUNK = 128
GMLP_HEADS = 8
POOL_WINDOWS = (2, 4, 8, 16)
POOL_GROUP = D_MODEL // len(POOL_WINDOWS)
POOL_HALO = 16
MLA_HEADS = 8
Q_LORA = 384
KV_LORA = 256
QK_NOPE = 128
QK_ROPE = 64
V_HEAD = 128
ROPE_THETA = 10000.0
QK_PAD = 256
S5_GROUP_CH = 16
S5_GROUPS = D_MODEL // S5_GROUP_CH
S5_STATE = 64
S5_BLOCK_GROUPS = LANES // S5_GROUP_CH
S5_BLOCKS = S5_GROUPS // S5_BLOCK_GROUPS
S5_BLOCK_STATE = S5_BLOCK_GROUPS * S5_STATE


def _rms(x, g):
    return x * lax.rsqrt(jnp.mean(x * x, axis=-1, keepdims=True) + RMS_EPS) * g


def _gelu(x):
    c = math.sqrt(2.0 / math.pi)
    return 0.5 * x * (1.0 + jnp.tanh(c * (x + 0.044715 * (x * x * x))))


def _dot(a, b):
    return jnp.dot(a, b, preferred_element_type=F32)


def _const_spec(shape):
    nd = len(shape)
    return pl.BlockSpec(shape, lambda *_: (0,) * nd, pipeline_mode=pl.Buffered(1))


def _row_spec(tm, width):
    return pl.BlockSpec((tm, width), lambda i: (i, 0))


def _params(*sem, flags=None):
    return pltpu.CompilerParams(dimension_semantics=sem, vmem_limit_bytes=VMEM_LIMIT, flags=flags)


MLP_TF = 1024


def _mlp_tail(h, g_ref, wu_ref, wd_ref):
    z = _rms(h, g_ref[2:3, :]).astype(BF16)
    acc = jnp.zeros(h.shape, F32)
    for c in range(D_FF // MLP_TF):
        a = _dot(z, wu_ref[:, c * MLP_TF:(c + 1) * MLP_TF])
        a = jnp.square(jnp.maximum(a, 0.0)).astype(BF16)
        acc = acc + _dot(a, wd_ref[c * MLP_TF:(c + 1) * MLP_TF, :])
    return h + _rms(acc, g_ref[3:4, :])


def _mlp_weight_specs(w_up, w_down, layer):
    return [pl.BlockSpec((None,) + w.shape[1:], lambda *_: (layer, 0, 0), pipeline_mode=pl.Buffered(1))
            for w in (w_up, w_down)]


def _mlp_body(h_ref, g_ref, wu_ref, wd_ref, o_ref):
    o_ref[...] = _mlp_tail(h_ref[...], g_ref, wu_ref, wd_ref)


def _mlp(h, g, w_up, w_down, layer, *, tm=512):
    s = h.shape[0]
    return pl.pallas_call(
        _mlp_body,
        grid=(s // tm,),
        in_specs=[_row_spec(tm, D_MODEL), _const_spec(g.shape)] + _mlp_weight_specs(w_up, w_down, layer),
        out_specs=_row_spec(tm, D_MODEL),
        out_shape=jax.ShapeDtypeStruct(h.shape, F32),
        compiler_params=_params("parallel"),
        name="mlp",
    )(h, g, w_up, w_down)


def _gmlp_body(h_ref, g_ref, win_ref, bin_ref, gv_ref, bv_ref, ws_ref, bs_ref, wout_ref, o_ref,
               sv_ref, *, tm):
    h = h_ref[...]
    z = _rms(h, g_ref[0:1, :]).astype(BF16)
    uv = _gelu(_dot(z, win_ref[...]) + bin_ref[...])
    u = uv[:, :D_MODEL]
    v = uv[:, D_MODEL:]
    mu = jnp.mean(v, axis=-1, keepdims=True)
    vc = v - mu
    v = vc * lax.rsqrt(jnp.mean(vc * vc, axis=-1, keepdims=True) + RMS_EPS)
    v = (v * gv_ref[...] + bv_ref[...]).astype(BF16)
    row = lax.broadcasted_iota(jnp.int32, (CHUNK, CHUNK), 0)
    col = lax.broadcasted_iota(jnp.int32, (CHUNK, CHUNK), 1)
    nchunk = tm // CHUNK
    for hd in range(GMLP_HEADS):
        ws = jnp.where(col <= row, ws_ref[hd], 0.0).astype(BF16)
        cols = slice(hd * LANES, (hd + 1) * LANES)
        rhs = jnp.concatenate([v[c * CHUNK:(c + 1) * CHUNK, cols] for c in range(nchunk)], axis=1)
        sv = _dot(ws, rhs)
        for c in range(nchunk):
            sv_ref[c * CHUNK:(c + 1) * CHUNK, cols] = (
                sv[:, c * LANES:(c + 1) * LANES] + bs_ref[:, cols])
    y = (u * sv_ref[...]).astype(BF16)
    o_ref[...] = h + _rms(_dot(y, wout_ref[...]), g_ref[1:2, :])


def _gmlp(h, g, w_in, b_in, g_v, b_v, w_s, b_s_full, w_out, *, tm=256):
    s = h.shape[0]
    return pl.pallas_call(
        functools.partial(_gmlp_body, tm=tm),
        grid=(s // tm,),
        in_specs=[_row_spec(tm, D_MODEL), _const_spec(g.shape), _const_spec(w_in.shape),
                  _const_spec(b_in.shape), _const_spec(g_v.shape), _const_spec(b_v.shape),
                  _const_spec(w_s.shape), _const_spec(b_s_full.shape), _const_spec(w_out.shape)],
        out_specs=_row_spec(tm, D_MODEL),
        out_shape=jax.ShapeDtypeStruct(h.shape, F32),
        scratch_shapes=[pltpu.VMEM((tm, D_MODEL), F32)],
        compiler_params=_params("parallel"),
        name="gmlp",
    )(h, g, w_in, b_in, g_v, b_v, w_s, b_s_full, w_out)


def _pool_body(h_ref, halo_ref, g_ref, w_ref, scale_ref, wu_ref, wd_ref, o_ref, zz_ref, *, tm):
    i = pl.program_id(0)
    h = h_ref[...]
    z = _rms(h, g_ref[0:1, :])
    zh = _rms(halo_ref[...], g_ref[0:1, :])
    zz_ref[0:POOL_HALO, :] = jnp.where(i > 0, zh, 0.0)
    zz_ref[POOL_HALO:, :] = z
    pos = i * tm + lax.broadcasted_iota(jnp.int32, (tm, 1), 0)
    outs = []
    for gi, w in enumerate(POOL_WINDOWS):
        cols = slice(gi * POOL_GROUP, (gi + 1) * POOL_GROUP)
        xg = z[:, cols]
        win = xg
        for k in range(1, w):
            win = win + zz_ref[POOL_HALO - k:POOL_HALO - k + tm, cols]
        count = jnp.minimum(pos + 1, w).astype(F32)
        diff = (win / count - xg).astype(BF16)
        outs.append(_dot(diff, w_ref[gi]))
    out = jnp.concatenate(outs, axis=-1) * scale_ref[...]
    o_ref[...] = _mlp_tail(h + _rms(out, g_ref[1:2, :]), g_ref, wu_ref, wd_ref)


def _pool_mlp(h, g, w_grp, scale, w_up, w_down, layer, *, tm=512):
    s = h.shape[0]
    ratio = tm // POOL_HALO
    return pl.pallas_call(
        functools.partial(_pool_body, tm=tm),
        grid=(s // tm,),
        in_specs=[_row_spec(tm, D_MODEL),
                  pl.BlockSpec((POOL_HALO, D_MODEL), lambda i: (jnp.maximum(i * ratio - 1, 0), 0)),
                  _const_spec(g.shape), _const_spec(w_grp.shape), _const_spec(scale.shape)]
        + _mlp_weight_specs(w_up, w_down, layer),
        out_specs=_row_spec(tm, D_MODEL),
        out_shape=jax.ShapeDtypeStruct(h.shape, F32),
        scratch_shapes=[pltpu.VMEM((tm + POOL_HALO, D_MODEL), F32)],
        compiler_params=_params("parallel"),
        name="pool_mlp",
    )(h, h, g, w_grp, scale, w_up, w_down)


def _rope_tile(x, cos, sin_lo, sin_hi):
    n = x.shape[1] // LANES
    outs = []
    for t in range(n):
        xt = x[:, t * LANES:(t + 1) * LANES]
        up = pltpu.roll(xt, 32, axis=1)
        down = pltpu.roll(xt, LANES - 32, axis=1)
        outs.append(xt * cos + down * sin_lo + up * sin_hi)
    return outs


def _mla_proj_body(h_ref, pos_ref, g_ref, wdq_ref, gq_ref, wuqn_ref, wuqr_ref, wdkv_ref, gkv_ref,
                   wuk_ref, wuv_ref, invf_ref, q_ref, k_ref, v_ref, *, scale):
    h = h_ref[...]
    z = _rms(h, g_ref[0:1, :]).astype(BF16)
    ang = pos_ref[...].astype(F32) * invf_ref[...]
    lane = lax.broadcasted_iota(jnp.int32, ang.shape, 1)
    cos = jnp.where(lane < QK_ROPE, jnp.cos(ang), 0.0)
    sin = jnp.sin(ang)
    sin_lo = jnp.where(lane < QK_ROPE // 2, -sin, 0.0)
    sin_hi = jnp.where((lane >= QK_ROPE // 2) & (lane < QK_ROPE), sin, 0.0)

    ql = _rms(_dot(z, wdq_ref[...]), gq_ref[...]).astype(BF16)
    qn = _dot(ql, wuqn_ref[...]) * scale
    qr = _rope_tile(_dot(ql, wuqr_ref[...]) * scale, cos, sin_lo, sin_hi)
    ckv = _dot(z, wdkv_ref[...])
    c = _rms(ckv[:, :KV_LORA], gkv_ref[...]).astype(BF16)
    kr = _rope_tile(ckv[:, KV_LORA:], cos, sin_lo, sin_hi)[0].astype(BF16)
    kn = _dot(c, wuk_ref[...])
    for hd in range(MLA_HEADS):
        cols = slice(hd * LANES, (hd + 1) * LANES)
        q_ref[:, hd * QK_PAD:hd * QK_PAD + LANES] = qn[:, cols].astype(BF16)
        q_ref[:, hd * QK_PAD + LANES:(hd + 1) * QK_PAD] = qr[hd].astype(BF16)
        k_ref[:, hd * QK_PAD:hd * QK_PAD + LANES] = kn[:, cols].astype(BF16)
        k_ref[:, hd * QK_PAD + LANES:(hd + 1) * QK_PAD] = kr
    v_ref[...] = _dot(c, wuv_ref[...]).T.astype(BF16)


def _mla_proj(h, pos, g, w_dq, g_q, w_uq_n, w_uq_r, w_dkv, g_kv, w_uk, w_uv, invf, *, tm=512):
    s = h.shape[0]
    scale = (QK_NOPE + QK_ROPE) ** -0.5 * math.log2(math.e)
    consts = (g, w_dq, g_q, w_uq_n, w_uq_r, w_dkv, g_kv, w_uk, w_uv, invf)
    return pl.pallas_call(
        functools.partial(_mla_proj_body, scale=scale),
        grid=(s // tm,),
        in_specs=[_row_spec(tm, D_MODEL), _row_spec(tm, 1)] + [_const_spec(a.shape) for a in consts],
        out_specs=[_row_spec(tm, MLA_HEADS * QK_PAD), _row_spec(tm, MLA_HEADS * QK_PAD),
                   pl.BlockSpec((MLA_HEADS * V_HEAD, tm), lambda i: (0, i))],
        out_shape=[jax.ShapeDtypeStruct((s, MLA_HEADS * QK_PAD), BF16),
                   jax.ShapeDtypeStruct((s, MLA_HEADS * QK_PAD), BF16),
                   jax.ShapeDtypeStruct((MLA_HEADS * V_HEAD, s), BF16)],
        compiler_params=_params("parallel"),
        name="mla_proj",
    )(h, pos, *consts)


def _attn_body(q_ref, k_ref, vt_ref, o_ref, qt_ref, s0_ref, s1_ref, p0_ref, p1_ref, a0_ref, a1_ref,
               mt0_ref, mt1_ref, m_ref, l_ref, acc_ref, *, t):
    qi = pl.program_id(1)
    nfull = qi
    s_refs = (s0_ref, s1_ref)
    p_refs = (p0_ref, p1_ref)
    a_refs = (a0_ref, a1_ref)
    mt_refs = (mt0_ref, mt1_ref)
    neg = -1e30
    rows = 2 * SUBLANES

    qt_ref[...] = q_ref[...].astype(F32).T.astype(BF16)
    m_ref[...] = jnp.full(m_ref.shape, neg, F32)
    l_ref[...] = jnp.zeros(l_ref.shape, F32)
    acc_ref[...] = jnp.zeros(acc_ref.shape, F32)

    def qk(tile, slot, masked=False):
        start = pl.multiple_of(tile * t, t)
        s = _dot(k_ref[pl.ds(start, t), :], qt_ref[...])
        if masked:
            kpos = lax.broadcasted_iota(jnp.int32, (t, t), 0)
            qpos = lax.broadcasted_iota(jnp.int32, (t, t), 1)
            s = jnp.where(kpos <= qpos, s, neg)
        s_refs[slot][...] = s
        nacc = 2
        mx = [s[r * SUBLANES:(r + 1) * SUBLANES, :] for r in range(nacc)]
        for r in range(nacc, t // SUBLANES):
            mx[r % nacc] = jnp.maximum(mx[r % nacc], s[r * SUBLANES:(r + 1) * SUBLANES, :])
        mt_refs[slot][...] = jnp.maximum(mx[0], mx[1])

    def softmax(slot):
        s_ref = s_refs[slot]
        p_ref = p_refs[slot]
        mt = mt_refs[slot][...]
        m_old = m_ref[...]
        m_new = jnp.maximum(m_old, jnp.max(mt, axis=0, keepdims=True))
        alpha = jnp.exp2(m_old - m_new)
        m_ref[...] = m_new
        a_refs[slot][...] = alpha
        mb = jnp.broadcast_to(m_new, (SUBLANES, t))
        nacc = 2
        ls = [None] * nacc
        for r in range(t // rows):
            lo = jnp.exp2(s_ref[r * rows:r * rows + SUBLANES, :] - mb)
            hi = jnp.exp2(s_ref[r * rows + SUBLANES:(r + 1) * rows, :] - mb)
            both = lo + hi
            ls[r % nacc] = both if ls[r % nacc] is None else ls[r % nacc] + both
            p_ref[r * rows:(r + 1) * rows, :] = jnp.concatenate([lo, hi], axis=0).astype(BF16)
        l_ref[...] = alpha * l_ref[...] + jnp.sum(ls[0] + ls[1], axis=0, keepdims=True)

    def pv(tile, slot):
        start = pl.multiple_of(tile * t, t)
        acc_ref[...] = acc_ref[...] * a_refs[slot][...] + _dot(vt_ref[:, pl.ds(start, t)],
                                                                p_refs[slot][...])

    last = jnp.maximum(nfull - 1, 0)
    qk(qi, 0, masked=True)
    qk(0, 1)
    softmax(0)

    def stage(j, slot):
        softmax(slot)
        qk(jnp.minimum(j, last), 1 - slot)
        pv(jnp.where(j == 1, qi, j - 2), 1 - slot)

    unroll = 4

    def body(jq, _):
        for u in range(unroll):
            stage(1 + unroll * jq + u, (1 + u) % 2)
        return 0

    lax.fori_loop(0, nfull // unroll, body, 0)
    done = 1 + unroll * (nfull // unroll)

    @pl.when(nfull % unroll >= 2)
    def _():
        stage(done, 1)
        stage(done + 1, 0)

    @pl.when(nfull % 2 == 1)
    def _():
        softmax(1)
        pv(jnp.where(nfull == 1, qi, nfull - 2), 0)
        pv(nfull - 1, 1)

    @pl.when(nfull % 2 == 0)
    def _():
        pv(jnp.where(nfull == 0, qi, nfull - 1), 0)

    o_t = acc_ref[...] * (1.0 / l_ref[...])
    o_ref[...] = o_t.T.astype(o_ref.dtype)


def _attention(q, k, vt, *, t=512):
    s = q.shape[0]
    return pl.pallas_call(
        functools.partial(_attn_body, t=t),
        grid=(MLA_HEADS, s // t),
        in_specs=[pl.BlockSpec((t, QK_PAD), lambda hd, i: (i, hd)),
                  pl.BlockSpec((s, QK_PAD), lambda hd, i: (0, hd)),
                  pl.BlockSpec((V_HEAD, s), lambda hd, i: (hd, 0))],
        out_specs=pl.BlockSpec((t, V_HEAD), lambda hd, i: (i, hd)),
        out_shape=jax.ShapeDtypeStruct((s, MLA_HEADS * V_HEAD), BF16),
        scratch_shapes=[
            pltpu.VMEM((QK_PAD, t), BF16),
            pltpu.VMEM((t, t), F32), pltpu.VMEM((t, t), F32),
            pltpu.VMEM((t, t), BF16), pltpu.VMEM((t, t), BF16),
            pltpu.VMEM((1, t), F32), pltpu.VMEM((1, t), F32),
            pltpu.VMEM((SUBLANES, t), F32), pltpu.VMEM((SUBLANES, t), F32),
            pltpu.VMEM((1, t), F32),
            pltpu.VMEM((1, t), F32),
            pltpu.VMEM((V_HEAD, t), F32),
        ],
        compiler_params=_params("parallel", "parallel"),
        name="mla_attn",
    )(q, k, vt)


def _out_proj_body(h_ref, o_ref_in, g_ref, wo_ref, wu_ref, wd_ref, o_ref):
    h = h_ref[...] + _rms(_dot(o_ref_in[...], wo_ref[...]), g_ref[1:2, :])
    o_ref[...] = _mlp_tail(h, g_ref, wu_ref, wd_ref)


def _out_proj_mlp(h, o, g, w_o, w_up, w_down, layer, *, tm=512):
    s = h.shape[0]
    return pl.pallas_call(
        _out_proj_body,
        grid=(s // tm,),
        in_specs=[_row_spec(tm, D_MODEL), _row_spec(tm, D_MODEL),
                  _const_spec(g.shape), _const_spec(w_o.shape)]
        + _mlp_weight_specs(w_up, w_down, layer),
        out_specs=_row_spec(tm, D_MODEL),
        out_shape=jax.ShapeDtypeStruct(h.shape, F32),
        compiler_params=_params("parallel"),
        name="mla_out_mlp",
    )(h, o, g, w_o, w_up, w_down)


def _s5_disc_body(lr_ref, li_ref, ldt_ref, br_ref, bi_ref, abr_ref, abi_ref, bbr_ref, bbi_ref):
    dt = jnp.exp(ldt_ref[...])
    lr = lr_ref[...]
    li = li_ref[...]
    mag = jnp.exp(lr * dt)
    ab_re = mag * jnp.cos(li * dt)
    ab_im = mag * jnp.sin(li * dt)
    den = lr * lr + li * li
    f_re = ((ab_re - 1.0) * lr + ab_im * li) / den
    f_im = (ab_im * lr - (ab_re - 1.0) * li) / den
    abr_ref[...] = ab_re
    abi_ref[...] = ab_im
    br = br_ref[...]
    bi = bi_ref[...]
    bbr_ref[...] = f_re[:, None, :] * br - f_im[:, None, :] * bi
    bbi_ref[...] = f_re[:, None, :] * bi + f_im[:, None, :] * br


def _s5_discretise(lam_re, lam_im, log_dt, b_re_t, b_im_t):
    g, p = lam_re.shape
    c = b_re_t.shape[1]
    return pl.pallas_call(
        _s5_disc_body,
        out_shape=[jax.ShapeDtypeStruct((g, p), F32), jax.ShapeDtypeStruct((g, p), F32),
                   jax.ShapeDtypeStruct((g, c, p), F32), jax.ShapeDtypeStruct((g, c, p), F32)],
        name="s5_disc",
    )(lam_re, lam_im, log_dt.reshape(g, 1), b_re_t, b_im_t)


def _s5_body(h_ref, g_ref, perm_ref, permt_ref, ar_ref, ai_ref, bblk_ref, cblk_ref, dskip_ref,
             wa_ref, wb_ref, o_ref, xr0_ref, xi0_ref, xr1_ref, xi1_ref, xc0_ref, xc1_ref,
             pr_ref, pi_ref, cr_ref, ci_ref, *, tm):
    seg = tm // SUBLANES
    i0 = pl.program_id(0)
    nst = S5_BLOCK_STATE
    pack = 2 * SUBLANES

    @pl.when(i0 == 0)
    def _():
        cr_ref[...] = jnp.zeros(cr_ref.shape, F32)
        ci_ref[...] = jnp.zeros(ci_ref.shape, F32)
        ar = ar_ref[...]
        ai = ai_ref[...]
        bcast = lambda v: jnp.broadcast_to(v, (SUBLANES, v.shape[1]))
        pr_ref[0:SUBLANES, :] = bcast(ar)
        pi_ref[0:SUBLANES, :] = bcast(ai)

        def pw(j, carry):
            r, im = carry
            r, im = r * ar - im * ai, r * ai + im * ar
            rows = pl.ds(pl.multiple_of(j * SUBLANES, SUBLANES), SUBLANES)
            pr_ref[rows, :] = bcast(r)
            pi_ref[rows, :] = bcast(im)
            return r, im

        lax.fori_loop(1, seg, pw, (ar, ai))

    h = h_ref[...]
    z = _rms(h, g_ref[0:1, :])
    zp = _dot(perm_ref[...], z.astype(BF16)).astype(BF16)

    yps = []
    for cb in range(S5_BLOCKS):
        st = slice(cb * nst, (cb + 1) * nst)
        xr_ref, xi_ref, xc_ref = ((xr0_ref, xi0_ref, xc0_ref), (xr1_ref, xi1_ref, xc1_ref))[cb % 2]
        bu = _dot(zp[:, cb * LANES:(cb + 1) * LANES], bblk_ref[cb])
        ar = jnp.broadcast_to(ar_ref[:, st], (SUBLANES, nst))
        ai = jnp.broadcast_to(ai_ref[:, st], (SUBLANES, nst))
        xr = jnp.zeros((SUBLANES, nst), F32)
        xi = xr
        for i in range(seg):
            rows = slice(i * SUBLANES, (i + 1) * SUBLANES)
            xr, xi = (ar * xr - ai * xi + bu[rows, :nst], ar * xi + ai * xr + bu[rows, nst:])
            xr_ref[rows, :] = xr
            xi_ref[rows, :] = xi
        alr = pr_ref[tm - 1:tm, st]
        ali = pi_ref[tm - 1:tm, st]
        c_r = cr_ref[:, st]
        c_i = ci_ref[:, st]
        rows_r, rows_i = [], []
        for sgm in range(SUBLANES):
            rows_r.append(c_r)
            rows_i.append(c_i)
            e_r = xr[sgm:sgm + 1, :]
            e_i = xi[sgm:sgm + 1, :]
            c_r, c_i = alr * c_r - ali * c_i + e_r, alr * c_i + ali * c_r + e_i
        cr_ref[:, st] = c_r
        ci_ref[:, st] = c_i
        cin_r = jnp.concatenate(rows_r, axis=0)
        cin_i = jnp.concatenate(rows_i, axis=0)
        for i2 in range(seg // 2):
            fr, fi = [], []
            for i in (2 * i2, 2 * i2 + 1):
                rows = slice(i * SUBLANES, (i + 1) * SUBLANES)
                p_r = pr_ref[rows, st]
                p_i = pi_ref[rows, st]
                fr.append(xr_ref[rows, :] + (p_r * cin_r - p_i * cin_i))
                fi.append(xi_ref[rows, :] + (p_r * cin_i + p_i * cin_r))
            rows2 = slice(i2 * pack, (i2 + 1) * pack)
            xc_ref[rows2, 0:nst] = jnp.concatenate(fr, axis=0).astype(BF16)
            xc_ref[rows2, nst:2 * nst] = jnp.concatenate(fi, axis=0).astype(BF16)
        yps.append(_dot(xc_ref[...], cblk_ref[cb]))

    yp = jnp.concatenate(yps, axis=1)
    yp_hi = yp.astype(BF16)
    yp_lo = (yp - yp_hi.astype(F32)).astype(BF16)
    y = _dot(permt_ref[...], yp_hi) + _dot(permt_ref[...], yp_lo)
    y = _gelu(y + dskip_ref[...] * z).astype(BF16)
    out = _dot(y, wa_ref[...]) * jax.nn.sigmoid(_dot(y, wb_ref[...]))
    o_ref[...] = h + _rms(out, g_ref[1:2, :])


def _s5_permutation(tm):
    seg = tm // SUBLANES
    r = jnp.arange(tm)
    src = (r % SUBLANES) * seg + r // SUBLANES
    return (src[:, None] == jnp.arange(tm)[None, :]).astype(BF16)


def _s5(h, g, ab_re, ab_im, bblk, cblk, dskip, w_a, w_b, *, tm=256):
    s = h.shape[0]
    seg = tm // SUBLANES
    nstate = S5_GROUPS * S5_STATE
    perm = _s5_permutation(tm)
    consts = (g, perm, perm.T, ab_re, ab_im, bblk, cblk, dskip, w_a, w_b)
    state = pltpu.VMEM((tm, S5_BLOCK_STATE), F32)
    return pl.pallas_call(
        functools.partial(_s5_body, tm=tm),
        grid=(s // tm,),
        in_specs=[_row_spec(tm, D_MODEL)] + [_const_spec(a.shape) for a in consts],
        out_specs=_row_spec(tm, D_MODEL),
        out_shape=jax.ShapeDtypeStruct(h.shape, F32),
        scratch_shapes=[
            state, state, state, state,
            pltpu.VMEM((tm, 2 * S5_BLOCK_STATE), BF16),
            pltpu.VMEM((tm, 2 * S5_BLOCK_STATE), BF16),
            pltpu.VMEM((tm, nstate), F32),
            pltpu.VMEM((tm, nstate), F32),
            pltpu.VMEM((1, nstate), F32),
            pltpu.VMEM((1, nstate), F32),
        ],
        compiler_params=_params("arbitrary"),
        name="s5",
    )(h, *consts)


def _s5_block_matrices(bb_re, bb_im, c_re, c_im):
    nb, gb, c, p = S5_BLOCKS, S5_BLOCK_GROUPS, S5_GROUP_CH, S5_STATE
    eye = jnp.eye(gb, dtype=F32)

    def in_map(bb):
        t = bb.reshape(nb, gb, c, p)
        return jnp.einsum('bgcp,gk->bgckp', t, eye).reshape(nb, gb * c, gb * p)

    def out_map(cc):
        t = cc.reshape(nb, gb, c, p)
        return jnp.einsum('bgcp,gk->bgpkc', t, eye).reshape(nb, gb * p, gb * c)

    bblk = jnp.concatenate([in_map(bb_re), in_map(bb_im)], axis=2).astype(BF16)
    cblk = jnp.concatenate([out_map(c_re), out_map(-c_im)], axis=1).astype(BF16)
    return bblk, cblk


def _mla_weights(w_uq, w_dkv):
    wq = w_uq.reshape(Q_LORA, MLA_HEADS, QK_NOPE + QK_ROPE)
    w_n = wq[:, :, :QK_NOPE].reshape(Q_LORA, MLA_HEADS * QK_NOPE)
    w_r = jnp.pad(wq[:, :, QK_NOPE:], ((0, 0), (0, 0), (0, LANES - QK_ROPE)))
    w_r = w_r.reshape(Q_LORA, MLA_HEADS * LANES)
    w_dkv_p = jnp.pad(w_dkv, ((0, 0), (0, LANES - QK_ROPE)))
    return w_n.astype(BF16), w_r.astype(BF16), w_dkv_p.astype(BF16)


def kernel(x, positions, norm_g, w_up, w_down, a_w_in, a_b_in, a_g_v, a_b_v, a_w_s, a_b_s, a_w_out, b_w_grp, b_scale, c_w_dq, c_g_q, c_w_uq, c_w_dkv, c_g_kv, c_w_uk, c_w_uv, c_w_o, d_lam_re, d_lam_im, d_log_dt, d_b_re, d_b_im, d_c_re, d_c_im, d_skip, d_w_glu_a, d_w_glu_b):
    bsz, s, d = x.shape
    assert bsz == 1 and d == D_MODEL
    h = x.reshape(s, d)
    row = lambda a: a.reshape(1, -1)
    w_up_b = w_up.astype(BF16)
    w_down_b = w_down.astype(BF16)

    b_s_full = jnp.repeat(a_b_s[0].T, D_MODEL // GMLP_HEADS, axis=1)
    h = _gmlp(h, norm_g[0], a_w_in[0].astype(BF16), row(a_b_in[0]), row(a_g_v[0]), row(a_b_v[0]),
              a_w_s[0], b_s_full, a_w_out[0].astype(BF16))
    h = _mlp(h, norm_g[0], w_up_b, w_down_b, 0)

    h = _pool_mlp(h, norm_g[1], b_w_grp[0].astype(BF16), row(b_scale[0]), w_up_b, w_down_b, 1)

    w_uq_n, w_uq_r, w_dkv_p = _mla_weights(c_w_uq[0], c_w_dkv[0])
    inv_freq = ROPE_THETA ** (-jnp.arange(0, QK_ROPE, 2, dtype=F32) / QK_ROPE)
    invf = jnp.concatenate([inv_freq, inv_freq, jnp.zeros((LANES - QK_ROPE,), F32)]).reshape(1, LANES)
    q, k, v = _mla_proj(h, positions.reshape(s, 1), norm_g[2], c_w_dq[0].astype(BF16), row(c_g_q[0]),
                        w_uq_n, w_uq_r, w_dkv_p, row(c_g_kv[0]), c_w_uk[0].astype(BF16),
                        c_w_uv[0].astype(BF16), invf)
    o = _attention(q, k, v)
    h = _out_proj_mlp(h, o, norm_g[2], c_w_o[0].astype(BF16), w_up_b, w_down_b, 2)

    tr = lambda a: jnp.swapaxes(a, 1, 2)
    ab_re, ab_im, bb_re, bb_im = _s5_discretise(d_lam_re[0], d_lam_im[0], d_log_dt[0],
                                                tr(d_b_re[0]), tr(d_b_im[0]))
    bblk, cblk = _s5_block_matrices(bb_re, bb_im, d_c_re[0], d_c_im[0])
    h = _s5(h, norm_g[3], row(ab_re), row(ab_im), bblk, cblk, row(d_skip[0]),
            d_w_glu_a[0].astype(BF16), d_w_glu_b[0].astype(BF16))
    h = _mlp(h, norm_g[3], w_up_b, w_down_b, 3)
    return h.reshape(bsz, s, d)
```

```python
import functools
import math

import jax
import jax.numpy as jnp
from jax import lax
from jax.experimental import pallas as pl
from jax.experimental.pallas import tpu as pltpu

F32 = jnp.float32
BF16 = jnp.bfloat16

D_MODEL = 1024
D_FF = 4 * D_MODEL
RMS_EPS = 1e-6
LANES = 128
SUBLANES = 8
VMEM_LIMIT = 56 * 1024 * 1024

CHUNK = 128
GMLP_HEADS = 8
POOL_WINDOWS = (2, 4, 8, 16)
POOL_GROUP = D_MODEL // len(POOL_WINDOWS)
POOL_HALO = 16
MLA_HEADS = 8
Q_LORA = 384
KV_LORA = 256
QK_NOPE = 128
QK_ROPE = 64
V_HEAD = 128
V_ROWS = V_HEAD + 16
ROPE_THETA = 10000.0
QK_PAD = 256
S5_GROUP_CH = 16
S5_GROUPS = D_MODEL // S5_GROUP_CH
S5_STATE = 64
S5_BLOCK_GROUPS = LANES // S5_GROUP_CH
S5_BLOCKS = S5_GROUPS // S5_BLOCK_GROUPS
S5_BLOCK_STATE = S5_BLOCK_GROUPS * S5_STATE


def _rms(x, g):
    return x * lax.rsqrt(jnp.mean(x * x, axis=-1, keepdims=True) + RMS_EPS) * g


def _gelu(x):
    c = math.sqrt(2.0 / math.pi)
    return 0.5 * x * (1.0 + jnp.tanh(c * (x + 0.044715 * (x * x * x))))


def _dot(a, b):
    return jnp.dot(a, b, preferred_element_type=F32)


def _const_spec(shape):
    nd = len(shape)
    return pl.BlockSpec(shape, lambda *_: (0,) * nd, pipeline_mode=pl.Buffered(1))


def _row_spec(tm, width):
    return pl.BlockSpec((tm, width), lambda i: (i, 0))


def _params(*sem, flags=None):
    return pltpu.CompilerParams(dimension_semantics=sem, vmem_limit_bytes=VMEM_LIMIT, flags=flags)


MLP_TF = 1024


def _mlp_tail(h, g_ref, wu_ref, wd_ref):
    z = _rms(h, g_ref[2:3, :]).astype(BF16)
    acc = jnp.zeros(h.shape, F32)
    for c in range(D_FF // MLP_TF):
        a = _dot(z, wu_ref[:, c * MLP_TF:(c + 1) * MLP_TF])
        a = jnp.square(jnp.maximum(a, 0.0)).astype(BF16)
        acc = acc + _dot(a, wd_ref[c * MLP_TF:(c + 1) * MLP_TF, :])
    return h + _rms(acc, g_ref[3:4, :])


def _mlp_weight_specs(w_up, w_down, layer):
    return [pl.BlockSpec((None,) + w.shape[1:], lambda *_: (layer, 0, 0), pipeline_mode=pl.Buffered(1))
            for w in (w_up, w_down)]


def _mlp_body(h_ref, g_ref, wu_ref, wd_ref, o_ref):
    o_ref[...] = _mlp_tail(h_ref[...], g_ref, wu_ref, wd_ref)


def _mlp(h, g, w_up, w_down, layer, *, tm=512):
    s = h.shape[0]
    return pl.pallas_call(
        _mlp_body,
        grid=(s // tm,),
        in_specs=[_row_spec(tm, D_MODEL), _const_spec(g.shape)] + _mlp_weight_specs(w_up, w_down, layer),
        out_specs=_row_spec(tm, D_MODEL),
        out_shape=jax.ShapeDtypeStruct(h.shape, F32),
        compiler_params=_params("parallel"),
        name="mlp",
    )(h, g, w_up, w_down)


def _gmlp_body(h_ref, g_ref, win_ref, bin_ref, gv_ref, bv_ref, ws_ref, bs_ref, wout_ref, o_ref,
               sv_ref, *, tm):
    h = h_ref[...]
    z = _rms(h, g_ref[0:1, :]).astype(BF16)
    uv = _gelu(_dot(z, win_ref[...]) + bin_ref[...])
    u = uv[:, :D_MODEL]
    v = uv[:, D_MODEL:]
    mu = jnp.mean(v, axis=-1, keepdims=True)
    vc = v - mu
    v = vc * lax.rsqrt(jnp.mean(vc * vc, axis=-1, keepdims=True) + RMS_EPS)
    v = (v * gv_ref[...] + bv_ref[...]).astype(BF16)
    row = lax.broadcasted_iota(jnp.int32, (CHUNK, CHUNK), 0)
    col = lax.broadcasted_iota(jnp.int32, (CHUNK, CHUNK), 1)
    nchunk = tm // CHUNK
    for hd in range(GMLP_HEADS):
        ws = jnp.where(col <= row, ws_ref[hd], 0.0).astype(BF16)
        cols = slice(hd * LANES, (hd + 1) * LANES)
        rhs = jnp.concatenate([v[c * CHUNK:(c + 1) * CHUNK, cols] for c in range(nchunk)], axis=1)
        sv = _dot(ws, rhs)
        for c in range(nchunk):
            sv_ref[c * CHUNK:(c + 1) * CHUNK, cols] = (
                sv[:, c * LANES:(c + 1) * LANES] + bs_ref[:, cols])
    y = (u * sv_ref[...]).astype(BF16)
    o_ref[...] = h + _rms(_dot(y, wout_ref[...]), g_ref[1:2, :])


def _gmlp(h, g, w_in, b_in, g_v, b_v, w_s, b_s_full, w_out, *, tm=512):
    s = h.shape[0]
    return pl.pallas_call(
        functools.partial(_gmlp_body, tm=tm),
        grid=(s // tm,),
        in_specs=[_row_spec(tm, D_MODEL), _const_spec(g.shape), _const_spec(w_in.shape),
                  _const_spec(b_in.shape), _const_spec(g_v.shape), _const_spec(b_v.shape),
                  _const_spec(w_s.shape), _const_spec(b_s_full.shape), _const_spec(w_out.shape)],
        out_specs=_row_spec(tm, D_MODEL),
        out_shape=jax.ShapeDtypeStruct(h.shape, F32),
        scratch_shapes=[pltpu.VMEM((tm, D_MODEL), F32)],
        compiler_params=_params("parallel"),
        name="gmlp",
    )(h, g, w_in, b_in, g_v, b_v, w_s, b_s_full, w_out)


def _pool_body(h_ref, halo_ref, g_ref, w_ref, scale_ref, wu_ref, wd_ref, o_ref, zz_ref, *, tm):
    i = pl.program_id(0)
    h = h_ref[...]
    z = _rms(h, g_ref[0:1, :])
    zh = _rms(halo_ref[...], g_ref[0:1, :])
    zz_ref[0:POOL_HALO, :] = jnp.where(i > 0, zh, 0.0)
    zz_ref[POOL_HALO:, :] = z
    pos = i * tm + lax.broadcasted_iota(jnp.int32, (tm, 1), 0)
    outs = []
    for gi, w in enumerate(POOL_WINDOWS):
        cols = slice(gi * POOL_GROUP, (gi + 1) * POOL_GROUP)
        xg = z[:, cols]
        win = xg
        for k in range(1, w):
            win = win + zz_ref[POOL_HALO - k:POOL_HALO - k + tm, cols]
        count = jnp.minimum(pos + 1, w).astype(F32)
        diff = (win / count - xg).astype(BF16)
        outs.append(_dot(diff, w_ref[gi]))
    out = jnp.concatenate(outs, axis=-1) * scale_ref[...]
    o_ref[...] = _mlp_tail(h + _rms(out, g_ref[1:2, :]), g_ref, wu_ref, wd_ref)


def _pool_mlp(h, g, w_grp, scale, w_up, w_down, layer, *, tm=512):
    s = h.shape[0]
    ratio = tm // POOL_HALO
    return pl.pallas_call(
        functools.partial(_pool_body, tm=tm),
        grid=(s // tm,),
        in_specs=[_row_spec(tm, D_MODEL),
                  pl.BlockSpec((POOL_HALO, D_MODEL), lambda i: (jnp.maximum(i * ratio - 1, 0), 0)),
                  _const_spec(g.shape), _const_spec(w_grp.shape), _const_spec(scale.shape)]
        + _mlp_weight_specs(w_up, w_down, layer),
        out_specs=_row_spec(tm, D_MODEL),
        out_shape=jax.ShapeDtypeStruct(h.shape, F32),
        scratch_shapes=[pltpu.VMEM((tm + POOL_HALO, D_MODEL), F32)],
        compiler_params=_params("parallel"),
        name="pool_mlp",
    )(h, h, g, w_grp, scale, w_up, w_down)


def _rope_tile(x, cos, sin_lo, sin_hi):
    n = x.shape[1] // LANES
    outs = []
    for t in range(n):
        xt = x[:, t * LANES:(t + 1) * LANES]
        up = pltpu.roll(xt, 32, axis=1)
        down = pltpu.roll(xt, LANES - 32, axis=1)
        outs.append(xt * cos + down * sin_lo + up * sin_hi)
    return outs


def _mla_proj_body(h_ref, pos_ref, g_ref, wdq_ref, gq_ref, wuqn_ref, wuqr_ref, wdkv_ref, gkv_ref,
                   wuk_ref, wuv_ref, invf_ref, q_ref, k_ref, v_ref, *, scale):
    h = h_ref[...]
    z = _rms(h, g_ref[0:1, :]).astype(BF16)
    ang = pos_ref[...].astype(F32) * invf_ref[...]
    lane = lax.broadcasted_iota(jnp.int32, ang.shape, 1)
    cos = jnp.where(lane < QK_ROPE, jnp.cos(ang), 0.0)
    sin = jnp.sin(ang)
    sin_lo = jnp.where(lane < QK_ROPE // 2, -sin, 0.0)
    sin_hi = jnp.where((lane >= QK_ROPE // 2) & (lane < QK_ROPE), sin, 0.0)

    ql = _rms(_dot(z, wdq_ref[...]), gq_ref[...]).astype(BF16)
    qn = _dot(ql, wuqn_ref[...]) * scale
    qr = _rope_tile(_dot(ql, wuqr_ref[...]) * scale, cos, sin_lo, sin_hi)
    ckv = _dot(z, wdkv_ref[...])
    c = _rms(ckv[:, :KV_LORA], gkv_ref[...]).astype(BF16)
    kr = _rope_tile(ckv[:, KV_LORA:], cos, sin_lo, sin_hi)[0].astype(BF16)
    kn = _dot(c, wuk_ref[...])
    for hd in range(MLA_HEADS):
        cols = slice(hd * LANES, (hd + 1) * LANES)
        q_ref[:, hd * QK_PAD:hd * QK_PAD + LANES] = qn[:, cols].astype(BF16)
        q_ref[:, hd * QK_PAD + LANES:(hd + 1) * QK_PAD] = qr[hd].astype(BF16)
        k_ref[:, hd * QK_PAD:hd * QK_PAD + LANES] = kn[:, cols].astype(BF16)
        k_ref[:, hd * QK_PAD + LANES:(hd + 1) * QK_PAD] = kr
    vt = _dot(c, wuv_ref[...]).T.astype(BF16)
    ones = jnp.ones((V_ROWS - V_HEAD, vt.shape[1]), BF16)
    for hd in range(MLA_HEADS):
        v_ref[hd * V_ROWS:hd * V_ROWS + V_HEAD, :] = vt[hd * V_HEAD:(hd + 1) * V_HEAD, :]
        v_ref[hd * V_ROWS + V_HEAD:(hd + 1) * V_ROWS, :] = ones


def _mla_proj(h, pos, g, w_dq, g_q, w_uq_n, w_uq_r, w_dkv, g_kv, w_uk, w_uv, invf, *, tm=512):
    s = h.shape[0]
    scale = (QK_NOPE + QK_ROPE) ** -0.5 * math.log2(math.e)
    consts = (g, w_dq, g_q, w_uq_n, w_uq_r, w_dkv, g_kv, w_uk, w_uv, invf)
    return pl.pallas_call(
        functools.partial(_mla_proj_body, scale=scale),
        grid=(s // tm,),
        in_specs=[_row_spec(tm, D_MODEL), _row_spec(tm, 1)] + [_const_spec(a.shape) for a in consts],
        out_specs=[_row_spec(tm, MLA_HEADS * QK_PAD), _row_spec(tm, MLA_HEADS * QK_PAD),
                   pl.BlockSpec((MLA_HEADS * V_ROWS, tm), lambda i: (0, i))],
        out_shape=[jax.ShapeDtypeStruct((s, MLA_HEADS * QK_PAD), BF16),
                   jax.ShapeDtypeStruct((s, MLA_HEADS * QK_PAD), BF16),
                   jax.ShapeDtypeStruct((MLA_HEADS * V_ROWS, s), BF16)],
        compiler_params=_params("parallel"),
        name="mla_proj",
    )(h, pos, *consts)


ATTN_SLOTS = 4
ATTN_UNROLL = 8


def _attn_body(q_ref, k_ref, vt_ref, o_ref, qt_ref, m_ref, acc_ref, *slot_refs, t):
    qi = pl.program_id(1)
    nfull = qi
    ns = ATTN_SLOTS
    s_refs, p_refs, a_refs, mt_refs = (slot_refs[i * ns:(i + 1) * ns] for i in range(4))
    neg = -1e30
    rows = 2 * SUBLANES

    qt_ref[...] = q_ref[...].astype(F32).T.astype(BF16)
    m_ref[...] = jnp.full(m_ref.shape, neg, F32)
    acc_ref[...] = jnp.zeros(acc_ref.shape, F32)

    def qk(tile, slot, masked=False):
        start = pl.multiple_of(tile * t, t)
        s = _dot(k_ref[pl.ds(start, t), :], qt_ref[...])
        if masked:
            kpos = lax.broadcasted_iota(jnp.int32, (t, t), 0)
            qpos = lax.broadcasted_iota(jnp.int32, (t, t), 1)
            s = jnp.where(kpos <= qpos, s, neg)
        s_refs[slot][...] = s
        nacc = 2
        mx = [s[r * SUBLANES:(r + 1) * SUBLANES, :] for r in range(nacc)]
        for r in range(nacc, t // SUBLANES):
            mx[r % nacc] = jnp.maximum(mx[r % nacc], s[r * SUBLANES:(r + 1) * SUBLANES, :])
        mt_refs[slot][...] = jnp.maximum(mx[0], mx[1])

    def softmax(slot):
        s_ref = s_refs[slot]
        p_ref = p_refs[slot]
        mt = mt_refs[slot][...]
        m_old = m_ref[...]
        m_new = jnp.maximum(m_old, jnp.max(mt, axis=0, keepdims=True))
        alpha = jnp.exp2(m_old - m_new)
        m_ref[...] = m_new
        a_refs[slot][...] = alpha
        mb = jnp.broadcast_to(m_new, (SUBLANES, t))
        for r in range(t // rows):
            lo = jnp.exp2(s_ref[r * rows:r * rows + SUBLANES, :] - mb)
            hi = jnp.exp2(s_ref[r * rows + SUBLANES:(r + 1) * rows, :] - mb)
            p_ref[r * rows:(r + 1) * rows, :] = jnp.concatenate([lo, hi], axis=0).astype(BF16)

    def pv(tile, slot):
        start = pl.multiple_of(tile * t, t)
        acc_ref[...] = acc_ref[...] * a_refs[slot][...] + _dot(vt_ref[:, pl.ds(start, t)],
                                                                p_refs[slot][...])

    last = jnp.maximum(nfull - 1, 0)
    qk(qi, 0, masked=True)
    qk(0, 1)
    softmax(0)

    def key_tile(j):
        return jnp.where(j == 0, qi, j - 1)

    def stage(j, slot, prefetch=True):
        softmax(slot)
        if prefetch:
            qk(jnp.minimum(j, last), (slot + 1) % ns)
        pv(key_tile(j - 1), (slot - 1) % ns)

    def run(j0, count, total=None):
        for u in range(count):
            stage(j0 + u, (1 + u) % ns, prefetch=total is None or u < total - 1)

    unroll = ATTN_UNROLL

    def body(jq, _):
        run(1 + unroll * jq, unroll)
        return 0

    lax.fori_loop(0, nfull // unroll, body, 0)
    done = 1 + unroll * (nfull // unroll)
    for blk in range(unroll // ns - 1):

        @pl.when(nfull % unroll >= (blk + 1) * ns)
        def _():
            run(done + blk * ns, ns)

    done = done + (nfull % unroll) // ns * ns
    for rem in range(ns):

        @pl.when(nfull % ns == rem)
        def _():
            run(done, rem, total=rem)
            pv(key_tile(nfull), rem % ns)

    acc = acc_ref[...]
    o_t = acc[:V_HEAD, :] * (1.0 / acc[V_HEAD:V_HEAD + 1, :])
    o_ref[...] = o_t.T.astype(o_ref.dtype)


def _attention(q, k, vt, *, t=512):
    s = q.shape[0]
    return pl.pallas_call(
        functools.partial(_attn_body, t=t),
        grid=(MLA_HEADS, s // t),
        in_specs=[pl.BlockSpec((t, QK_PAD), lambda hd, i: (i, hd)),
                  pl.BlockSpec((s, QK_PAD), lambda hd, i: (0, hd)),
                  pl.BlockSpec((V_ROWS, s), lambda hd, i: (hd, 0))],
        out_specs=pl.BlockSpec((t, V_HEAD), lambda hd, i: (i, hd)),
        out_shape=jax.ShapeDtypeStruct((s, MLA_HEADS * V_HEAD), BF16),
        scratch_shapes=[
            pltpu.VMEM((QK_PAD, t), BF16),
            pltpu.VMEM((1, t), F32),
            pltpu.VMEM((V_ROWS, t), F32),
        ] + [pltpu.VMEM((t, t), F32)] * ATTN_SLOTS
          + [pltpu.VMEM((t, t), BF16)] * ATTN_SLOTS
          + [pltpu.VMEM((1, t), F32)] * ATTN_SLOTS
          + [pltpu.VMEM((SUBLANES, t), F32)] * ATTN_SLOTS,
        compiler_params=_params("parallel", "parallel"),
        name="mla_attn",
    )(q, k, vt)


def _out_proj_body(h_ref, o_ref_in, g_ref, wo_ref, wu_ref, wd_ref, o_ref):
    h = h_ref[...] + _rms(_dot(o_ref_in[...], wo_ref[...]), g_ref[1:2, :])
    o_ref[...] = _mlp_tail(h, g_ref, wu_ref, wd_ref)


def _out_proj_mlp(h, o, g, w_o, w_up, w_down, layer, *, tm=512):
    s = h.shape[0]
    return pl.pallas_call(
        _out_proj_body,
        grid=(s // tm,),
        in_specs=[_row_spec(tm, D_MODEL), _row_spec(tm, D_MODEL),
                  _const_spec(g.shape), _const_spec(w_o.shape)]
        + _mlp_weight_specs(w_up, w_down, layer),
        out_specs=_row_spec(tm, D_MODEL),
        out_shape=jax.ShapeDtypeStruct(h.shape, F32),
        compiler_params=_params("parallel"),
        name="mla_out_mlp",
    )(h, o, g, w_o, w_up, w_down)


def _s5_disc_body(lr_ref, li_ref, ldt_ref, br_ref, bi_ref, abr_ref, abi_ref, bbr_ref, bbi_ref):
    dt = jnp.exp(ldt_ref[...])
    lr = lr_ref[...]
    li = li_ref[...]
    mag = jnp.exp(lr * dt)
    ab_re = mag * jnp.cos(li * dt)
    ab_im = mag * jnp.sin(li * dt)
    den = lr * lr + li * li
    f_re = ((ab_re - 1.0) * lr + ab_im * li) / den
    f_im = (ab_im * lr - (ab_re - 1.0) * li) / den
    abr_ref[...] = ab_re
    abi_ref[...] = ab_im
    br = br_ref[...]
    bi = bi_ref[...]
    bbr_ref[...] = f_re[:, None, :] * br - f_im[:, None, :] * bi
    bbi_ref[...] = f_re[:, None, :] * bi + f_im[:, None, :] * br


def _s5_discretise(lam_re, lam_im, log_dt, b_re_t, b_im_t):
    g, p = lam_re.shape
    c = b_re_t.shape[1]
    return pl.pallas_call(
        _s5_disc_body,
        out_shape=[jax.ShapeDtypeStruct((g, p), F32), jax.ShapeDtypeStruct((g, p), F32),
                   jax.ShapeDtypeStruct((g, c, p), F32), jax.ShapeDtypeStruct((g, c, p), F32)],
        name="s5_disc",
    )(lam_re, lam_im, log_dt.reshape(g, 1), b_re_t, b_im_t)


def _s5_body(h_ref, g_ref, perm_ref, permt_ref, ar_ref, ai_ref, bblk_ref, cblk_ref, dskip_ref,
             wa_ref, wb_ref, o_ref, xr0_ref, xi0_ref, xr1_ref, xi1_ref, xc0_ref, xc1_ref,
             pr_ref, pi_ref, cr_ref, ci_ref, *, tm):
    seg = tm // SUBLANES
    i0 = pl.program_id(0)
    nst = S5_BLOCK_STATE
    pack = 2 * SUBLANES

    @pl.when(i0 == 0)
    def _():
        cr_ref[...] = jnp.zeros(cr_ref.shape, F32)
        ci_ref[...] = jnp.zeros(ci_ref.shape, F32)
        ar = ar_ref[...]
        ai = ai_ref[...]
        bcast = lambda v: jnp.broadcast_to(v, (SUBLANES, v.shape[1]))
        pr_ref[0:SUBLANES, :] = bcast(ar)
        pi_ref[0:SUBLANES, :] = bcast(ai)

        def pw(j, carry):
            r, im = carry
            r, im = r * ar - im * ai, r * ai + im * ar
            rows = pl.ds(pl.multiple_of(j * SUBLANES, SUBLANES), SUBLANES)
            pr_ref[rows, :] = bcast(r)
            pi_ref[rows, :] = bcast(im)
            return r, im

        lax.fori_loop(1, seg, pw, (ar, ai))

    h = h_ref[...]
    z = _rms(h, g_ref[0:1, :])
    zp = _dot(perm_ref[...], z.astype(BF16)).astype(BF16)

    yps = []
    for cb in range(S5_BLOCKS):
        st = slice(cb * nst, (cb + 1) * nst)
        xr_ref, xi_ref, xc_ref = ((xr0_ref, xi0_ref, xc0_ref), (xr1_ref, xi1_ref, xc1_ref))[cb % 2]
        bu = _dot(zp[:, cb * LANES:(cb + 1) * LANES], bblk_ref[cb])
        ar = jnp.broadcast_to(ar_ref[:, st], (SUBLANES, nst))
        ai = jnp.broadcast_to(ai_ref[:, st], (SUBLANES, nst))
        xr = jnp.zeros((SUBLANES, nst), F32)
        xi = xr
        for i in range(seg):
            rows = slice(i * SUBLANES, (i + 1) * SUBLANES)
            xr, xi = (ar * xr - ai * xi + bu[rows, :nst], ar * xi + ai * xr + bu[rows, nst:])
            xr_ref[rows, :] = xr
            xi_ref[rows, :] = xi
        alr = pr_ref[tm - 1:tm, st]
        ali = pi_ref[tm - 1:tm, st]
        c_r = cr_ref[:, st]
        c_i = ci_ref[:, st]
        rows_r, rows_i = [], []
        for sgm in range(SUBLANES):
            rows_r.append(c_r)
            rows_i.append(c_i)
            e_r = xr[sgm:sgm + 1, :]
            e_i = xi[sgm:sgm + 1, :]
            c_r, c_i = alr * c_r - ali * c_i + e_r, alr * c_i + ali * c_r + e_i
        cr_ref[:, st] = c_r
        ci_ref[:, st] = c_i
        cin_r = jnp.concatenate(rows_r, axis=0)
        cin_i = jnp.concatenate(rows_i, axis=0)
        for i2 in range(seg // 2):
            fr, fi = [], []
            for i in (2 * i2, 2 * i2 + 1):
                rows = slice(i * SUBLANES, (i + 1) * SUBLANES)
                p_r = pr_ref[rows, st]
                p_i = pi_ref[rows, st]
                fr.append(xr_ref[rows, :] + (p_r * cin_r - p_i * cin_i))
                fi.append(xi_ref[rows, :] + (p_r * cin_i + p_i * cin_r))
            rows2 = slice(i2 * pack, (i2 + 1) * pack)
            xc_ref[rows2, 0:nst] = jnp.concatenate(fr, axis=0).astype(BF16)
            xc_ref[rows2, nst:2 * nst] = jnp.concatenate(fi, axis=0).astype(BF16)
        yps.append(_dot(xc_ref[...], cblk_ref[cb]))

    yp = jnp.concatenate(yps, axis=1)
    yp_hi = yp.astype(BF16)
    yp_lo = (yp - yp_hi.astype(F32)).astype(BF16)
    y = _dot(permt_ref[...], yp_hi) + _dot(permt_ref[...], yp_lo)
    y = _gelu(y + dskip_ref[...] * z).astype(BF16)
    out = _dot(y, wa_ref[...]) * jax.nn.sigmoid(_dot(y, wb_ref[...]))
    o_ref[...] = h + _rms(out, g_ref[1:2, :])


def _s5_permutation(tm):
    seg = tm // SUBLANES
    r = jnp.arange(tm)
    src = (r % SUBLANES) * seg + r // SUBLANES
    return (src[:, None] == jnp.arange(tm)[None, :]).astype(BF16)


def _s5(h, g, ab_re, ab_im, bblk, cblk, dskip, w_a, w_b, *, tm=512):
    s = h.shape[0]
    seg = tm // SUBLANES
    nstate = S5_GROUPS * S5_STATE
    perm = _s5_permutation(tm)
    consts = (g, perm, perm.T, ab_re, ab_im, bblk, cblk, dskip, w_a, w_b)
    state = pltpu.VMEM((tm, S5_BLOCK_STATE), F32)
    return pl.pallas_call(
        functools.partial(_s5_body, tm=tm),
        grid=(s // tm,),
        in_specs=[_row_spec(tm, D_MODEL)] + [_const_spec(a.shape) for a in consts],
        out_specs=_row_spec(tm, D_MODEL),
        out_shape=jax.ShapeDtypeStruct(h.shape, F32),
        scratch_shapes=[
            state, state, state, state,
            pltpu.VMEM((tm, 2 * S5_BLOCK_STATE), BF16),
            pltpu.VMEM((tm, 2 * S5_BLOCK_STATE), BF16),
            pltpu.VMEM((tm, nstate), F32),
            pltpu.VMEM((tm, nstate), F32),
            pltpu.VMEM((1, nstate), F32),
            pltpu.VMEM((1, nstate), F32),
        ],
        compiler_params=_params("arbitrary"),
        name="s5",
    )(h, *consts)


def _s5_block_matrices(bb_re, bb_im, c_re, c_im):
    nb, gb, c, p = S5_BLOCKS, S5_BLOCK_GROUPS, S5_GROUP_CH, S5_STATE
    eye = jnp.eye(gb, dtype=F32)

    def in_map(bb):
        t = bb.reshape(nb, gb, c, p)
        return jnp.einsum('bgcp,gk->bgckp', t, eye).reshape(nb, gb * c, gb * p)

    def out_map(cc):
        t = cc.reshape(nb, gb, c, p)
        return jnp.einsum('bgcp,gk->bgpkc', t, eye).reshape(nb, gb * p, gb * c)

    bblk = jnp.concatenate([in_map(bb_re), in_map(bb_im)], axis=2).astype(BF16)
    cblk = jnp.concatenate([out_map(c_re), out_map(-c_im)], axis=1).astype(BF16)
    return bblk, cblk


def _mla_weights(w_uq, w_dkv):
    wq = w_uq.reshape(Q_LORA, MLA_HEADS, QK_NOPE + QK_ROPE)
    w_n = wq[:, :, :QK_NOPE].reshape(Q_LORA, MLA_HEADS * QK_NOPE)
    w_r = jnp.pad(wq[:, :, QK_NOPE:], ((0, 0), (0, 0), (0, LANES - QK_ROPE)))
    w_r = w_r.reshape(Q_LORA, MLA_HEADS * LANES)
    w_dkv_p = jnp.pad(w_dkv, ((0, 0), (0, LANES - QK_ROPE)))
    return w_n.astype(BF16), w_r.astype(BF16), w_dkv_p.astype(BF16)


def kernel(x, positions, norm_g, w_up, w_down, a_w_in, a_b_in, a_g_v, a_b_v, a_w_s, a_b_s, a_w_out, b_w_grp, b_scale, c_w_dq, c_g_q, c_w_uq, c_w_dkv, c_g_kv, c_w_uk, c_w_uv, c_w_o, d_lam_re, d_lam_im, d_log_dt, d_b_re, d_b_im, d_c_re, d_c_im, d_skip, d_w_glu_a, d_w_glu_b):
    bsz, s, d = x.shape
    assert bsz == 1 and d == D_MODEL
    h = x.reshape(s, d)
    row = lambda a: a.reshape(1, -1)
    w_up_b = w_up.astype(BF16)
    w_down_b = w_down.astype(BF16)

    b_s_full = jnp.repeat(a_b_s[0].T, D_MODEL // GMLP_HEADS, axis=1)
    h = _gmlp(h, norm_g[0], a_w_in[0].astype(BF16), row(a_b_in[0]), row(a_g_v[0]), row(a_b_v[0]),
              a_w_s[0], b_s_full, a_w_out[0].astype(BF16))
    h = _mlp(h, norm_g[0], w_up_b, w_down_b, 0)

    h = _pool_mlp(h, norm_g[1], b_w_grp[0].astype(BF16), row(b_scale[0]), w_up_b, w_down_b, 1)

    w_uq_n, w_uq_r, w_dkv_p = _mla_weights(c_w_uq[0], c_w_dkv[0])
    inv_freq = ROPE_THETA ** (-jnp.arange(0, QK_ROPE, 2, dtype=F32) / QK_ROPE)
    invf = jnp.concatenate([inv_freq, inv_freq, jnp.zeros((LANES - QK_ROPE,), F32)]).reshape(1, LANES)
    q, k, v = _mla_proj(h, positions.reshape(s, 1), norm_g[2], c_w_dq[0].astype(BF16), row(c_g_q[0]),
                        w_uq_n, w_uq_r, w_dkv_p, row(c_g_kv[0]), c_w_uk[0].astype(BF16),
                        c_w_uv[0].astype(BF16), invf)
    o = _attention(q, k, v)
    h = _out_proj_mlp(h, o, norm_g[2], c_w_o[0].astype(BF16), w_up_b, w_down_b, 2)

    tr = lambda a: jnp.swapaxes(a, 1, 2)
    ab_re, ab_im, bb_re, bb_im = _s5_discretise(d_lam_re[0], d_lam_im[0], d_log_dt[0],
                                                tr(d_b_re[0]), tr(d_b_im[0]))
    bblk, cblk = _s5_block_matrices(bb_re, bb_im, d_c_re[0], d_c_im[0])
    h = _s5(h, norm_g[3], row(ab_re), row(ab_im), bblk, cblk, row(d_skip[0]),
            d_w_glu_a[0].astype(BF16), d_w_glu_b[0].astype(BF16))
    h = _mlp(h, norm_g[3], w_up_b, w_down_b, 3)
    return h.reshape(bsz, s, d)
```

```python
import functools
import math

import jax
import jax.numpy as jnp
from jax import lax
from jax.experimental import pallas as pl
from jax.experimental.pallas import tpu as pltpu

F32 = jnp.float32
BF16 = jnp.bfloat16

D_MODEL = 1024
D_FF = 4 * D_MODEL
RMS_EPS = 1e-6
LANES = 128
SUBLANES = 8
VMEM_LIMIT = 56 * 1024 * 1024

CHUNK = 128
GMLP_HEADS = 8
POOL_WINDOWS = (2, 4, 8, 16)
POOL_GROUP = D_MODEL // len(POOL_WINDOWS)
POOL_HALO = 32
assert POOL_WINDOWS == tuple(2 << k for k in range(len(POOL_WINDOWS)))
assert POOL_HALO == SUBLANES * len(POOL_WINDOWS)
MLA_HEADS = 8
Q_LORA = 384
KV_LORA = 256
QK_NOPE = 128
QK_ROPE = 64
V_HEAD = 128
V_ROWS = V_HEAD + 16
ROPE_THETA = 10000.0
QK_PAD = 256
S5_GROUP_CH = 16
S5_GROUPS = D_MODEL // S5_GROUP_CH
S5_STATE = 64
S5_BLOCK_GROUPS = LANES // S5_GROUP_CH
S5_BLOCKS = S5_GROUPS // S5_BLOCK_GROUPS
S5_BLOCK_STATE = S5_BLOCK_GROUPS * S5_STATE


def _rms(x, g):
    return x * lax.rsqrt(jnp.mean(x * x, axis=-1, keepdims=True) + RMS_EPS) * g


def _gelu(x):
    c = math.sqrt(2.0 / math.pi)
    return 0.5 * x * (1.0 + jnp.tanh(c * (x + 0.044715 * (x * x * x))))


def _dot(a, b):
    return jnp.dot(a, b, preferred_element_type=F32)


def _const_spec(shape):
    nd = len(shape)
    return pl.BlockSpec(shape, lambda *_: (0,) * nd, pipeline_mode=pl.Buffered(1))


def _row_spec(tm, width):
    return pl.BlockSpec((tm, width), lambda i: (i, 0))


def _params(*sem, flags=None):
    return pltpu.CompilerParams(dimension_semantics=sem, vmem_limit_bytes=VMEM_LIMIT, flags=flags)


MLP_TF = 1024


def _mlp_tail(h, g_ref, wu_ref, wd_ref):
    z = _rms(h, g_ref[2:3, :]).astype(BF16)
    acc = jnp.zeros(h.shape, F32)
    for c in range(D_FF // MLP_TF):
        a = _dot(z, wu_ref[:, c * MLP_TF:(c + 1) * MLP_TF])
        a = jnp.square(jnp.maximum(a, 0.0)).astype(BF16)
        acc = acc + _dot(a, wd_ref[c * MLP_TF:(c + 1) * MLP_TF, :])
    return h + _rms(acc, g_ref[3:4, :])


def _mlp_weight_specs(w_up, w_down, layer):
    return [pl.BlockSpec((None,) + w.shape[1:], lambda *_: (layer, 0, 0), pipeline_mode=pl.Buffered(1))
            for w in (w_up, w_down)]


def _mlp_body(h_ref, g_ref, wu_ref, wd_ref, o_ref):
    o_ref[...] = _mlp_tail(h_ref[...], g_ref, wu_ref, wd_ref)


def _mlp(h, g, w_up, w_down, layer, *, tm=512):
    s = h.shape[0]
    return pl.pallas_call(
        _mlp_body,
        grid=(s // tm,),
        in_specs=[_row_spec(tm, D_MODEL), _const_spec(g.shape)] + _mlp_weight_specs(w_up, w_down, layer),
        out_specs=_row_spec(tm, D_MODEL),
        out_shape=jax.ShapeDtypeStruct(h.shape, F32),
        compiler_params=_params("parallel"),
        name="mlp",
    )(h, g, w_up, w_down)


def _gmlp_body(h_ref, g_ref, win_ref, bin_ref, gv_ref, bv_ref, ws_ref, bs_ref, wout_ref, o_ref,
               sv_ref, *, tm):
    h = h_ref[...]
    z = _rms(h, g_ref[0:1, :]).astype(BF16)
    uv = _gelu(_dot(z, win_ref[...]) + bin_ref[...])
    u = uv[:, :D_MODEL]
    v = uv[:, D_MODEL:]
    mu = jnp.mean(v, axis=-1, keepdims=True)
    vc = v - mu
    v = vc * lax.rsqrt(jnp.mean(vc * vc, axis=-1, keepdims=True) + RMS_EPS)
    v = (v * gv_ref[...] + bv_ref[...]).astype(BF16)
    row = lax.broadcasted_iota(jnp.int32, (CHUNK, CHUNK), 0)
    col = lax.broadcasted_iota(jnp.int32, (CHUNK, CHUNK), 1)
    nchunk = tm // CHUNK
    for hd in range(GMLP_HEADS):
        ws = jnp.where(col <= row, ws_ref[hd], 0.0).astype(BF16)
        cols = slice(hd * LANES, (hd + 1) * LANES)
        rhs = jnp.concatenate([v[c * CHUNK:(c + 1) * CHUNK, cols] for c in range(nchunk)], axis=1)
        sv = _dot(ws, rhs)
        for c in range(nchunk):
            sv_ref[c * CHUNK:(c + 1) * CHUNK, cols] = (
                sv[:, c * LANES:(c + 1) * LANES] + bs_ref[:, cols])
    y = (u * sv_ref[...]).astype(BF16)
    o_ref[...] = h + _rms(_dot(y, wout_ref[...]), g_ref[1:2, :])


def _gmlp(h, g, w_in, b_in, g_v, b_v, w_s, b_s_full, w_out, *, tm=512):
    s = h.shape[0]
    return pl.pallas_call(
        functools.partial(_gmlp_body, tm=tm),
        grid=(s // tm,),
        in_specs=[_row_spec(tm, D_MODEL), _const_spec(g.shape), _const_spec(w_in.shape),
                  _const_spec(b_in.shape), _const_spec(g_v.shape), _const_spec(b_v.shape),
                  _const_spec(w_s.shape), _const_spec(b_s_full.shape), _const_spec(w_out.shape)],
        out_specs=_row_spec(tm, D_MODEL),
        out_shape=jax.ShapeDtypeStruct(h.shape, F32),
        scratch_shapes=[pltpu.VMEM((tm, D_MODEL), F32)],
        compiler_params=_params("parallel"),
        name="gmlp",
    )(h, g, w_in, b_in, g_v, b_v, w_s, b_s_full, w_out)


def _pool_body(h_ref, halo_ref, g_ref, w_ref, scale_ref, wu_ref, wd_ref, o_ref, zz_ref, ws_ref,
               *, tm):
    i = pl.program_id(0)
    h = h_ref[...]
    z = _rms(h, g_ref[0:1, :])
    zh = _rms(halo_ref[...], g_ref[0:1, :])
    zz_ref[0:POOL_HALO, :] = jnp.where(i > 0, zh, 0.0)
    zz_ref[POOL_HALO:, :] = z
    nrows = tm + POOL_HALO
    for k in range(1, len(POOL_WINDOWS) + 1):
        d = 1 << (k - 1)
        lo = SUBLANES * k
        cols = slice((k - 1) * POOL_GROUP, D_MODEL)
        src = zz_ref if k == 1 else ws_ref
        ws_ref[lo:, cols] = src[lo:, cols] + src[lo - d:nrows - d, cols]
    pos = i * tm + lax.broadcasted_iota(jnp.int32, (tm, 1), 0)
    outs = []
    for gi, w in enumerate(POOL_WINDOWS):
        cols = slice(gi * POOL_GROUP, (gi + 1) * POOL_GROUP)
        xg = z[:, cols]
        win = ws_ref[POOL_HALO:, cols]
        count = jnp.minimum(pos + 1, w).astype(F32)
        diff = (win / count - xg).astype(BF16)
        outs.append(_dot(diff, w_ref[gi]))
    out = jnp.concatenate(outs, axis=-1) * scale_ref[...]
    o_ref[...] = _mlp_tail(h + _rms(out, g_ref[1:2, :]), g_ref, wu_ref, wd_ref)


def _pool_mlp(h, g, w_grp, scale, w_up, w_down, layer, *, tm=512):
    s = h.shape[0]
    ratio = tm // POOL_HALO
    return pl.pallas_call(
        functools.partial(_pool_body, tm=tm),
        grid=(s // tm,),
        in_specs=[_row_spec(tm, D_MODEL),
                  pl.BlockSpec((POOL_HALO, D_MODEL), lambda i: (jnp.maximum(i * ratio - 1, 0), 0)),
                  _const_spec(g.shape), _const_spec(w_grp.shape), _const_spec(scale.shape)]
        + _mlp_weight_specs(w_up, w_down, layer),
        out_specs=_row_spec(tm, D_MODEL),
        out_shape=jax.ShapeDtypeStruct(h.shape, F32),
        scratch_shapes=[pltpu.VMEM((tm + POOL_HALO, D_MODEL), F32)] * 2,
        compiler_params=_params("parallel"),
        name="pool_mlp",
    )(h, h, g, w_grp, scale, w_up, w_down)


def _rope_tile(x, cos, sin_lo, sin_hi):
    n = x.shape[1] // LANES
    outs = []
    for t in range(n):
        xt = x[:, t * LANES:(t + 1) * LANES]
        up = pltpu.roll(xt, 32, axis=1)
        down = pltpu.roll(xt, LANES - 32, axis=1)
        outs.append(xt * cos + down * sin_lo + up * sin_hi)
    return outs


def _mla_proj_body(h_ref, pos_ref, g_ref, wdq_ref, gq_ref, wuqn_ref, wuqr_ref, wdkv_ref, gkv_ref,
                   wuk_ref, wuv_ref, invf_ref, q_ref, k_ref, v_ref, *, scale):
    h = h_ref[...]
    z = _rms(h, g_ref[0:1, :]).astype(BF16)
    ang = pos_ref[...].astype(F32) * invf_ref[...]
    lane = lax.broadcasted_iota(jnp.int32, ang.shape, 1)
    cos = jnp.where(lane < QK_ROPE, jnp.cos(ang), 0.0)
    sin = jnp.sin(ang)
    sin_lo = jnp.where(lane < QK_ROPE // 2, -sin, 0.0)
    sin_hi = jnp.where((lane >= QK_ROPE // 2) & (lane < QK_ROPE), sin, 0.0)

    ql = _rms(_dot(z, wdq_ref[...]), gq_ref[...]).astype(BF16)
    qn = _dot(ql, wuqn_ref[...]) * scale
    qr = _rope_tile(_dot(ql, wuqr_ref[...]) * scale, cos, sin_lo, sin_hi)
    ckv = _dot(z, wdkv_ref[...])
    c = _rms(ckv[:, :KV_LORA], gkv_ref[...]).astype(BF16)
    kr = _rope_tile(ckv[:, KV_LORA:], cos, sin_lo, sin_hi)[0].astype(BF16)
    kn = _dot(c, wuk_ref[...])
    for hd in range(MLA_HEADS):
        cols = slice(hd * LANES, (hd + 1) * LANES)
        q_ref[:, hd * QK_PAD:hd * QK_PAD + LANES] = qn[:, cols].astype(BF16)
        q_ref[:, hd * QK_PAD + LANES:(hd + 1) * QK_PAD] = qr[hd].astype(BF16)
        k_ref[:, hd * QK_PAD:hd * QK_PAD + LANES] = kn[:, cols].astype(BF16)
        k_ref[:, hd * QK_PAD + LANES:(hd + 1) * QK_PAD] = kr
    vt = lax.dot_general(wuv_ref[...], c, (((1,), (1,)), ((), ())),
                         preferred_element_type=F32).astype(BF16)
    ones = jnp.ones((V_ROWS - V_HEAD, vt.shape[1]), BF16)
    for hd in range(MLA_HEADS):
        v_ref[hd * V_ROWS:hd * V_ROWS + V_HEAD, :] = vt[hd * V_HEAD:(hd + 1) * V_HEAD, :]
        v_ref[hd * V_ROWS + V_HEAD:(hd + 1) * V_ROWS, :] = ones


def _mla_proj(h, pos, g, w_dq, g_q, w_uq_n, w_uq_r, w_dkv, g_kv, w_uk, w_uv, invf, *, tm=512):
    s = h.shape[0]
    scale = (QK_NOPE + QK_ROPE) ** -0.5 * math.log2(math.e)
    consts = (g, w_dq, g_q, w_uq_n, w_uq_r, w_dkv, g_kv, w_uk, w_uv, invf)
    return pl.pallas_call(
        functools.partial(_mla_proj_body, scale=scale),
        grid=(s // tm,),
        in_specs=[_row_spec(tm, D_MODEL), _row_spec(tm, 1)] + [_const_spec(a.shape) for a in consts],
        out_specs=[_row_spec(tm, MLA_HEADS * QK_PAD), _row_spec(tm, MLA_HEADS * QK_PAD),
                   pl.BlockSpec((MLA_HEADS * V_ROWS, tm), lambda i: (0, i))],
        out_shape=[jax.ShapeDtypeStruct((s, MLA_HEADS * QK_PAD), BF16),
                   jax.ShapeDtypeStruct((s, MLA_HEADS * QK_PAD), BF16),
                   jax.ShapeDtypeStruct((MLA_HEADS * V_ROWS, s), BF16)],
        compiler_params=_params("parallel"),
        name="mla_proj",
    )(h, pos, *consts)


ATTN_SLOTS = 4
ATTN_UNROLL = 8


def _attn_body(qa_ref, qb_ref, k_ref, vt_ref, oa_ref, ob_ref, qt_ref, m_ref, acc_ref, *slot_refs,
               t, nt):
    qa = pl.program_id(1)
    qb = nt - 1 - qa
    ns = ATTN_SLOTS
    s_refs, p_refs, a_refs, mt_refs = (slot_refs[i * ns:(i + 1) * ns] for i in range(4))
    neg = -1e30
    rows = 2 * SUBLANES
    last = nt

    qt_ref[0] = qa_ref[...].astype(F32).T.astype(BF16)
    qt_ref[1] = qb_ref[...].astype(F32).T.astype(BF16)
    m_ref[...] = jnp.full(m_ref.shape, neg, F32)
    acc_ref[...] = jnp.zeros(acc_ref.shape, F32)

    def owner(j):
        return jnp.where(j < 2, j, (j >= qa + 2).astype(jnp.int32))

    def key_tile(j):
        full = jnp.where(j >= qa + 2, j - 2 - qa, j - 2)
        return jnp.where(j == 0, qa, jnp.where(j == 1, qb, full))

    def qk(j, slot, masked=False):
        start = pl.multiple_of(key_tile(j) * t, t)
        s = _dot(k_ref[pl.ds(start, t), :], qt_ref[owner(j)])
        if masked:
            kpos = lax.broadcasted_iota(jnp.int32, (t, t), 0)
            qpos = lax.broadcasted_iota(jnp.int32, (t, t), 1)
            s = jnp.where(kpos <= qpos, s, neg)
        s_refs[slot][...] = s
        nacc = 2
        mx = [s[r * SUBLANES:(r + 1) * SUBLANES, :] for r in range(nacc)]
        for r in range(nacc, t // SUBLANES):
            mx[r % nacc] = jnp.maximum(mx[r % nacc], s[r * SUBLANES:(r + 1) * SUBLANES, :])
        mt_refs[slot][...] = jnp.maximum(mx[0], mx[1])

    def softmax(j, slot):
        s_ref = s_refs[slot]
        p_ref = p_refs[slot]
        own = owner(j)
        mt = mt_refs[slot][...]
        m_old = m_ref[own]
        m_new = jnp.maximum(m_old, jnp.max(mt, axis=0, keepdims=True))
        m_ref[own] = m_new
        a_refs[slot][...] = jnp.exp2(m_old - m_new)
        mb = jnp.broadcast_to(m_new, (SUBLANES, t))
        for r in range(t // rows):
            lo = jnp.exp2(s_ref[r * rows:r * rows + SUBLANES, :] - mb)
            hi = jnp.exp2(s_ref[r * rows + SUBLANES:(r + 1) * rows, :] - mb)
            p_ref[r * rows:(r + 1) * rows, :] = jnp.concatenate([lo, hi], axis=0).astype(BF16)

    def pv(j, slot):
        start = pl.multiple_of(key_tile(j) * t, t)
        own = owner(j)
        acc_ref[own] = acc_ref[own] * a_refs[slot][...] + _dot(vt_ref[:, pl.ds(start, t)],
                                                                p_refs[slot][...])

    qk(0, 0, masked=True)
    qk(1, 1, masked=True)
    softmax(0, 0)

    def body(jq, _):
        for u in range(ATTN_UNROLL):
            j = 1 + ATTN_UNROLL * jq + u
            slot = (1 + u) % ns
            softmax(j, slot)
            qk(jnp.minimum(j + 1, last), (slot + 1) % ns)
            pv(j - 1, (slot - 1) % ns)
        return 0

    lax.fori_loop(0, nt // ATTN_UNROLL, body, 0)
    pv(last, last % ns)

    for own, o_ref in ((0, oa_ref), (1, ob_ref)):
        acc = acc_ref[own]
        o_t = acc[:V_HEAD, :] * (1.0 / acc[V_HEAD:V_HEAD + 1, :])
        o_ref[...] = o_t.T.astype(o_ref.dtype)


def _attention(q, k, vt, *, t=512):
    s = q.shape[0]
    nt = s // t
    assert nt % 2 == 0 and nt % ATTN_UNROLL == 0
    half = nt // 2
    out = jax.ShapeDtypeStruct((s // 2, MLA_HEADS * V_HEAD), BF16)
    return pl.pallas_call(
        functools.partial(_attn_body, t=t, nt=nt),
        grid=(MLA_HEADS, half),
        in_specs=[pl.BlockSpec((t, QK_PAD), lambda hd, i: (i, hd)),
                  pl.BlockSpec((t, QK_PAD), lambda hd, i: (nt - 1 - i, hd)),
                  pl.BlockSpec((s, QK_PAD), lambda hd, i: (0, hd)),
                  pl.BlockSpec((V_ROWS, s), lambda hd, i: (hd, 0))],
        out_specs=[pl.BlockSpec((t, V_HEAD), lambda hd, i: (i, hd)),
                   pl.BlockSpec((t, V_HEAD), lambda hd, i: (half - 1 - i, hd))],
        out_shape=[out, out],
        scratch_shapes=[
            pltpu.VMEM((2, QK_PAD, t), BF16),
            pltpu.VMEM((2, 1, t), F32),
            pltpu.VMEM((2, V_ROWS, t), F32),
        ] + [pltpu.VMEM((t, t), F32)] * ATTN_SLOTS
          + [pltpu.VMEM((t, t), BF16)] * ATTN_SLOTS
          + [pltpu.VMEM((1, t), F32)] * ATTN_SLOTS
          + [pltpu.VMEM((SUBLANES, t), F32)] * ATTN_SLOTS,
        compiler_params=_params("parallel", "parallel"),
        name="mla_attn",
    )(q, q, k, vt)


def _out_proj_body(h_ref, oa_ref, ob_ref, g_ref, wo_ref, wu_ref, wd_ref, o_ref, *, half):
    o = jnp.where(pl.program_id(0) < half, oa_ref[...], ob_ref[...])
    h = h_ref[...] + _rms(_dot(o, wo_ref[...]), g_ref[1:2, :])
    o_ref[...] = _mlp_tail(h, g_ref, wu_ref, wd_ref)


def _out_proj_mlp(h, oa, ob, g, w_o, w_up, w_down, layer, *, tm=512):
    s = h.shape[0]
    half = oa.shape[0] // tm
    return pl.pallas_call(
        functools.partial(_out_proj_body, half=half),
        grid=(s // tm,),
        in_specs=[_row_spec(tm, D_MODEL),
                  pl.BlockSpec((tm, D_MODEL), lambda i: (jnp.minimum(i, half - 1), 0)),
                  pl.BlockSpec((tm, D_MODEL), lambda i: (jnp.maximum(i - half, 0), 0)),
                  _const_spec(g.shape), _const_spec(w_o.shape)]
        + _mlp_weight_specs(w_up, w_down, layer),
        out_specs=_row_spec(tm, D_MODEL),
        out_shape=jax.ShapeDtypeStruct(h.shape, F32),
        compiler_params=_params("parallel"),
        name="mla_out_mlp",
    )(h, oa, ob, g, w_o, w_up, w_down)


def _s5_disc_body(lr_ref, li_ref, ldt_ref, br_ref, bi_ref, abr_ref, abi_ref, bbr_ref, bbi_ref):
    dt = jnp.exp(ldt_ref[...])
    lr = lr_ref[...]
    li = li_ref[...]
    mag = jnp.exp(lr * dt)
    ab_re = mag * jnp.cos(li * dt)
    ab_im = mag * jnp.sin(li * dt)
    den = lr * lr + li * li
    f_re = ((ab_re - 1.0) * lr + ab_im * li) / den
    f_im = (ab_im * lr - (ab_re - 1.0) * li) / den
    abr_ref[...] = ab_re
    abi_ref[...] = ab_im
    br = br_ref[...]
    bi = bi_ref[...]
    bbr_ref[...] = f_re[:, None, :] * br - f_im[:, None, :] * bi
    bbi_ref[...] = f_re[:, None, :] * bi + f_im[:, None, :] * br


def _s5_discretise(lam_re, lam_im, log_dt, b_re_t, b_im_t):
    g, p = lam_re.shape
    c = b_re_t.shape[1]
    return pl.pallas_call(
        _s5_disc_body,
        out_shape=[jax.ShapeDtypeStruct((g, p), F32), jax.ShapeDtypeStruct((g, p), F32),
                   jax.ShapeDtypeStruct((g, c, p), F32), jax.ShapeDtypeStruct((g, c, p), F32)],
        name="s5_disc",
    )(lam_re, lam_im, log_dt.reshape(g, 1), b_re_t, b_im_t)


def _s5_body(h_ref, g_ref, perm_ref, permt_ref, ar_ref, ai_ref, bblk_ref, cblk_ref, dskip_ref,
             wa_ref, wb_ref, o_ref, xr0_ref, xi0_ref, xr1_ref, xi1_ref, xc0_ref, xc1_ref,
             pr_ref, pi_ref, cr_ref, ci_ref, *, tm):
    seg = tm // SUBLANES
    i0 = pl.program_id(0)
    nst = S5_BLOCK_STATE
    pack = 2 * SUBLANES

    @pl.when(i0 == 0)
    def _():
        cr_ref[...] = jnp.zeros(cr_ref.shape, F32)
        ci_ref[...] = jnp.zeros(ci_ref.shape, F32)
        ar = ar_ref[...]
        ai = ai_ref[...]
        bcast = lambda v: jnp.broadcast_to(v, (SUBLANES, v.shape[1]))
        pr_ref[0:SUBLANES, :] = bcast(ar)
        pi_ref[0:SUBLANES, :] = bcast(ai)

        def pw(j, carry):
            r, im = carry
            r, im = r * ar - im * ai, r * ai + im * ar
            rows = pl.ds(pl.multiple_of(j * SUBLANES, SUBLANES), SUBLANES)
            pr_ref[rows, :] = bcast(r)
            pi_ref[rows, :] = bcast(im)
            return r, im

        lax.fori_loop(1, seg, pw, (ar, ai))

    h = h_ref[...]
    z = _rms(h, g_ref[0:1, :])
    zp = _dot(perm_ref[...], z.astype(BF16)).astype(BF16)

    yps = []
    for cb in range(S5_BLOCKS):
        st = slice(cb * nst, (cb + 1) * nst)
        xr_ref, xi_ref, xc_ref = ((xr0_ref, xi0_ref, xc0_ref), (xr1_ref, xi1_ref, xc1_ref))[cb % 2]
        bu = _dot(zp[:, cb * LANES:(cb + 1) * LANES], bblk_ref[cb])
        ar = jnp.broadcast_to(ar_ref[:, st], (SUBLANES, nst))
        ai = jnp.broadcast_to(ai_ref[:, st], (SUBLANES, nst))
        xr = jnp.zeros((SUBLANES, nst), F32)
        xi = xr
        for i in range(seg):
            rows = slice(i * SUBLANES, (i + 1) * SUBLANES)
            xr, xi = (ar * xr - ai * xi + bu[rows, :nst], ar * xi + ai * xr + bu[rows, nst:])
            xr_ref[rows, :] = xr
            xi_ref[rows, :] = xi
        alr = pr_ref[tm - 1:tm, st]
        ali = pi_ref[tm - 1:tm, st]
        c_r = cr_ref[:, st]
        c_i = ci_ref[:, st]
        rows_r, rows_i = [], []
        for sgm in range(SUBLANES):
            rows_r.append(c_r)
            rows_i.append(c_i)
            e_r = xr[sgm:sgm + 1, :]
            e_i = xi[sgm:sgm + 1, :]
            c_r, c_i = alr * c_r - ali * c_i + e_r, alr * c_i + ali * c_r + e_i
        cr_ref[:, st] = c_r
        ci_ref[:, st] = c_i
        cin_r = jnp.concatenate(rows_r, axis=0)
        cin_i = jnp.concatenate(rows_i, axis=0)
        for i2 in range(seg // 2):
            fr, fi = [], []
            for i in (2 * i2, 2 * i2 + 1):
                rows = slice(i * SUBLANES, (i + 1) * SUBLANES)
                p_r = pr_ref[rows, st]
                p_i = pi_ref[rows, st]
                fr.append(xr_ref[rows, :] + (p_r * cin_r - p_i * cin_i))
                fi.append(xi_ref[rows, :] + (p_r * cin_i + p_i * cin_r))
            rows2 = slice(i2 * pack, (i2 + 1) * pack)
            xc_ref[rows2, 0:nst] = jnp.concatenate(fr, axis=0).astype(BF16)
            xc_ref[rows2, nst:2 * nst] = jnp.concatenate(fi, axis=0).astype(BF16)
        yps.append(_dot(xc_ref[...], cblk_ref[cb]))

    yp = jnp.concatenate(yps, axis=1)
    yp_hi = yp.astype(BF16)
    yp_lo = (yp - yp_hi.astype(F32)).astype(BF16)
    y = _dot(permt_ref[...], yp_hi) + _dot(permt_ref[...], yp_lo)
    y = _gelu(y + dskip_ref[...] * z).astype(BF16)
    out = _dot(y, wa_ref[...]) * jax.nn.sigmoid(_dot(y, wb_ref[...]))
    o_ref[...] = h + _rms(out, g_ref[1:2, :])


def _s5_permutation(tm):
    seg = tm // SUBLANES
    r = jnp.arange(tm)
    src = (r % SUBLANES) * seg + r // SUBLANES
    return (src[:, None] == jnp.arange(tm)[None, :]).astype(BF16)


def _s5(h, g, ab_re, ab_im, bblk, cblk, dskip, w_a, w_b, *, tm=512):
    s = h.shape[0]
    seg = tm // SUBLANES
    nstate = S5_GROUPS * S5_STATE
    perm = _s5_permutation(tm)
    consts = (g, perm, perm.T, ab_re, ab_im, bblk, cblk, dskip, w_a, w_b)
    state = pltpu.VMEM((tm, S5_BLOCK_STATE), F32)
    return pl.pallas_call(
        functools.partial(_s5_body, tm=tm),
        grid=(s // tm,),
        in_specs=[_row_spec(tm, D_MODEL)] + [_const_spec(a.shape) for a in consts],
        out_specs=_row_spec(tm, D_MODEL),
        out_shape=jax.ShapeDtypeStruct(h.shape, F32),
        scratch_shapes=[
            state, state, state, state,
            pltpu.VMEM((tm, 2 * S5_BLOCK_STATE), BF16),
            pltpu.VMEM((tm, 2 * S5_BLOCK_STATE), BF16),
            pltpu.VMEM((tm, nstate), F32),
            pltpu.VMEM((tm, nstate), F32),
            pltpu.VMEM((1, nstate), F32),
            pltpu.VMEM((1, nstate), F32),
        ],
        compiler_params=_params("arbitrary"),
        name="s5",
    )(h, *consts)


def _s5_block_matrices(bb_re, bb_im, c_re, c_im):
    nb, gb, c, p = S5_BLOCKS, S5_BLOCK_GROUPS, S5_GROUP_CH, S5_STATE
    eye = jnp.eye(gb, dtype=F32)

    def in_map(bb):
        t = bb.reshape(nb, gb, c, p)
        return jnp.einsum('bgcp,gk->bgckp', t, eye).reshape(nb, gb * c, gb * p)

    def out_map(cc):
        t = cc.reshape(nb, gb, c, p)
        return jnp.einsum('bgcp,gk->bgpkc', t, eye).reshape(nb, gb * p, gb * c)

    bblk = jnp.concatenate([in_map(bb_re), in_map(bb_im)], axis=2).astype(BF16)
    cblk = jnp.concatenate([out_map(c_re), out_map(-c_im)], axis=1).astype(BF16)
    return bblk, cblk


def _mla_weights(w_uq, w_dkv):
    wq = w_uq.reshape(Q_LORA, MLA_HEADS, QK_NOPE + QK_ROPE)
    w_n = wq[:, :, :QK_NOPE].reshape(Q_LORA, MLA_HEADS * QK_NOPE)
    w_r = jnp.pad(wq[:, :, QK_NOPE:], ((0, 0), (0, 0), (0, LANES - QK_ROPE)))
    w_r = w_r.reshape(Q_LORA, MLA_HEADS * LANES)
    w_dkv_p = jnp.pad(w_dkv, ((0, 0), (0, LANES - QK_ROPE)))
    return w_n.astype(BF16), w_r.astype(BF16), w_dkv_p.astype(BF16)


def kernel(x, positions, norm_g, w_up, w_down, a_w_in, a_b_in, a_g_v, a_b_v, a_w_s, a_b_s, a_w_out, b_w_grp, b_scale, c_w_dq, c_g_q, c_w_uq, c_w_dkv, c_g_kv, c_w_uk, c_w_uv, c_w_o, d_lam_re, d_lam_im, d_log_dt, d_b_re, d_b_im, d_c_re, d_c_im, d_skip, d_w_glu_a, d_w_glu_b):
    bsz, s, d = x.shape
    assert bsz == 1 and d == D_MODEL
    h = x.reshape(s, d)
    row = lambda a: a.reshape(1, -1)
    w_up_b = w_up.astype(BF16)
    w_down_b = w_down.astype(BF16)

    b_s_full = jnp.repeat(a_b_s[0].T, D_MODEL // GMLP_HEADS, axis=1)
    h = _gmlp(h, norm_g[0], a_w_in[0].astype(BF16), row(a_b_in[0]), row(a_g_v[0]), row(a_b_v[0]),
              a_w_s[0], b_s_full, a_w_out[0].astype(BF16))
    h = _mlp(h, norm_g[0], w_up_b, w_down_b, 0)

    h = _pool_mlp(h, norm_g[1], b_w_grp[0].astype(BF16), row(b_scale[0]), w_up_b, w_down_b, 1)

    w_uq_n, w_uq_r, w_dkv_p = _mla_weights(c_w_uq[0], c_w_dkv[0])
    inv_freq = ROPE_THETA ** (-jnp.arange(0, QK_ROPE, 2, dtype=F32) / QK_ROPE)
    invf = jnp.concatenate([inv_freq, inv_freq, jnp.zeros((LANES - QK_ROPE,), F32)]).reshape(1, LANES)
    q, k, v = _mla_proj(h, positions.reshape(s, 1), norm_g[2], c_w_dq[0].astype(BF16), row(c_g_q[0]),
                        w_uq_n, w_uq_r, w_dkv_p, row(c_g_kv[0]), c_w_uk[0].astype(BF16),
                        c_w_uv[0].T.astype(BF16), invf)
    oa, ob = _attention(q, k, v)
    h = _out_proj_mlp(h, oa, ob, norm_g[2], c_w_o[0].astype(BF16), w_up_b, w_down_b, 2)

    tr = lambda a: jnp.swapaxes(a, 1, 2)
    ab_re, ab_im, bb_re, bb_im = _s5_discretise(d_lam_re[0], d_lam_im[0], d_log_dt[0],
                                                tr(d_b_re[0]), tr(d_b_im[0]))
    bblk, cblk = _s5_block_matrices(bb_re, bb_im, d_c_re[0], d_c_im[0])
    h = _s5(h, norm_g[3], row(ab_re), row(ab_im), bblk, cblk, row(d_skip[0]),
            d_w_glu_a[0].astype(BF16), d_w_glu_b[0].astype(BF16))
    h = _mlp(h, norm_g[3], w_up_b, w_down_b, 3)
    return h.reshape(bsz, s, d)
```

```python
import functools
import math

import jax
import jax.numpy as jnp
from jax import lax
from jax.experimental import pallas as pl
from jax.experimental.pallas import tpu as pltpu

F32 = jnp.float32
BF16 = jnp.bfloat16

D_MODEL = 1024
D_FF = 4 * D_MODEL
RMS_EPS = 1e-6
LANES = 128
SUBLANES = 8
VMEM_LIMIT = 56 * 1024 * 1024

CHUNK = 128
GMLP_HEADS = 8
POOL_WINDOWS = (2, 4, 8, 16)
POOL_GROUP = D_MODEL // len(POOL_WINDOWS)
POOL_HALO = 32
assert POOL_WINDOWS == tuple(2 << k for k in range(len(POOL_WINDOWS)))
assert POOL_HALO == SUBLANES * len(POOL_WINDOWS)
MLA_HEADS = 8
Q_LORA = 384
KV_LORA = 256
QK_NOPE = 128
QK_ROPE = 64
V_HEAD = 128
V_ROWS = V_HEAD + 16
ROPE_THETA = 10000.0
QK_PAD = 256
S5_GROUP_CH = 16
S5_GROUPS = D_MODEL // S5_GROUP_CH
S5_STATE = 64
S5_BLOCK_GROUPS = LANES // S5_GROUP_CH
S5_BLOCKS = S5_GROUPS // S5_BLOCK_GROUPS
S5_BLOCK_STATE = S5_BLOCK_GROUPS * S5_STATE


def _rms(x, g):
    return x * lax.rsqrt(jnp.mean(x * x, axis=-1, keepdims=True) + RMS_EPS) * g


def _gelu(x):
    c = math.sqrt(2.0 / math.pi)
    return 0.5 * x * (1.0 + jnp.tanh(c * (x + 0.044715 * (x * x * x))))


def _dot(a, b):
    return jnp.dot(a, b, preferred_element_type=F32)


def _const_spec(shape):
    nd = len(shape)
    return pl.BlockSpec(shape, lambda *_: (0,) * nd, pipeline_mode=pl.Buffered(1))


def _row_spec(tm, width):
    return pl.BlockSpec((tm, width), lambda i: (i, 0))


def _params(*sem, flags=None):
    return pltpu.CompilerParams(dimension_semantics=sem, vmem_limit_bytes=VMEM_LIMIT, flags=flags)


MLP_TF = 1024


def _mlp_tail(h, g_ref, wu_ref, wd_ref):
    z = _rms(h, g_ref[2:3, :]).astype(BF16)
    acc = jnp.zeros(h.shape, F32)
    for c in range(D_FF // MLP_TF):
        a = _dot(z, wu_ref[:, c * MLP_TF:(c + 1) * MLP_TF])
        a = jnp.square(jnp.maximum(a, 0.0)).astype(BF16)
        acc = acc + _dot(a, wd_ref[c * MLP_TF:(c + 1) * MLP_TF, :])
    return h + _rms(acc, g_ref[3:4, :])


def _mlp_weight_specs(w_up, w_down, layer):
    return [pl.BlockSpec((None,) + w.shape[1:], lambda *_: (layer, 0, 0), pipeline_mode=pl.Buffered(1))
            for w in (w_up, w_down)]


def _mlp_body(h_ref, g_ref, wu_ref, wd_ref, o_ref):
    o_ref[...] = _mlp_tail(h_ref[...], g_ref, wu_ref, wd_ref)


def _mlp(h, g, w_up, w_down, layer, *, tm=512):
    s = h.shape[0]
    return pl.pallas_call(
        _mlp_body,
        grid=(s // tm,),
        in_specs=[_row_spec(tm, D_MODEL), _const_spec(g.shape)] + _mlp_weight_specs(w_up, w_down, layer),
        out_specs=_row_spec(tm, D_MODEL),
        out_shape=jax.ShapeDtypeStruct(h.shape, F32),
        compiler_params=_params("parallel"),
        name="mlp",
    )(h, g, w_up, w_down)


def _gmlp_body(h_ref, g_ref, win_ref, bin_ref, gv_ref, bv_ref, ws_ref, bs_ref, wout_ref, o_ref,
               sv_ref, *, tm):
    h = h_ref[...]
    z = _rms(h, g_ref[0:1, :]).astype(BF16)
    uv = _gelu(_dot(z, win_ref[...]) + bin_ref[...])
    u = uv[:, :D_MODEL]
    v = uv[:, D_MODEL:]
    mu = jnp.mean(v, axis=-1, keepdims=True)
    vc = v - mu
    v = vc * lax.rsqrt(jnp.mean(vc * vc, axis=-1, keepdims=True) + RMS_EPS)
    v = (v * gv_ref[...] + bv_ref[...]).astype(BF16)
    row = lax.broadcasted_iota(jnp.int32, (CHUNK, CHUNK), 0)
    col = lax.broadcasted_iota(jnp.int32, (CHUNK, CHUNK), 1)
    nchunk = tm // CHUNK
    for hd in range(GMLP_HEADS):
        ws = jnp.where(col <= row, ws_ref[hd], 0.0).astype(BF16)
        cols = slice(hd * LANES, (hd + 1) * LANES)
        rhs = jnp.concatenate([v[c * CHUNK:(c + 1) * CHUNK, cols] for c in range(nchunk)], axis=1)
        sv = _dot(ws, rhs)
        for c in range(nchunk):
            sv_ref[c * CHUNK:(c + 1) * CHUNK, cols] = (
                sv[:, c * LANES:(c + 1) * LANES] + bs_ref[:, cols])
    y = (u * sv_ref[...]).astype(BF16)
    o_ref[...] = h + _rms(_dot(y, wout_ref[...]), g_ref[1:2, :])


def _gmlp(h, g, w_in, b_in, g_v, b_v, w_s, b_s_full, w_out, *, tm=512):
    s = h.shape[0]
    return pl.pallas_call(
        functools.partial(_gmlp_body, tm=tm),
        grid=(s // tm,),
        in_specs=[_row_spec(tm, D_MODEL), _const_spec(g.shape), _const_spec(w_in.shape),
                  _const_spec(b_in.shape), _const_spec(g_v.shape), _const_spec(b_v.shape),
                  _const_spec(w_s.shape), _const_spec(b_s_full.shape), _const_spec(w_out.shape)],
        out_specs=_row_spec(tm, D_MODEL),
        out_shape=jax.ShapeDtypeStruct(h.shape, F32),
        scratch_shapes=[pltpu.VMEM((tm, D_MODEL), F32)],
        compiler_params=_params("parallel"),
        name="gmlp",
    )(h, g, w_in, b_in, g_v, b_v, w_s, b_s_full, w_out)


def _pool_body(h_ref, halo_ref, g_ref, w_ref, scale_ref, wu_ref, wd_ref, o_ref, zz_ref, ws_ref,
               *, tm):
    i = pl.program_id(0)
    h = h_ref[...]
    z = _rms(h, g_ref[0:1, :])
    zh = _rms(halo_ref[...], g_ref[0:1, :])
    zz_ref[0:POOL_HALO, :] = jnp.where(i > 0, zh, 0.0)
    zz_ref[POOL_HALO:, :] = z
    nrows = tm + POOL_HALO
    for k in range(1, len(POOL_WINDOWS) + 1):
        d = 1 << (k - 1)
        lo = SUBLANES * k
        cols = slice((k - 1) * POOL_GROUP, D_MODEL)
        src = zz_ref if k == 1 else ws_ref
        ws_ref[lo:, cols] = src[lo:, cols] + src[lo - d:nrows - d, cols]
    pos = i * tm + lax.broadcasted_iota(jnp.int32, (tm, 1), 0)
    outs = []
    for gi, w in enumerate(POOL_WINDOWS):
        cols = slice(gi * POOL_GROUP, (gi + 1) * POOL_GROUP)
        xg = z[:, cols]
        win = ws_ref[POOL_HALO:, cols]
        count = jnp.minimum(pos + 1, w).astype(F32)
        diff = (win / count - xg).astype(BF16)
        outs.append(_dot(diff, w_ref[gi]))
    out = jnp.concatenate(outs, axis=-1) * scale_ref[...]
    o_ref[...] = _mlp_tail(h + _rms(out, g_ref[1:2, :]), g_ref, wu_ref, wd_ref)


def _pool_mlp(h, g, w_grp, scale, w_up, w_down, layer, *, tm=512):
    s = h.shape[0]
    ratio = tm // POOL_HALO
    return pl.pallas_call(
        functools.partial(_pool_body, tm=tm),
        grid=(s // tm,),
        in_specs=[_row_spec(tm, D_MODEL),
                  pl.BlockSpec((POOL_HALO, D_MODEL), lambda i: (jnp.maximum(i * ratio - 1, 0), 0)),
                  _const_spec(g.shape), _const_spec(w_grp.shape), _const_spec(scale.shape)]
        + _mlp_weight_specs(w_up, w_down, layer),
        out_specs=_row_spec(tm, D_MODEL),
        out_shape=jax.ShapeDtypeStruct(h.shape, F32),
        scratch_shapes=[pltpu.VMEM((tm + POOL_HALO, D_MODEL), F32)] * 2,
        compiler_params=_params("parallel"),
        name="pool_mlp",
    )(h, h, g, w_grp, scale, w_up, w_down)


def _rope_tile(x, cos, sin_lo, sin_hi):
    n = x.shape[1] // LANES
    outs = []
    for t in range(n):
        xt = x[:, t * LANES:(t + 1) * LANES]
        up = pltpu.roll(xt, 32, axis=1)
        down = pltpu.roll(xt, LANES - 32, axis=1)
        outs.append(xt * cos + down * sin_lo + up * sin_hi)
    return outs


def _mla_proj_body(h_ref, pos_ref, g_ref, wdq_ref, gq_ref, wuqn_ref, wuqr_ref, wdkv_ref, gkv_ref,
                   wuk_ref, wuv_ref, invf_ref, q_ref, k_ref, v_ref, *, scale):
    h = h_ref[...]
    z = _rms(h, g_ref[0:1, :]).astype(BF16)
    ang = pos_ref[...].astype(F32) * invf_ref[...]
    lane = lax.broadcasted_iota(jnp.int32, ang.shape, 1)
    cos = jnp.where(lane < QK_ROPE, jnp.cos(ang), 0.0)
    sin = jnp.sin(ang)
    sin_lo = jnp.where(lane < QK_ROPE // 2, -sin, 0.0)
    sin_hi = jnp.where((lane >= QK_ROPE // 2) & (lane < QK_ROPE), sin, 0.0)

    ql = _rms(_dot(z, wdq_ref[...]), gq_ref[...]).astype(BF16)
    qn = _dot(ql, wuqn_ref[...]) * scale
    qr = _rope_tile(_dot(ql, wuqr_ref[...]) * scale, cos, sin_lo, sin_hi)
    ckv = _dot(z, wdkv_ref[...])
    c = _rms(ckv[:, :KV_LORA], gkv_ref[...]).astype(BF16)
    kr = _rope_tile(ckv[:, KV_LORA:], cos, sin_lo, sin_hi)[0].astype(BF16)
    kn = _dot(c, wuk_ref[...])
    for hd in range(MLA_HEADS):
        cols = slice(hd * LANES, (hd + 1) * LANES)
        q_ref[:, hd * QK_PAD:hd * QK_PAD + LANES] = qn[:, cols].astype(BF16)
        q_ref[:, hd * QK_PAD + LANES:(hd + 1) * QK_PAD] = qr[hd].astype(BF16)
        k_ref[:, hd * QK_PAD:hd * QK_PAD + LANES] = kn[:, cols].astype(BF16)
        k_ref[:, hd * QK_PAD + LANES:(hd + 1) * QK_PAD] = kr
    vt = lax.dot_general(wuv_ref[...], c, (((1,), (1,)), ((), ())),
                         preferred_element_type=F32).astype(BF16)
    ones = jnp.ones((V_ROWS - V_HEAD, vt.shape[1]), BF16)
    for hd in range(MLA_HEADS):
        v_ref[hd * V_ROWS:hd * V_ROWS + V_HEAD, :] = vt[hd * V_HEAD:(hd + 1) * V_HEAD, :]
        v_ref[hd * V_ROWS + V_HEAD:(hd + 1) * V_ROWS, :] = ones


def _mla_proj(h, pos, g, w_dq, g_q, w_uq_n, w_uq_r, w_dkv, g_kv, w_uk, w_uv, invf, *, tm=512):
    s = h.shape[0]
    scale = (QK_NOPE + QK_ROPE) ** -0.5 * math.log2(math.e)
    consts = (g, w_dq, g_q, w_uq_n, w_uq_r, w_dkv, g_kv, w_uk, w_uv, invf)
    return pl.pallas_call(
        functools.partial(_mla_proj_body, scale=scale),
        grid=(s // tm,),
        in_specs=[_row_spec(tm, D_MODEL), _row_spec(tm, 1)] + [_const_spec(a.shape) for a in consts],
        out_specs=[_row_spec(tm, MLA_HEADS * QK_PAD), _row_spec(tm, MLA_HEADS * QK_PAD),
                   pl.BlockSpec((MLA_HEADS * V_ROWS, tm), lambda i: (0, i))],
        out_shape=[jax.ShapeDtypeStruct((s, MLA_HEADS * QK_PAD), BF16),
                   jax.ShapeDtypeStruct((s, MLA_HEADS * QK_PAD), BF16),
                   jax.ShapeDtypeStruct((MLA_HEADS * V_ROWS, s), BF16)],
        compiler_params=_params("parallel"),
        name="mla_proj",
    )(h, pos, *consts)


ATTN_SLOTS = 4
ATTN_UNROLL = 8
ATTN_MAX_JUMP = 64.0


def _attn_body(qa_ref, qb_ref, k_ref, vt_ref, oa_ref, ob_ref, qt_ref, m_ref, acc_ref, lag_ref, rl_ref,
               jump_ref, *slot_refs,
               t, nt):
    qa = pl.program_id(1)
    qb = nt - 1 - qa
    ns = ATTN_SLOTS
    s_refs, p_refs, a_refs, mt_refs = (slot_refs[i * ns:(i + 1) * ns] for i in range(4))
    neg = -1e30
    rows = 2 * SUBLANES
    last = nt

    qt_ref[0] = qa_ref[...].astype(F32).T.astype(BF16)
    qt_ref[1] = qb_ref[...].astype(F32).T.astype(BF16)
    m_ref[...] = jnp.full(m_ref.shape, neg, F32)
    acc_ref[...] = jnp.zeros(acc_ref.shape, F32)
    jump_ref[...] = jnp.zeros(jump_ref.shape, F32)

    def owner(j):
        return jnp.where(j < 2, j, (j >= qa + 2).astype(jnp.int32))

    def key_tile(j):
        full = jnp.where(j >= qa + 2, j - 2 - qa, j - 2)
        return jnp.where(j == 0, qa, jnp.where(j == 1, qb, full))

    def qk(j, slot, masked=False):
        start = pl.multiple_of(key_tile(j) * t, t)
        s = _dot(k_ref[pl.ds(start, t), :], qt_ref[owner(j)])
        if masked:
            kpos = lax.broadcasted_iota(jnp.int32, (t, t), 0)
            qpos = lax.broadcasted_iota(jnp.int32, (t, t), 1)
            s = jnp.where(kpos <= qpos, s, neg)
        s_refs[slot][...] = s
        nacc = 2
        mx = [s[r * SUBLANES:(r + 1) * SUBLANES, :] for r in range(nacc)]
        for r in range(nacc, t // SUBLANES):
            mx[r % nacc] = jnp.maximum(mx[r % nacc], s[r * SUBLANES:(r + 1) * SUBLANES, :])
        mt_refs[slot][...] = jnp.maximum(mx[0], mx[1])

    def softmax(j, slot):
        s_ref = s_refs[slot]
        p_ref = p_refs[slot]
        own = owner(j)
        mt = mt_refs[slot][...]
        m_old = m_ref[own]
        m_new = jnp.maximum(m_old, jnp.max(mt, axis=0, keepdims=True))
        m_ref[own] = m_new
        a_refs[slot][...] = jnp.exp2(m_old - m_new)
        mb = jnp.broadcast_to(m_new, (SUBLANES, t))
        for r in range(t // rows):
            lo = jnp.exp2(s_ref[r * rows:r * rows + SUBLANES, :] - mb)
            hi = jnp.exp2(s_ref[r * rows + SUBLANES:(r + 1) * rows, :] - mb)
            p_ref[r * rows:(r + 1) * rows, :] = jnp.concatenate([lo, hi], axis=0).astype(BF16)

    def pv(j, slot):
        start = pl.multiple_of(key_tile(j) * t, t)
        own = owner(j)
        acc_ref[own] = acc_ref[own] * a_refs[slot][...] + _dot(vt_ref[:, pl.ds(start, t)],
                                                                p_refs[slot][...])

    qk(0, 0, masked=True)
    qk(1, 1, masked=True)
    softmax(0, 0)
    softmax(1, 1)
    lag_ref[...] = m_ref[...]
    rl_ref[...] = m_ref[...]

    def fused(j, slot):
        own = owner(j)
        start = pl.multiple_of(key_tile(j) * t, t)
        ref = lag_ref[own]
        a_refs[slot][...] = jnp.exp2(rl_ref[own] - ref)
        rl_ref[own] = ref
        s = _dot(k_ref[pl.ds(start, t), :], qt_ref[own])
        rb = jnp.broadcast_to(ref, (SUBLANES, t))
        p_ref = p_refs[slot]
        mx = [None, None]
        for r in range(t // rows):
            lo = s[r * rows:r * rows + SUBLANES, :]
            hi = s[r * rows + SUBLANES:(r + 1) * rows, :]
            mx[0] = lo if mx[0] is None else jnp.maximum(mx[0], lo)
            mx[1] = hi if mx[1] is None else jnp.maximum(mx[1], hi)
            p_ref[r * rows:(r + 1) * rows, :] = jnp.concatenate(
                [jnp.exp2(lo - rb), jnp.exp2(hi - rb)], axis=0).astype(BF16)
        m_tile = jnp.max(jnp.maximum(mx[0], mx[1]), axis=0, keepdims=True)
        m_old = m_ref[own]
        lag_ref[own] = m_old
        m_ref[own] = jnp.maximum(m_old, m_tile)
        jump_ref[...] = jnp.maximum(jump_ref[...], m_tile - ref)

    def stage(k, slot):
        fused(k + 2, (slot + 2) % ns)
        pv(k, slot)

    nstage = nt - 1

    def body(jq, _):
        for u in range(ATTN_UNROLL):
            stage(ATTN_UNROLL * jq + u, u % ns)
        return 0

    lax.fori_loop(0, nstage // ATTN_UNROLL, body, 0)
    for k in range(nstage - nstage % ATTN_UNROLL, nstage):
        stage(k, k % ns)
    pv(last - 1, (last - 1) % ns)
    pv(last, last % ns)

    def finish():
        for own, o_ref in ((0, oa_ref), (1, ob_ref)):
            acc = acc_ref[own]
            o_t = acc[:V_HEAD, :] * (1.0 / acc[V_HEAD:V_HEAD + 1, :])
            o_ref[...] = o_t.T.astype(o_ref.dtype)

    finish()

    @pl.when(jnp.max(jump_ref[...]) > ATTN_MAX_JUMP)
    def _():
        m_ref[...] = jnp.full(m_ref.shape, neg, F32)
        acc_ref[...] = jnp.zeros(acc_ref.shape, F32)
        for j in (0, 1):
            qk(j, 0, masked=True)
            softmax(j, 0)
            pv(j, 0)

        def exact(j, _):
            qk(j, 0)
            softmax(j, 0)
            pv(j, 0)
            return 0

        lax.fori_loop(2, last + 1, exact, 0)
        finish()


def _attention(q, k, vt, *, t=512):
    s = q.shape[0]
    nt = s // t
    assert nt % 2 == 0 and nt % ATTN_UNROLL == 0
    half = nt // 2
    out = jax.ShapeDtypeStruct((s // 2, MLA_HEADS * V_HEAD), BF16)
    return pl.pallas_call(
        functools.partial(_attn_body, t=t, nt=nt),
        grid=(MLA_HEADS, half),
        in_specs=[pl.BlockSpec((t, QK_PAD), lambda hd, i: (i, hd)),
                  pl.BlockSpec((t, QK_PAD), lambda hd, i: (nt - 1 - i, hd)),
                  pl.BlockSpec((s, QK_PAD), lambda hd, i: (0, hd)),
                  pl.BlockSpec((V_ROWS, s), lambda hd, i: (hd, 0))],
        out_specs=[pl.BlockSpec((t, V_HEAD), lambda hd, i: (i, hd)),
                   pl.BlockSpec((t, V_HEAD), lambda hd, i: (half - 1 - i, hd))],
        out_shape=[out, out],
        scratch_shapes=[
            pltpu.VMEM((2, QK_PAD, t), BF16),
            pltpu.VMEM((2, 1, t), F32),
            pltpu.VMEM((2, V_ROWS, t), F32),
            pltpu.VMEM((2, 1, t), F32),
            pltpu.VMEM((2, 1, t), F32),
            pltpu.VMEM((1, t), F32),
        ] + [pltpu.VMEM((t, t), F32)] * ATTN_SLOTS
          + [pltpu.VMEM((t, t), BF16)] * ATTN_SLOTS
          + [pltpu.VMEM((1, t), F32)] * ATTN_SLOTS
          + [pltpu.VMEM((SUBLANES, t), F32)] * ATTN_SLOTS,
        compiler_params=_params("parallel", "parallel"),
        name="mla_attn",
    )(q, q, k, vt)


def _out_proj_body(h_ref, oa_ref, ob_ref, g_ref, wo_ref, wu_ref, wd_ref, o_ref, *, half):
    o = jnp.where(pl.program_id(0) < half, oa_ref[...], ob_ref[...])
    h = h_ref[...] + _rms(_dot(o, wo_ref[...]), g_ref[1:2, :])
    o_ref[...] = _mlp_tail(h, g_ref, wu_ref, wd_ref)


def _out_proj_mlp(h, oa, ob, g, w_o, w_up, w_down, layer, *, tm=512):
    s = h.shape[0]
    half = oa.shape[0] // tm
    return pl.pallas_call(
        functools.partial(_out_proj_body, half=half),
        grid=(s // tm,),
        in_specs=[_row_spec(tm, D_MODEL),
                  pl.BlockSpec((tm, D_MODEL), lambda i: (jnp.minimum(i, half - 1), 0)),
                  pl.BlockSpec((tm, D_MODEL), lambda i: (jnp.maximum(i - half, 0), 0)),
                  _const_spec(g.shape), _const_spec(w_o.shape)]
        + _mlp_weight_specs(w_up, w_down, layer),
        out_specs=_row_spec(tm, D_MODEL),
        out_shape=jax.ShapeDtypeStruct(h.shape, F32),
        compiler_params=_params("parallel"),
        name="mla_out_mlp",
    )(h, oa, ob, g, w_o, w_up, w_down)


def _s5_disc_body(lr_ref, li_ref, ldt_ref, br_ref, bi_ref, abr_ref, abi_ref, bbr_ref, bbi_ref):
    dt = jnp.exp(ldt_ref[...])
    lr = lr_ref[...]
    li = li_ref[...]
    mag = jnp.exp(lr * dt)
    ab_re = mag * jnp.cos(li * dt)
    ab_im = mag * jnp.sin(li * dt)
    den = lr * lr + li * li
    f_re = ((ab_re - 1.0) * lr + ab_im * li) / den
    f_im = (ab_im * lr - (ab_re - 1.0) * li) / den
    abr_ref[...] = ab_re
    abi_ref[...] = ab_im
    br = br_ref[...]
    bi = bi_ref[...]
    bbr_ref[...] = f_re[:, None, :] * br - f_im[:, None, :] * bi
    bbi_ref[...] = f_re[:, None, :] * bi + f_im[:, None, :] * br


def _s5_discretise(lam_re, lam_im, log_dt, b_re_t, b_im_t):
    g, p = lam_re.shape
    c = b_re_t.shape[1]
    return pl.pallas_call(
        _s5_disc_body,
        out_shape=[jax.ShapeDtypeStruct((g, p), F32), jax.ShapeDtypeStruct((g, p), F32),
                   jax.ShapeDtypeStruct((g, c, p), F32), jax.ShapeDtypeStruct((g, c, p), F32)],
        name="s5_disc",
    )(lam_re, lam_im, log_dt.reshape(g, 1), b_re_t, b_im_t)


def _s5_body(h_ref, g_ref, perm_ref, permt_ref, ar_ref, ai_ref, bblk_ref, cblk_ref, dskip_ref,
             wa_ref, wb_ref, o_ref, xr0_ref, xi0_ref, xr1_ref, xi1_ref, xc0_ref, xc1_ref,
             pr_ref, pi_ref, cr_ref, ci_ref, *, tm):
    seg = tm // SUBLANES
    i0 = pl.program_id(0)
    nst = S5_BLOCK_STATE
    pack = 2 * SUBLANES

    @pl.when(i0 == 0)
    def _():
        cr_ref[...] = jnp.zeros(cr_ref.shape, F32)
        ci_ref[...] = jnp.zeros(ci_ref.shape, F32)
        ar = ar_ref[...]
        ai = ai_ref[...]
        bcast = lambda v: jnp.broadcast_to(v, (SUBLANES, v.shape[1]))
        pr_ref[0:SUBLANES, :] = bcast(ar)
        pi_ref[0:SUBLANES, :] = bcast(ai)

        def pw(j, carry):
            r, im = carry
            r, im = r * ar - im * ai, r * ai + im * ar
            rows = pl.ds(pl.multiple_of(j * SUBLANES, SUBLANES), SUBLANES)
            pr_ref[rows, :] = bcast(r)
            pi_ref[rows, :] = bcast(im)
            return r, im

        lax.fori_loop(1, seg, pw, (ar, ai))

    h = h_ref[...]
    z = _rms(h, g_ref[0:1, :])
    zp = _dot(perm_ref[...], z.astype(BF16)).astype(BF16)

    yps = []
    for cb in range(S5_BLOCKS):
        st = slice(cb * nst, (cb + 1) * nst)
        xr_ref, xi_ref, xc_ref = ((xr0_ref, xi0_ref, xc0_ref), (xr1_ref, xi1_ref, xc1_ref))[cb % 2]
        bu = _dot(zp[:, cb * LANES:(cb + 1) * LANES], bblk_ref[cb])
        ar = jnp.broadcast_to(ar_ref[:, st], (SUBLANES, nst))
        ai = jnp.broadcast_to(ai_ref[:, st], (SUBLANES, nst))
        xr = jnp.zeros((SUBLANES, nst), F32)
        xi = xr
        for i in range(seg):
            rows = slice(i * SUBLANES, (i + 1) * SUBLANES)
            xr, xi = (ar * xr - ai * xi + bu[rows, :nst], ar * xi + ai * xr + bu[rows, nst:])
            xr_ref[rows, :] = xr
            xi_ref[rows, :] = xi
        alr = pr_ref[tm - 1:tm, st]
        ali = pi_ref[tm - 1:tm, st]
        c_r = cr_ref[:, st]
        c_i = ci_ref[:, st]
        rows_r, rows_i = [], []
        for sgm in range(SUBLANES):
            rows_r.append(c_r)
            rows_i.append(c_i)
            e_r = xr[sgm:sgm + 1, :]
            e_i = xi[sgm:sgm + 1, :]
            c_r, c_i = alr * c_r - ali * c_i + e_r, alr * c_i + ali * c_r + e_i
        cr_ref[:, st] = c_r
        ci_ref[:, st] = c_i
        cin_r = jnp.concatenate(rows_r, axis=0)
        cin_i = jnp.concatenate(rows_i, axis=0)
        for i2 in range(seg // 2):
            fr, fi = [], []
            for i in (2 * i2, 2 * i2 + 1):
                rows = slice(i * SUBLANES, (i + 1) * SUBLANES)
                p_r = pr_ref[rows, st]
                p_i = pi_ref[rows, st]
                fr.append(xr_ref[rows, :] + (p_r * cin_r - p_i * cin_i))
                fi.append(xi_ref[rows, :] + (p_r * cin_i + p_i * cin_r))
            rows2 = slice(i2 * pack, (i2 + 1) * pack)
            xc_ref[rows2, 0:nst] = jnp.concatenate(fr, axis=0).astype(BF16)
            xc_ref[rows2, nst:2 * nst] = jnp.concatenate(fi, axis=0).astype(BF16)
        yps.append(_dot(xc_ref[...], cblk_ref[cb]))

    yp = jnp.concatenate(yps, axis=1)
    yp_hi = yp.astype(BF16)
    yp_lo = (yp - yp_hi.astype(F32)).astype(BF16)
    y = _dot(permt_ref[...], yp_hi) + _dot(permt_ref[...], yp_lo)
    y = _gelu(y + dskip_ref[...] * z).astype(BF16)
    out = _dot(y, wa_ref[...]) * jax.nn.sigmoid(_dot(y, wb_ref[...]))
    o_ref[...] = h + _rms(out, g_ref[1:2, :])


def _s5_permutation(tm):
    seg = tm // SUBLANES
    r = jnp.arange(tm)
    src = (r % SUBLANES) * seg + r // SUBLANES
    return (src[:, None] == jnp.arange(tm)[None, :]).astype(BF16)


def _s5(h, g, ab_re, ab_im, bblk, cblk, dskip, w_a, w_b, *, tm=512):
    s = h.shape[0]
    seg = tm // SUBLANES
    nstate = S5_GROUPS * S5_STATE
    perm = _s5_permutation(tm)
    consts = (g, perm, perm.T, ab_re, ab_im, bblk, cblk, dskip, w_a, w_b)
    state = pltpu.VMEM((tm, S5_BLOCK_STATE), F32)
    return pl.pallas_call(
        functools.partial(_s5_body, tm=tm),
        grid=(s // tm,),
        in_specs=[_row_spec(tm, D_MODEL)] + [_const_spec(a.shape) for a in consts],
        out_specs=_row_spec(tm, D_MODEL),
        out_shape=jax.ShapeDtypeStruct(h.shape, F32),
        scratch_shapes=[
            state, state, state, state,
            pltpu.VMEM((tm, 2 * S5_BLOCK_STATE), BF16),
            pltpu.VMEM((tm, 2 * S5_BLOCK_STATE), BF16),
            pltpu.VMEM((tm, nstate), F32),
            pltpu.VMEM((tm, nstate), F32),
            pltpu.VMEM((1, nstate), F32),
            pltpu.VMEM((1, nstate), F32),
        ],
        compiler_params=_params("arbitrary"),
        name="s5",
    )(h, *consts)


def _s5_block_matrices(bb_re, bb_im, c_re, c_im):
    nb, gb, c, p = S5_BLOCKS, S5_BLOCK_GROUPS, S5_GROUP_CH, S5_STATE
    eye = jnp.eye(gb, dtype=F32)

    def in_map(bb):
        t = bb.reshape(nb, gb, c, p)
        return jnp.einsum('bgcp,gk->bgckp', t, eye).reshape(nb, gb * c, gb * p)

    def out_map(cc):
        t = cc.reshape(nb, gb, c, p)
        return jnp.einsum('bgcp,gk->bgpkc', t, eye).reshape(nb, gb * p, gb * c)

    bblk = jnp.concatenate([in_map(bb_re), in_map(bb_im)], axis=2).astype(BF16)
    cblk = jnp.concatenate([out_map(c_re), out_map(-c_im)], axis=1).astype(BF16)
    return bblk, cblk


def _mla_weights(w_uq, w_dkv):
    wq = w_uq.reshape(Q_LORA, MLA_HEADS, QK_NOPE + QK_ROPE)
    w_n = wq[:, :, :QK_NOPE].reshape(Q_LORA, MLA_HEADS * QK_NOPE)
    w_r = jnp.pad(wq[:, :, QK_NOPE:], ((0, 0), (0, 0), (0, LANES - QK_ROPE)))
    w_r = w_r.reshape(Q_LORA, MLA_HEADS * LANES)
    w_dkv_p = jnp.pad(w_dkv, ((0, 0), (0, LANES - QK_ROPE)))
    return w_n.astype(BF16), w_r.astype(BF16), w_dkv_p.astype(BF16)


def kernel(x, positions, norm_g, w_up, w_down, a_w_in, a_b_in, a_g_v, a_b_v, a_w_s, a_b_s, a_w_out, b_w_grp, b_scale, c_w_dq, c_g_q, c_w_uq, c_w_dkv, c_g_kv, c_w_uk, c_w_uv, c_w_o, d_lam_re, d_lam_im, d_log_dt, d_b_re, d_b_im, d_c_re, d_c_im, d_skip, d_w_glu_a, d_w_glu_b):
    bsz, s, d = x.shape
    assert bsz == 1 and d == D_MODEL
    h = x.reshape(s, d)
    row = lambda a: a.reshape(1, -1)
    w_up_b = w_up.astype(BF16)
    w_down_b = w_down.astype(BF16)

    b_s_full = jnp.repeat(a_b_s[0].T, D_MODEL // GMLP_HEADS, axis=1)
    h = _gmlp(h, norm_g[0], a_w_in[0].astype(BF16), row(a_b_in[0]), row(a_g_v[0]), row(a_b_v[0]),
              a_w_s[0], b_s_full, a_w_out[0].astype(BF16))
    h = _mlp(h, norm_g[0], w_up_b, w_down_b, 0)

    h = _pool_mlp(h, norm_g[1], b_w_grp[0].astype(BF16), row(b_scale[0]), w_up_b, w_down_b, 1)

    w_uq_n, w_uq_r, w_dkv_p = _mla_weights(c_w_uq[0], c_w_dkv[0])
    inv_freq = ROPE_THETA ** (-jnp.arange(0, QK_ROPE, 2, dtype=F32) / QK_ROPE)
    invf = jnp.concatenate([inv_freq, inv_freq, jnp.zeros((LANES - QK_ROPE,), F32)]).reshape(1, LANES)
    q, k, v = _mla_proj(h, positions.reshape(s, 1), norm_g[2], c_w_dq[0].astype(BF16), row(c_g_q[0]),
                        w_uq_n, w_uq_r, w_dkv_p, row(c_g_kv[0]), c_w_uk[0].astype(BF16),
                        c_w_uv[0].T.astype(BF16), invf)
    oa, ob = _attention(q, k, v)
    h = _out_proj_mlp(h, oa, ob, norm_g[2], c_w_o[0].astype(BF16), w_up_b, w_down_b, 2)

    tr = lambda a: jnp.swapaxes(a, 1, 2)
    ab_re, ab_im, bb_re, bb_im = _s5_discretise(d_lam_re[0], d_lam_im[0], d_log_dt[0],
                                                tr(d_b_re[0]), tr(d_b_im[0]))
    bblk, cblk = _s5_block_matrices(bb_re, bb_im, d_c_re[0], d_c_im[0])
    h = _s5(h, norm_g[3], row(ab_re), row(ab_im), bblk, cblk, row(d_skip[0]),
            d_w_glu_a[0].astype(BF16), d_w_glu_b[0].astype(BF16))
    h = _mlp(h, norm_g[3], w_up_b, w_down_b, 3)
    return h.reshape(bsz, s, d)
```

```python
import functools
import math

import jax
import jax.numpy as jnp
from jax import lax
from jax.experimental import pallas as pl
from jax.experimental.pallas import tpu as pltpu

F32 = jnp.float32
BF16 = jnp.bfloat16

D_MODEL = 1024
D_FF = 4 * D_MODEL
RMS_EPS = 1e-6
LANES = 128
SUBLANES = 8
VMEM_LIMIT = 56 * 1024 * 1024

CHUNK = 128
GMLP_HEADS = 8
POOL_WINDOWS = (2, 4, 8, 16)
POOL_GROUP = D_MODEL // len(POOL_WINDOWS)
POOL_HALO = 32
assert POOL_WINDOWS == tuple(2 << k for k in range(len(POOL_WINDOWS)))
assert POOL_HALO == SUBLANES * len(POOL_WINDOWS)
MLA_HEADS = 8
Q_LORA = 384
KV_LORA = 256
QK_NOPE = 128
QK_ROPE = 64
V_HEAD = 128
V_ROWS = V_HEAD + 16
ROPE_THETA = 10000.0
QK_PAD = 256
S5_GROUP_CH = 16
S5_GROUPS = D_MODEL // S5_GROUP_CH
S5_STATE = 64
S5_BLOCK_GROUPS = LANES // S5_GROUP_CH
S5_BLOCKS = S5_GROUPS // S5_BLOCK_GROUPS
S5_BLOCK_STATE = S5_BLOCK_GROUPS * S5_STATE


def _rms(x, g):
    return x * lax.rsqrt(jnp.mean(x * x, axis=-1, keepdims=True) + RMS_EPS) * g


def _gelu(x):
    c = math.sqrt(2.0 / math.pi)
    return 0.5 * x * (1.0 + jnp.tanh(c * (x + 0.044715 * (x * x * x))))


def _dot(a, b):
    return jnp.dot(a, b, preferred_element_type=F32)


def _const_spec(shape):
    nd = len(shape)
    return pl.BlockSpec(shape, lambda *_: (0,) * nd, pipeline_mode=pl.Buffered(1))


def _row_spec(tm, width):
    return pl.BlockSpec((tm, width), lambda i: (i, 0))


def _params(*sem, flags=None):
    return pltpu.CompilerParams(dimension_semantics=sem, vmem_limit_bytes=VMEM_LIMIT, flags=flags)


MLP_TF = 1024


def _mlp_tail(h, g_ref, wu_ref, wd_ref):
    z = _rms(h, g_ref[2:3, :]).astype(BF16)
    acc = jnp.zeros(h.shape, F32)
    for c in range(D_FF // MLP_TF):
        a = _dot(z, wu_ref[:, c * MLP_TF:(c + 1) * MLP_TF])
        a = jnp.square(jnp.maximum(a, 0.0)).astype(BF16)
        acc = acc + _dot(a, wd_ref[c * MLP_TF:(c + 1) * MLP_TF, :])
    return h + _rms(acc, g_ref[3:4, :])


def _mlp_weight_specs(w_up, w_down, layer):
    return [pl.BlockSpec((None,) + w.shape[1:], lambda *_: (layer, 0, 0), pipeline_mode=pl.Buffered(1))
            for w in (w_up, w_down)]


def _mlp_body(h_ref, g_ref, wu_ref, wd_ref, o_ref):
    o_ref[...] = _mlp_tail(h_ref[...], g_ref, wu_ref, wd_ref)


def _mlp(h, g, w_up, w_down, layer, *, tm=512):
    s = h.shape[0]
    return pl.pallas_call(
        _mlp_body,
        grid=(s // tm,),
        in_specs=[_row_spec(tm, D_MODEL), _const_spec(g.shape)] + _mlp_weight_specs(w_up, w_down, layer),
        out_specs=_row_spec(tm, D_MODEL),
        out_shape=jax.ShapeDtypeStruct(h.shape, F32),
        compiler_params=_params("parallel"),
        name="mlp",
    )(h, g, w_up, w_down)


def _gmlp_body(h_ref, g_ref, win_ref, bin_ref, gv_ref, bv_ref, ws_ref, bs_ref, wout_ref, o_ref,
               sv_ref, *, tm):
    h = h_ref[...]
    z = _rms(h, g_ref[0:1, :]).astype(BF16)
    uv = _gelu(_dot(z, win_ref[...]) + bin_ref[...])
    u = uv[:, :D_MODEL]
    v = uv[:, D_MODEL:]
    mu = jnp.mean(v, axis=-1, keepdims=True)
    vc = v - mu
    v = vc * lax.rsqrt(jnp.mean(vc * vc, axis=-1, keepdims=True) + RMS_EPS)
    v = (v * gv_ref[...] + bv_ref[...]).astype(BF16)
    row = lax.broadcasted_iota(jnp.int32, (CHUNK, CHUNK), 0)
    col = lax.broadcasted_iota(jnp.int32, (CHUNK, CHUNK), 1)
    nchunk = tm // CHUNK
    for hd in range(GMLP_HEADS):
        ws = jnp.where(col <= row, ws_ref[hd], 0.0).astype(BF16)
        cols = slice(hd * LANES, (hd + 1) * LANES)
        rhs = jnp.concatenate([v[c * CHUNK:(c + 1) * CHUNK, cols] for c in range(nchunk)], axis=1)
        sv = _dot(ws, rhs)
        for c in range(nchunk):
            sv_ref[c * CHUNK:(c + 1) * CHUNK, cols] = (
                sv[:, c * LANES:(c + 1) * LANES] + bs_ref[:, cols])
    y = (u * sv_ref[...]).astype(BF16)
    o_ref[...] = h + _rms(_dot(y, wout_ref[...]), g_ref[1:2, :])


def _gmlp(h, g, w_in, b_in, g_v, b_v, w_s, b_s_full, w_out, *, tm=512):
    s = h.shape[0]
    return pl.pallas_call(
        functools.partial(_gmlp_body, tm=tm),
        grid=(s // tm,),
        in_specs=[_row_spec(tm, D_MODEL), _const_spec(g.shape), _const_spec(w_in.shape),
                  _const_spec(b_in.shape), _const_spec(g_v.shape), _const_spec(b_v.shape),
                  _const_spec(w_s.shape), _const_spec(b_s_full.shape), _const_spec(w_out.shape)],
        out_specs=_row_spec(tm, D_MODEL),
        out_shape=jax.ShapeDtypeStruct(h.shape, F32),
        scratch_shapes=[pltpu.VMEM((tm, D_MODEL), F32)],
        compiler_params=_params("parallel"),
        name="gmlp",
    )(h, g, w_in, b_in, g_v, b_v, w_s, b_s_full, w_out)


def _pool_body(h_ref, halo_ref, g_ref, w_ref, scale_ref, wu_ref, wd_ref, o_ref, zz_ref, ws_ref,
               *, tm):
    i = pl.program_id(0)
    h = h_ref[...]
    z = _rms(h, g_ref[0:1, :])
    zh = _rms(halo_ref[...], g_ref[0:1, :])
    zz_ref[0:POOL_HALO, :] = jnp.where(i > 0, zh, 0.0)
    zz_ref[POOL_HALO:, :] = z
    nrows = tm + POOL_HALO
    for k in range(1, len(POOL_WINDOWS) + 1):
        d = 1 << (k - 1)
        lo = SUBLANES * k
        cols = slice((k - 1) * POOL_GROUP, D_MODEL)
        src = zz_ref if k == 1 else ws_ref
        ws_ref[lo:, cols] = src[lo:, cols] + src[lo - d:nrows - d, cols]
    pos = i * tm + lax.broadcasted_iota(jnp.int32, (tm, 1), 0)
    outs = []
    for gi, w in enumerate(POOL_WINDOWS):
        cols = slice(gi * POOL_GROUP, (gi + 1) * POOL_GROUP)
        xg = z[:, cols]
        win = ws_ref[POOL_HALO:, cols]
        count = jnp.minimum(pos + 1, w).astype(F32)
        diff = (win / count - xg).astype(BF16)
        outs.append(_dot(diff, w_ref[gi]))
    out = jnp.concatenate(outs, axis=-1) * scale_ref[...]
    o_ref[...] = _mlp_tail(h + _rms(out, g_ref[1:2, :]), g_ref, wu_ref, wd_ref)


def _pool_mlp(h, g, w_grp, scale, w_up, w_down, layer, *, tm=512):
    s = h.shape[0]
    ratio = tm // POOL_HALO
    return pl.pallas_call(
        functools.partial(_pool_body, tm=tm),
        grid=(s // tm,),
        in_specs=[_row_spec(tm, D_MODEL),
                  pl.BlockSpec((POOL_HALO, D_MODEL), lambda i: (jnp.maximum(i * ratio - 1, 0), 0)),
                  _const_spec(g.shape), _const_spec(w_grp.shape), _const_spec(scale.shape)]
        + _mlp_weight_specs(w_up, w_down, layer),
        out_specs=_row_spec(tm, D_MODEL),
        out_shape=jax.ShapeDtypeStruct(h.shape, F32),
        scratch_shapes=[pltpu.VMEM((tm + POOL_HALO, D_MODEL), F32)] * 2,
        compiler_params=_params("parallel"),
        name="pool_mlp",
    )(h, h, g, w_grp, scale, w_up, w_down)


def _rope_tile(x, cos, sin_lo, sin_hi):
    n = x.shape[1] // LANES
    outs = []
    for t in range(n):
        xt = x[:, t * LANES:(t + 1) * LANES]
        up = pltpu.roll(xt, 32, axis=1)
        down = pltpu.roll(xt, LANES - 32, axis=1)
        outs.append(xt * cos + down * sin_lo + up * sin_hi)
    return outs


def _mla_proj_body(h_ref, pos_ref, g_ref, wdq_ref, gq_ref, wuqn_ref, wuqr_ref, wdkv_ref, gkv_ref,
                   wuk_ref, wuv_ref, invf_ref, q_ref, k_ref, v_ref, *, scale):
    h = h_ref[...]
    z = _rms(h, g_ref[0:1, :]).astype(BF16)
    ang = pos_ref[...].astype(F32) * invf_ref[...]
    lane = lax.broadcasted_iota(jnp.int32, ang.shape, 1)
    cos = jnp.where(lane < QK_ROPE, jnp.cos(ang), 0.0)
    sin = jnp.sin(ang)
    sin_lo = jnp.where(lane < QK_ROPE // 2, -sin, 0.0)
    sin_hi = jnp.where((lane >= QK_ROPE // 2) & (lane < QK_ROPE), sin, 0.0)

    ql = _rms(_dot(z, wdq_ref[...]), gq_ref[...]).astype(BF16)
    qn = _dot(ql, wuqn_ref[...]) * scale
    qr = _rope_tile(_dot(ql, wuqr_ref[...]) * scale, cos, sin_lo, sin_hi)
    ckv = _dot(z, wdkv_ref[...])
    c = _rms(ckv[:, :KV_LORA], gkv_ref[...]).astype(BF16)
    kr = _rope_tile(ckv[:, KV_LORA:], cos, sin_lo, sin_hi)[0].astype(BF16)
    kn = _dot(c, wuk_ref[...])
    for hd in range(MLA_HEADS):
        cols = slice(hd * LANES, (hd + 1) * LANES)
        q_ref[:, hd * QK_PAD:hd * QK_PAD + LANES] = qn[:, cols].astype(BF16)
        q_ref[:, hd * QK_PAD + LANES:(hd + 1) * QK_PAD] = qr[hd].astype(BF16)
        k_ref[:, hd * QK_PAD:hd * QK_PAD + LANES] = kn[:, cols].astype(BF16)
        k_ref[:, hd * QK_PAD + LANES:(hd + 1) * QK_PAD] = kr
    vt = lax.dot_general(wuv_ref[...], c, (((1,), (1,)), ((), ())),
                         preferred_element_type=F32).astype(BF16)
    ones = jnp.ones((V_ROWS - V_HEAD, vt.shape[1]), BF16)
    for hd in range(MLA_HEADS):
        v_ref[hd * V_ROWS:hd * V_ROWS + V_HEAD, :] = vt[hd * V_HEAD:(hd + 1) * V_HEAD, :]
        v_ref[hd * V_ROWS + V_HEAD:(hd + 1) * V_ROWS, :] = ones


def _mla_proj(h, pos, g, w_dq, g_q, w_uq_n, w_uq_r, w_dkv, g_kv, w_uk, w_uv, invf, *, tm=512):
    s = h.shape[0]
    scale = (QK_NOPE + QK_ROPE) ** -0.5 * math.log2(math.e)
    consts = (g, w_dq, g_q, w_uq_n, w_uq_r, w_dkv, g_kv, w_uk, w_uv, invf)
    return pl.pallas_call(
        functools.partial(_mla_proj_body, scale=scale),
        grid=(s // tm,),
        in_specs=[_row_spec(tm, D_MODEL), _row_spec(tm, 1)] + [_const_spec(a.shape) for a in consts],
        out_specs=[_row_spec(tm, MLA_HEADS * QK_PAD), _row_spec(tm, MLA_HEADS * QK_PAD),
                   pl.BlockSpec((MLA_HEADS * V_ROWS, tm), lambda i: (0, i))],
        out_shape=[jax.ShapeDtypeStruct((s, MLA_HEADS * QK_PAD), BF16),
                   jax.ShapeDtypeStruct((s, MLA_HEADS * QK_PAD), BF16),
                   jax.ShapeDtypeStruct((MLA_HEADS * V_ROWS, s), BF16)],
        compiler_params=_params("parallel"),
        name="mla_proj",
    )(h, pos, *consts)


ATTN_SLOTS = 4
ATTN_UNROLL = 8
ATTN_MAX_JUMP = 64.0


def _attn_body(qa_ref, qb_ref, k_ref, vt_ref, oa_ref, ob_ref, qt_ref, m_ref, acc_ref, lag_ref, rl_ref,
               jump_ref, low_ref, s_ref, *slot_refs, t, nt):
    qa = pl.program_id(1)
    qb = nt - 1 - qa
    ns = ATTN_SLOTS
    p_refs, a_refs = slot_refs[:ns], slot_refs[ns:]
    neg = -1e30
    rows = 2 * SUBLANES
    last = nt

    qt_ref[0] = qa_ref[...].astype(F32).T.astype(BF16)
    qt_ref[1] = qb_ref[...].astype(F32).T.astype(BF16)
    for state in (m_ref, acc_ref, lag_ref, rl_ref, jump_ref, low_ref):
        state[...] = jnp.zeros(state.shape, F32)

    def owner(j):
        return jnp.where(j < 2, j, (j >= qa + 2).astype(jnp.int32))

    def key_tile(j):
        full = jnp.where(j >= qa + 2, j - 2 - qa, j - 2)
        return jnp.where(j == 0, qa, jnp.where(j == 1, qb, full))

    def scores(j, masked):
        start = pl.multiple_of(key_tile(j) * t, t)
        s = _dot(k_ref[pl.ds(start, t), :], qt_ref[owner(j)])
        if masked:
            kpos = lax.broadcasted_iota(jnp.int32, (t, t), 0)
            qpos = lax.broadcasted_iota(jnp.int32, (t, t), 1)
            s = jnp.where(kpos <= qpos, s, neg)
        return s

    def fused(j, slot, masked=False):
        own = owner(j)
        ref = lag_ref[own]
        a_refs[slot][...] = jnp.exp2(rl_ref[own] - ref)
        rl_ref[own] = ref
        s = scores(j, masked)
        rb = jnp.broadcast_to(ref, (SUBLANES, t))
        p_ref = p_refs[slot]
        mx = [None, None]
        for r in range(t // rows):
            lo = s[r * rows:r * rows + SUBLANES, :]
            hi = s[r * rows + SUBLANES:(r + 1) * rows, :]
            mx[0] = lo if mx[0] is None else jnp.maximum(mx[0], lo)
            mx[1] = hi if mx[1] is None else jnp.maximum(mx[1], hi)
            p_ref[r * rows:(r + 1) * rows, :] = jnp.concatenate(
                [jnp.exp2(lo - rb), jnp.exp2(hi - rb)], axis=0).astype(BF16)
        m_tile = jnp.max(jnp.maximum(mx[0], mx[1]), axis=0, keepdims=True)
        m_old = m_ref[own]
        lag_ref[own] = m_old
        m_ref[own] = jnp.maximum(m_old, m_tile)
        jump_ref[...] = jnp.maximum(jump_ref[...], m_tile - ref)
        if masked:
            low_ref[...] = jnp.minimum(low_ref[...], m_tile - ref)

    def pv(j, slot):
        start = pl.multiple_of(key_tile(j) * t, t)
        own = owner(j)
        acc_ref[own] = acc_ref[own] * a_refs[slot][...] + _dot(vt_ref[:, pl.ds(start, t)],
                                                                p_refs[slot][...])

    def stage(k, slot):
        fused(k + 2, (slot + 2) % ns)
        pv(k, slot)

    fused(0, 0, masked=True)
    fused(1, 1, masked=True)
    nstage = nt - 1

    def body(jq, _):
        for u in range(ATTN_UNROLL):
            stage(ATTN_UNROLL * jq + u, u % ns)
        return 0

    lax.fori_loop(0, nstage // ATTN_UNROLL, body, 0)
    for k in range(nstage - nstage % ATTN_UNROLL, nstage):
        stage(k, k % ns)
    pv(last - 1, (last - 1) % ns)
    pv(last, last % ns)

    def finish():
        for own, o_ref in ((0, oa_ref), (1, ob_ref)):
            acc = acc_ref[own]
            o_t = acc[:V_HEAD, :] * (1.0 / acc[V_HEAD:V_HEAD + 1, :])
            o_ref[...] = o_t.T.astype(o_ref.dtype)

    finish()

    @pl.when((jnp.max(jump_ref[...]) > ATTN_MAX_JUMP) | (jnp.min(low_ref[...]) < -ATTN_MAX_JUMP))
    def _():
        m_ref[...] = jnp.full(m_ref.shape, neg, F32)
        acc_ref[...] = jnp.zeros(acc_ref.shape, F32)

        def exact(j, masked):
            own = owner(j)
            s_ref[...] = scores(j, masked)
            mx = s_ref[0:SUBLANES, :]
            for r in range(1, t // SUBLANES):
                mx = jnp.maximum(mx, s_ref[r * SUBLANES:(r + 1) * SUBLANES, :])
            m_old = m_ref[own]
            m_new = jnp.maximum(m_old, jnp.max(mx, axis=0, keepdims=True))
            m_ref[own] = m_new
            a_refs[0][...] = jnp.exp2(m_old - m_new)
            mb = jnp.broadcast_to(m_new, (rows, t))
            for r in range(t // rows):
                p_refs[0][r * rows:(r + 1) * rows, :] = jnp.exp2(
                    s_ref[r * rows:(r + 1) * rows, :] - mb).astype(BF16)
            pv(j, 0)

        exact(0, True)
        exact(1, True)
        lax.fori_loop(2, last + 1, lambda j, c: (exact(j, False), c)[1], 0)
        finish()


def _attention(q, k, vt, *, t=512):
    s = q.shape[0]
    nt = s // t
    assert nt % 2 == 0
    half = nt // 2
    out = jax.ShapeDtypeStruct((s // 2, MLA_HEADS * V_HEAD), BF16)
    vec = pltpu.VMEM((2, 1, t), F32)
    return pl.pallas_call(
        functools.partial(_attn_body, t=t, nt=nt),
        grid=(MLA_HEADS, half),
        in_specs=[pl.BlockSpec((t, QK_PAD), lambda hd, i: (i, hd)),
                  pl.BlockSpec((t, QK_PAD), lambda hd, i: (nt - 1 - i, hd)),
                  pl.BlockSpec((s, QK_PAD), lambda hd, i: (0, hd)),
                  pl.BlockSpec((V_ROWS, s), lambda hd, i: (hd, 0))],
        out_specs=[pl.BlockSpec((t, V_HEAD), lambda hd, i: (i, hd)),
                   pl.BlockSpec((t, V_HEAD), lambda hd, i: (half - 1 - i, hd))],
        out_shape=[out, out],
        scratch_shapes=[
            pltpu.VMEM((2, QK_PAD, t), BF16),
            vec,
            pltpu.VMEM((2, V_ROWS, t), F32),
            vec,
            vec,
            pltpu.VMEM((1, t), F32),
            pltpu.VMEM((1, t), F32),
            pltpu.VMEM((t, t), F32),
        ] + [pltpu.VMEM((t, t), BF16)] * ATTN_SLOTS
          + [pltpu.VMEM((1, t), F32)] * ATTN_SLOTS,
        compiler_params=_params("parallel", "parallel"),
        name="mla_attn",
    )(q, q, k, vt)


def _out_proj_body(h_ref, oa_ref, ob_ref, g_ref, wo_ref, wu_ref, wd_ref, o_ref, *, half):
    o = jnp.where(pl.program_id(0) < half, oa_ref[...], ob_ref[...])
    h = h_ref[...] + _rms(_dot(o, wo_ref[...]), g_ref[1:2, :])
    o_ref[...] = _mlp_tail(h, g_ref, wu_ref, wd_ref)


def _out_proj_mlp(h, oa, ob, g, w_o, w_up, w_down, layer, *, tm=512):
    s = h.shape[0]
    half = oa.shape[0] // tm
    return pl.pallas_call(
        functools.partial(_out_proj_body, half=half),
        grid=(s // tm,),
        in_specs=[_row_spec(tm, D_MODEL),
                  pl.BlockSpec((tm, D_MODEL), lambda i: (jnp.minimum(i, half - 1), 0)),
                  pl.BlockSpec((tm, D_MODEL), lambda i: (jnp.maximum(i - half, 0), 0)),
                  _const_spec(g.shape), _const_spec(w_o.shape)]
        + _mlp_weight_specs(w_up, w_down, layer),
        out_specs=_row_spec(tm, D_MODEL),
        out_shape=jax.ShapeDtypeStruct(h.shape, F32),
        compiler_params=_params("parallel"),
        name="mla_out_mlp",
    )(h, oa, ob, g, w_o, w_up, w_down)


def _s5_disc_body(lr_ref, li_ref, ldt_ref, br_ref, bi_ref, abr_ref, abi_ref, bbr_ref, bbi_ref):
    dt = jnp.exp(ldt_ref[...])
    lr = lr_ref[...]
    li = li_ref[...]
    mag = jnp.exp(lr * dt)
    ab_re = mag * jnp.cos(li * dt)
    ab_im = mag * jnp.sin(li * dt)
    den = lr * lr + li * li
    f_re = ((ab_re - 1.0) * lr + ab_im * li) / den
    f_im = (ab_im * lr - (ab_re - 1.0) * li) / den
    abr_ref[...] = ab_re
    abi_ref[...] = ab_im
    br = br_ref[...]
    bi = bi_ref[...]
    bbr_ref[...] = f_re[:, None, :] * br - f_im[:, None, :] * bi
    bbi_ref[...] = f_re[:, None, :] * bi + f_im[:, None, :] * br


def _s5_discretise(lam_re, lam_im, log_dt, b_re_t, b_im_t):
    g, p = lam_re.shape
    c = b_re_t.shape[1]
    return pl.pallas_call(
        _s5_disc_body,
        out_shape=[jax.ShapeDtypeStruct((g, p), F32), jax.ShapeDtypeStruct((g, p), F32),
                   jax.ShapeDtypeStruct((g, c, p), F32), jax.ShapeDtypeStruct((g, c, p), F32)],
        name="s5_disc",
    )(lam_re, lam_im, log_dt.reshape(g, 1), b_re_t, b_im_t)


def _s5_body(h_ref, g_ref, perm_ref, ar_ref, ai_ref, bblk_ref, cblk_ref, dskip_ref,
             wa_ref, wb_ref, o_ref, xr0_ref, xi0_ref, xr1_ref, xi1_ref, xc0_ref, xc1_ref,
             yp_ref, y_ref, pr_ref, pi_ref, cr_ref, ci_ref, *, tm):
    seg = tm // SUBLANES
    i0 = pl.program_id(0)
    nst = S5_BLOCK_STATE
    pack = 2 * SUBLANES

    @pl.when(i0 == 0)
    def _():
        cr_ref[...] = jnp.zeros(cr_ref.shape, F32)
        ci_ref[...] = jnp.zeros(ci_ref.shape, F32)
        ar = ar_ref[...]
        ai = ai_ref[...]
        bcast = lambda v: jnp.broadcast_to(v, (SUBLANES, v.shape[1]))
        pr_ref[0:SUBLANES, :] = bcast(ar)
        pi_ref[0:SUBLANES, :] = bcast(ai)

        def pw(j, carry):
            r, im = carry
            r, im = r * ar - im * ai, r * ai + im * ar
            rows = pl.ds(pl.multiple_of(j * SUBLANES, SUBLANES), SUBLANES)
            pr_ref[rows, :] = bcast(r)
            pi_ref[rows, :] = bcast(im)
            return r, im

        lax.fori_loop(1, seg, pw, (ar, ai))

    h = h_ref[...]
    z = _rms(h, g_ref[0:1, :])
    zp = _dot(perm_ref[...], z.astype(BF16)).astype(BF16)

    for cb in range(S5_BLOCKS):
        st = slice(cb * nst, (cb + 1) * nst)
        xr_ref, xi_ref, xc_ref = ((xr0_ref, xi0_ref, xc0_ref), (xr1_ref, xi1_ref, xc1_ref))[cb % 2]
        bu = _dot(zp[:, cb * LANES:(cb + 1) * LANES], bblk_ref[cb])
        ar = jnp.broadcast_to(ar_ref[:, st], (SUBLANES, nst))
        ai = jnp.broadcast_to(ai_ref[:, st], (SUBLANES, nst))
        xr = jnp.zeros((SUBLANES, nst), F32)
        xi = xr
        for i in range(seg):
            rows = slice(i * SUBLANES, (i + 1) * SUBLANES)
            xr, xi = (ar * xr - ai * xi + bu[rows, :nst], ar * xi + ai * xr + bu[rows, nst:])
            xr_ref[rows, :] = xr
            xi_ref[rows, :] = xi
        alr = pr_ref[tm - 1:tm, st]
        ali = pi_ref[tm - 1:tm, st]
        c_r = cr_ref[:, st]
        c_i = ci_ref[:, st]
        rows_r, rows_i = [], []
        for sgm in range(SUBLANES):
            rows_r.append(c_r)
            rows_i.append(c_i)
            e_r = xr[sgm:sgm + 1, :]
            e_i = xi[sgm:sgm + 1, :]
            c_r, c_i = alr * c_r - ali * c_i + e_r, alr * c_i + ali * c_r + e_i
        cr_ref[:, st] = c_r
        ci_ref[:, st] = c_i
        cin_r = jnp.concatenate(rows_r, axis=0)
        cin_i = jnp.concatenate(rows_i, axis=0)
        for i2 in range(seg // 2):
            fr, fi = [], []
            for i in (2 * i2, 2 * i2 + 1):
                rows = slice(i * SUBLANES, (i + 1) * SUBLANES)
                p_r = pr_ref[rows, st]
                p_i = pi_ref[rows, st]
                fr.append(xr_ref[rows, :] + (p_r * cin_r - p_i * cin_i))
                fi.append(xi_ref[rows, :] + (p_r * cin_i + p_i * cin_r))
            rows2 = slice(i2 * pack, (i2 + 1) * pack)
            xc_ref[rows2, 0:nst] = jnp.concatenate(fr, axis=0).astype(BF16)
            xc_ref[rows2, nst:2 * nst] = jnp.concatenate(fi, axis=0).astype(BF16)
        yp_ref[cb] = _dot(xc_ref[...], cblk_ref[cb])

    for sgm in range(SUBLANES):
        for j in range(seg // SUBLANES):
            dst = slice(sgm * seg + j * SUBLANES, sgm * seg + (j + 1) * SUBLANES)
            src = pl.ds(j * SUBLANES * SUBLANES + sgm, SUBLANES, stride=SUBLANES)
            for cb in range(S5_BLOCKS):
                y_ref[dst, cb * LANES:(cb + 1) * LANES] = yp_ref[cb, src, :]
    y = _gelu(y_ref[...] + dskip_ref[...] * z).astype(BF16)
    out = _dot(y, wa_ref[...]) * jax.nn.sigmoid(_dot(y, wb_ref[...]))
    o_ref[...] = h + _rms(out, g_ref[1:2, :])


def _s5_permutation(tm):
    seg = tm // SUBLANES
    r = jnp.arange(tm)
    src = (r % SUBLANES) * seg + r // SUBLANES
    return (src[:, None] == jnp.arange(tm)[None, :]).astype(BF16)


def _s5(h, g, ab_re, ab_im, bblk, cblk, dskip, w_a, w_b, *, tm=512):
    s = h.shape[0]
    seg = tm // SUBLANES
    nstate = S5_GROUPS * S5_STATE
    perm = _s5_permutation(tm)
    consts = (g, perm, ab_re, ab_im, bblk, cblk, dskip, w_a, w_b)
    state = pltpu.VMEM((tm, S5_BLOCK_STATE), F32)
    return pl.pallas_call(
        functools.partial(_s5_body, tm=tm),
        grid=(s // tm,),
        in_specs=[_row_spec(tm, D_MODEL)] + [_const_spec(a.shape) for a in consts],
        out_specs=_row_spec(tm, D_MODEL),
        out_shape=jax.ShapeDtypeStruct(h.shape, F32),
        scratch_shapes=[
            state, state, state, state,
            pltpu.VMEM((tm, 2 * S5_BLOCK_STATE), BF16),
            pltpu.VMEM((tm, 2 * S5_BLOCK_STATE), BF16),
            pltpu.VMEM((S5_BLOCKS, tm, LANES), F32),
            pltpu.VMEM((tm, D_MODEL), F32),
            pltpu.VMEM((tm, nstate), F32),
            pltpu.VMEM((tm, nstate), F32),
            pltpu.VMEM((1, nstate), F32),
            pltpu.VMEM((1, nstate), F32),
        ],
        compiler_params=_params("arbitrary"),
        name="s5",
    )(h, *consts)


def _s5_block_matrices(bb_re, bb_im, c_re, c_im):
    nb, gb, c, p = S5_BLOCKS, S5_BLOCK_GROUPS, S5_GROUP_CH, S5_STATE
    eye = jnp.eye(gb, dtype=F32)

    def in_map(bb):
        t = bb.reshape(nb, gb, c, p)
        return jnp.einsum('bgcp,gk->bgckp', t, eye).reshape(nb, gb * c, gb * p)

    def out_map(cc):
        t = cc.reshape(nb, gb, c, p)
        return jnp.einsum('bgcp,gk->bgpkc', t, eye).reshape(nb, gb * p, gb * c)

    bblk = jnp.concatenate([in_map(bb_re), in_map(bb_im)], axis=2).astype(BF16)
    cblk = jnp.concatenate([out_map(c_re), out_map(-c_im)], axis=1).astype(BF16)
    return bblk, cblk


def _mla_weights(w_uq, w_dkv):
    wq = w_uq.reshape(Q_LORA, MLA_HEADS, QK_NOPE + QK_ROPE)
    w_n = wq[:, :, :QK_NOPE].reshape(Q_LORA, MLA_HEADS * QK_NOPE)
    w_r = jnp.pad(wq[:, :, QK_NOPE:], ((0, 0), (0, 0), (0, LANES - QK_ROPE)))
    w_r = w_r.reshape(Q_LORA, MLA_HEADS * LANES)
    w_dkv_p = jnp.pad(w_dkv, ((0, 0), (0, LANES - QK_ROPE)))
    return w_n.astype(BF16), w_r.astype(BF16), w_dkv_p.astype(BF16)


def kernel(x, positions, norm_g, w_up, w_down, a_w_in, a_b_in, a_g_v, a_b_v, a_w_s, a_b_s, a_w_out, b_w_grp, b_scale, c_w_dq, c_g_q, c_w_uq, c_w_dkv, c_g_kv, c_w_uk, c_w_uv, c_w_o, d_lam_re, d_lam_im, d_log_dt, d_b_re, d_b_im, d_c_re, d_c_im, d_skip, d_w_glu_a, d_w_glu_b):
    bsz, s, d = x.shape
    assert bsz == 1 and d == D_MODEL
    h = x.reshape(s, d)
    row = lambda a: a.reshape(1, -1)
    w_up_b = w_up.astype(BF16)
    w_down_b = w_down.astype(BF16)

    b_s_full = jnp.repeat(a_b_s[0].T, D_MODEL // GMLP_HEADS, axis=1)
    h = _gmlp(h, norm_g[0], a_w_in[0].astype(BF16), row(a_b_in[0]), row(a_g_v[0]), row(a_b_v[0]),
              a_w_s[0], b_s_full, a_w_out[0].astype(BF16))
    h = _mlp(h, norm_g[0], w_up_b, w_down_b, 0)

    h = _pool_mlp(h, norm_g[1], b_w_grp[0].astype(BF16), row(b_scale[0]), w_up_b, w_down_b, 1)

    w_uq_n, w_uq_r, w_dkv_p = _mla_weights(c_w_uq[0], c_w_dkv[0])
    inv_freq = ROPE_THETA ** (-jnp.arange(0, QK_ROPE, 2, dtype=F32) / QK_ROPE)
    invf = jnp.concatenate([inv_freq, inv_freq, jnp.zeros((LANES - QK_ROPE,), F32)]).reshape(1, LANES)
    q, k, v = _mla_proj(h, positions.reshape(s, 1), norm_g[2], c_w_dq[0].astype(BF16), row(c_g_q[0]),
                        w_uq_n, w_uq_r, w_dkv_p, row(c_g_kv[0]), c_w_uk[0].astype(BF16),
                        c_w_uv[0].T.astype(BF16), invf)
    oa, ob = _attention(q, k, v)
    h = _out_proj_mlp(h, oa, ob, norm_g[2], c_w_o[0].astype(BF16), w_up_b, w_down_b, 2)

    tr = lambda a: jnp.swapaxes(a, 1, 2)
    ab_re, ab_im, bb_re, bb_im = _s5_discretise(d_lam_re[0], d_lam_im[0], d_log_dt[0],
                                                tr(d_b_re[0]), tr(d_b_im[0]))
    bblk, cblk = _s5_block_matrices(bb_re, bb_im, d_c_re[0], d_c_im[0])
    h = _s5(h, norm_g[3], row(ab_re), row(ab_im), bblk, cblk, row(d_skip[0]),
            d_w_glu_a[0].astype(BF16), d_w_glu_b[0].astype(BF16))
    h = _mlp(h, norm_g[3], w_up_b, w_down_b, 3)
    return h.reshape(bsz, s, d)
```

```python
import functools
import math

import jax
import jax.numpy as jnp
from jax import lax
from jax.experimental import pallas as pl
from jax.experimental.pallas import tpu as pltpu

F32 = jnp.float32
BF16 = jnp.bfloat16

D_MODEL = 1024
D_FF = 4 * D_MODEL
RMS_EPS = 1e-6
LANES = 128
SUBLANES = 8
VMEM_LIMIT = 56 * 1024 * 1024

CHUNK = 128
GMLP_HEADS = 8
POOL_WINDOWS = (2, 4, 8, 16)
POOL_GROUP = D_MODEL // len(POOL_WINDOWS)
POOL_HALO = 32
assert POOL_WINDOWS == tuple(2 << k for k in range(len(POOL_WINDOWS)))
assert POOL_HALO == SUBLANES * len(POOL_WINDOWS)
MLA_HEADS = 8
Q_LORA = 384
KV_LORA = 256
QK_NOPE = 128
QK_ROPE = 64
V_HEAD = 128
V_ROWS = V_HEAD + 16
ROPE_THETA = 10000.0
QK_PAD = 256
S5_GROUP_CH = 16
S5_GROUPS = D_MODEL // S5_GROUP_CH
S5_STATE = 64
S5_BLOCK_GROUPS = LANES // S5_GROUP_CH
S5_BLOCKS = S5_GROUPS // S5_BLOCK_GROUPS
S5_BLOCK_STATE = S5_BLOCK_GROUPS * S5_STATE


def _rms(x, g):
    return x * lax.rsqrt(jnp.mean(x * x, axis=-1, keepdims=True) + RMS_EPS) * g


def _gelu(x):
    c = math.sqrt(2.0 / math.pi)
    return 0.5 * x * (1.0 + jnp.tanh(c * (x + 0.044715 * (x * x * x))))


def _dot(a, b):
    return jnp.dot(a, b, preferred_element_type=F32)


def _const_spec(shape):
    nd = len(shape)
    return pl.BlockSpec(shape, lambda *_: (0,) * nd, pipeline_mode=pl.Buffered(1))


def _row_spec(tm, width):
    return pl.BlockSpec((tm, width), lambda i: (i, 0))


def _params(*sem, flags=None):
    return pltpu.CompilerParams(dimension_semantics=sem, vmem_limit_bytes=VMEM_LIMIT, flags=flags)


MLP_TF = 1024


def _mlp_tail(h, g_ref, wu_ref, wd_ref):
    z = _rms(h, g_ref[2:3, :]).astype(BF16)
    acc = jnp.zeros(h.shape, F32)
    for c in range(D_FF // MLP_TF):
        a = _dot(z, wu_ref[:, c * MLP_TF:(c + 1) * MLP_TF])
        a = jnp.square(jnp.maximum(a, 0.0)).astype(BF16)
        acc = acc + _dot(a, wd_ref[c * MLP_TF:(c + 1) * MLP_TF, :])
    return h + _rms(acc, g_ref[3:4, :])


def _mlp_weight_specs(w_up, w_down, layer):
    return [pl.BlockSpec((None,) + w.shape[1:], lambda *_: (layer, 0, 0), pipeline_mode=pl.Buffered(1))
            for w in (w_up, w_down)]


def _mlp_body(h_ref, g_ref, wu_ref, wd_ref, o_ref):
    o_ref[...] = _mlp_tail(h_ref[...], g_ref, wu_ref, wd_ref)


def _mlp(h, g, w_up, w_down, layer, *, tm=512):
    s = h.shape[0]
    return pl.pallas_call(
        _mlp_body,
        grid=(s // tm,),
        in_specs=[_row_spec(tm, D_MODEL), _const_spec(g.shape)] + _mlp_weight_specs(w_up, w_down, layer),
        out_specs=_row_spec(tm, D_MODEL),
        out_shape=jax.ShapeDtypeStruct(h.shape, F32),
        compiler_params=_params("parallel"),
        name="mlp",
    )(h, g, w_up, w_down)


def _gmlp_body(h_ref, g_ref, win_ref, bin_ref, gv_ref, bv_ref, ws_ref, bs_ref, wout_ref, o_ref,
               sv_ref, *, tm):
    h = h_ref[...]
    z = _rms(h, g_ref[0:1, :]).astype(BF16)
    uv = _gelu(_dot(z, win_ref[...]) + bin_ref[...])
    u = uv[:, :D_MODEL]
    v = uv[:, D_MODEL:]
    mu = jnp.mean(v, axis=-1, keepdims=True)
    vc = v - mu
    v = vc * lax.rsqrt(jnp.mean(vc * vc, axis=-1, keepdims=True) + RMS_EPS)
    v = (v * gv_ref[...] + bv_ref[...]).astype(BF16)
    row = lax.broadcasted_iota(jnp.int32, (CHUNK, CHUNK), 0)
    col = lax.broadcasted_iota(jnp.int32, (CHUNK, CHUNK), 1)
    nchunk = tm // CHUNK
    for hd in range(GMLP_HEADS):
        ws = jnp.where(col <= row, ws_ref[hd], 0.0).astype(BF16)
        cols = slice(hd * LANES, (hd + 1) * LANES)
        rhs = jnp.concatenate([v[c * CHUNK:(c + 1) * CHUNK, cols] for c in range(nchunk)], axis=1)
        sv = _dot(ws, rhs)
        for c in range(nchunk):
            sv_ref[c * CHUNK:(c + 1) * CHUNK, cols] = (
                sv[:, c * LANES:(c + 1) * LANES] + bs_ref[:, cols])
    y = (u * sv_ref[...]).astype(BF16)
    o_ref[...] = h + _rms(_dot(y, wout_ref[...]), g_ref[1:2, :])


def _gmlp(h, g, w_in, b_in, g_v, b_v, w_s, b_s_full, w_out, *, tm=512):
    s = h.shape[0]
    return pl.pallas_call(
        functools.partial(_gmlp_body, tm=tm),
        grid=(s // tm,),
        in_specs=[_row_spec(tm, D_MODEL), _const_spec(g.shape), _const_spec(w_in.shape),
                  _const_spec(b_in.shape), _const_spec(g_v.shape), _const_spec(b_v.shape),
                  _const_spec(w_s.shape), _const_spec(b_s_full.shape), _const_spec(w_out.shape)],
        out_specs=_row_spec(tm, D_MODEL),
        out_shape=jax.ShapeDtypeStruct(h.shape, F32),
        scratch_shapes=[pltpu.VMEM((tm, D_MODEL), F32)],
        compiler_params=_params("parallel"),
        name="gmlp",
    )(h, g, w_in, b_in, g_v, b_v, w_s, b_s_full, w_out)


def _pool_body(h_ref, halo_ref, g_ref, w_ref, scale_ref, wu_ref, wd_ref, o_ref, zz_ref, ws_ref,
               *, tm):
    i = pl.program_id(0)
    h = h_ref[...]
    z = _rms(h, g_ref[0:1, :])
    zh = _rms(halo_ref[...], g_ref[0:1, :])
    zz_ref[0:POOL_HALO, :] = jnp.where(i > 0, zh, 0.0)
    zz_ref[POOL_HALO:, :] = z
    nrows = tm + POOL_HALO
    for k in range(1, len(POOL_WINDOWS) + 1):
        d = 1 << (k - 1)
        lo = SUBLANES * k
        cols = slice((k - 1) * POOL_GROUP, D_MODEL)
        src = zz_ref if k == 1 else ws_ref
        ws_ref[lo:, cols] = src[lo:, cols] + src[lo - d:nrows - d, cols]
    pos = i * tm + lax.broadcasted_iota(jnp.int32, (tm, 1), 0)
    outs = []
    for gi, w in enumerate(POOL_WINDOWS):
        cols = slice(gi * POOL_GROUP, (gi + 1) * POOL_GROUP)
        xg = z[:, cols]
        win = ws_ref[POOL_HALO:, cols]
        count = jnp.minimum(pos + 1, w).astype(F32)
        diff = (win / count - xg).astype(BF16)
        outs.append(_dot(diff, w_ref[gi]))
    out = jnp.concatenate(outs, axis=-1) * scale_ref[...]
    o_ref[...] = _mlp_tail(h + _rms(out, g_ref[1:2, :]), g_ref, wu_ref, wd_ref)


def _pool_mlp(h, g, w_grp, scale, w_up, w_down, layer, *, tm=512):
    s = h.shape[0]
    ratio = tm // POOL_HALO
    return pl.pallas_call(
        functools.partial(_pool_body, tm=tm),
        grid=(s // tm,),
        in_specs=[_row_spec(tm, D_MODEL),
                  pl.BlockSpec((POOL_HALO, D_MODEL), lambda i: (jnp.maximum(i * ratio - 1, 0), 0)),
                  _const_spec(g.shape), _const_spec(w_grp.shape), _const_spec(scale.shape)]
        + _mlp_weight_specs(w_up, w_down, layer),
        out_specs=_row_spec(tm, D_MODEL),
        out_shape=jax.ShapeDtypeStruct(h.shape, F32),
        scratch_shapes=[pltpu.VMEM((tm + POOL_HALO, D_MODEL), F32)] * 2,
        compiler_params=_params("parallel"),
        name="pool_mlp",
    )(h, h, g, w_grp, scale, w_up, w_down)


def _rope_tile(x, cos, sin_lo, sin_hi):
    n = x.shape[1] // LANES
    outs = []
    for t in range(n):
        xt = x[:, t * LANES:(t + 1) * LANES]
        up = pltpu.roll(xt, 32, axis=1)
        down = pltpu.roll(xt, LANES - 32, axis=1)
        outs.append(xt * cos + down * sin_lo + up * sin_hi)
    return outs


def _mla_proj_body(h_ref, pos_ref, g_ref, wdq_ref, gq_ref, wuqn_ref, wuqr_ref, wdkv_ref, gkv_ref,
                   wuk_ref, wuv_ref, invf_ref, q_ref, k_ref, v_ref, *, scale):
    h = h_ref[...]
    z = _rms(h, g_ref[0:1, :]).astype(BF16)
    ang = pos_ref[...].astype(F32) * invf_ref[...]
    lane = lax.broadcasted_iota(jnp.int32, ang.shape, 1)
    cos = jnp.where(lane < QK_ROPE, jnp.cos(ang), 0.0)
    sin = jnp.sin(ang)
    sin_lo = jnp.where(lane < QK_ROPE // 2, -sin, 0.0)
    sin_hi = jnp.where((lane >= QK_ROPE // 2) & (lane < QK_ROPE), sin, 0.0)

    ql = _rms(_dot(z, wdq_ref[...]), gq_ref[...]).astype(BF16)
    qn = _dot(ql, wuqn_ref[...]) * scale
    qr = _rope_tile(_dot(ql, wuqr_ref[...]) * scale, cos, sin_lo, sin_hi)
    ckv = _dot(z, wdkv_ref[...])
    c = _rms(ckv[:, :KV_LORA], gkv_ref[...]).astype(BF16)
    kr = _rope_tile(ckv[:, KV_LORA:], cos, sin_lo, sin_hi)[0].astype(BF16)
    kn = _dot(c, wuk_ref[...])
    for hd in range(MLA_HEADS):
        cols = slice(hd * LANES, (hd + 1) * LANES)
        q_ref[:, hd * QK_PAD:hd * QK_PAD + LANES] = qn[:, cols].astype(BF16)
        q_ref[:, hd * QK_PAD + LANES:(hd + 1) * QK_PAD] = qr[hd].astype(BF16)
        k_ref[:, hd * QK_PAD:hd * QK_PAD + LANES] = kn[:, cols].astype(BF16)
        k_ref[:, hd * QK_PAD + LANES:(hd + 1) * QK_PAD] = kr
    vt = lax.dot_general(wuv_ref[...], c, (((1,), (1,)), ((), ())),
                         preferred_element_type=F32).astype(BF16)
    ones = jnp.ones((V_ROWS - V_HEAD, vt.shape[1]), BF16)
    for hd in range(MLA_HEADS):
        v_ref[hd * V_ROWS:hd * V_ROWS + V_HEAD, :] = vt[hd * V_HEAD:(hd + 1) * V_HEAD, :]
        v_ref[hd * V_ROWS + V_HEAD:(hd + 1) * V_ROWS, :] = ones


def _mla_proj(h, pos, g, w_dq, g_q, w_uq_n, w_uq_r, w_dkv, g_kv, w_uk, w_uv, invf, *, tm=512):
    s = h.shape[0]
    scale = (QK_NOPE + QK_ROPE) ** -0.5 * math.log2(math.e)
    consts = (g, w_dq, g_q, w_uq_n, w_uq_r, w_dkv, g_kv, w_uk, w_uv, invf)
    return pl.pallas_call(
        functools.partial(_mla_proj_body, scale=scale),
        grid=(s // tm,),
        in_specs=[_row_spec(tm, D_MODEL), _row_spec(tm, 1)] + [_const_spec(a.shape) for a in consts],
        out_specs=[_row_spec(tm, MLA_HEADS * QK_PAD), _row_spec(tm, MLA_HEADS * QK_PAD),
                   pl.BlockSpec((MLA_HEADS * V_ROWS, tm), lambda i: (0, i))],
        out_shape=[jax.ShapeDtypeStruct((s, MLA_HEADS * QK_PAD), BF16),
                   jax.ShapeDtypeStruct((s, MLA_HEADS * QK_PAD), BF16),
                   jax.ShapeDtypeStruct((MLA_HEADS * V_ROWS, s), BF16)],
        compiler_params=_params("parallel"),
        name="mla_proj",
    )(h, pos, *consts)


ATTN_SLOTS = 4
ATTN_UNROLL = 16
ATTN_MAX_JUMP = 64.0


def _attn_body(qa_ref, qb_ref, k_ref, vt_ref, oa_ref, ob_ref, qt_ref, m_ref, acc_ref, lag_ref, rl_ref,
               jump_ref, low_ref, s_ref, *slot_refs, t, nt):
    qa = pl.program_id(1)
    qb = nt - 1 - qa
    ns = ATTN_SLOTS
    p_refs, a_refs = slot_refs[:ns], slot_refs[ns:]
    neg = -1e30
    rows = 2 * SUBLANES
    last = nt

    qt_ref[0] = qa_ref[...].astype(F32).T.astype(BF16)
    qt_ref[1] = qb_ref[...].astype(F32).T.astype(BF16)
    for state in (m_ref, acc_ref, lag_ref, rl_ref, jump_ref, low_ref):
        state[...] = jnp.zeros(state.shape, F32)

    def owner(j):
        return jnp.where(j < 2, j, (j >= qa + 2).astype(jnp.int32))

    def key_tile(j):
        full = jnp.where(j >= qa + 2, j - 2 - qa, j - 2)
        return jnp.where(j == 0, qa, jnp.where(j == 1, qb, full))

    def scores(j, masked):
        start = pl.multiple_of(key_tile(j) * t, t)
        s = _dot(k_ref[pl.ds(start, t), :], qt_ref[owner(j)])
        if masked:
            kpos = lax.broadcasted_iota(jnp.int32, (t, t), 0)
            qpos = lax.broadcasted_iota(jnp.int32, (t, t), 1)
            s = jnp.where(kpos <= qpos, s, neg)
        return s

    def fused(j, slot, masked=False):
        own = owner(j)
        ref = lag_ref[own]
        a_refs[slot][...] = jnp.exp2(rl_ref[own] - ref)
        rl_ref[own] = ref
        s = scores(j, masked)
        rb = jnp.broadcast_to(ref, (SUBLANES, t))
        p_ref = p_refs[slot]
        mx = [None, None]
        for r in range(t // rows):
            lo = s[r * rows:r * rows + SUBLANES, :]
            hi = s[r * rows + SUBLANES:(r + 1) * rows, :]
            mx[0] = lo if mx[0] is None else jnp.maximum(mx[0], lo)
            mx[1] = hi if mx[1] is None else jnp.maximum(mx[1], hi)
            p_ref[r * rows:(r + 1) * rows, :] = jnp.concatenate(
                [jnp.exp2(lo - rb), jnp.exp2(hi - rb)], axis=0).astype(BF16)
        m_tile = jnp.max(jnp.maximum(mx[0], mx[1]), axis=0, keepdims=True)
        m_old = m_ref[own]
        lag_ref[own] = m_old
        m_ref[own] = jnp.maximum(m_old, m_tile)
        jump_ref[...] = jnp.maximum(jump_ref[...], m_tile - ref)
        if masked:
            low_ref[...] = jnp.minimum(low_ref[...], m_tile - ref)

    def pv(j, slot):
        start = pl.multiple_of(key_tile(j) * t, t)
        own = owner(j)
        acc_ref[own] = acc_ref[own] * a_refs[slot][...] + _dot(vt_ref[:, pl.ds(start, t)],
                                                                p_refs[slot][...])

    def stage(k, slot):
        fused(k + 2, (slot + 2) % ns)
        pv(k, slot)

    fused(0, 0, masked=True)
    fused(1, 1, masked=True)
    nstage = nt - 1

    def body(jq, _):
        for u in range(ATTN_UNROLL):
            stage(ATTN_UNROLL * jq + u, u % ns)
        return 0

    lax.fori_loop(0, nstage // ATTN_UNROLL, body, 0)
    for k in range(nstage - nstage % ATTN_UNROLL, nstage):
        stage(k, k % ns)
    pv(last - 1, (last - 1) % ns)
    pv(last, last % ns)

    def finish():
        for own, o_ref in ((0, oa_ref), (1, ob_ref)):
            acc = acc_ref[own]
            o_t = acc[:V_HEAD, :] * (1.0 / acc[V_HEAD:V_HEAD + 1, :])
            o_ref[...] = o_t.T.astype(o_ref.dtype)

    finish()

    @pl.when((jnp.max(jump_ref[...]) > ATTN_MAX_JUMP) | (jnp.min(low_ref[...]) < -ATTN_MAX_JUMP))
    def _():
        m_ref[...] = jnp.full(m_ref.shape, neg, F32)
        acc_ref[...] = jnp.zeros(acc_ref.shape, F32)

        def exact(j, masked):
            own = owner(j)
            s_ref[...] = scores(j, masked)
            mx = s_ref[0:SUBLANES, :]
            for r in range(1, t // SUBLANES):
                mx = jnp.maximum(mx, s_ref[r * SUBLANES:(r + 1) * SUBLANES, :])
            m_old = m_ref[own]
            m_new = jnp.maximum(m_old, jnp.max(mx, axis=0, keepdims=True))
            m_ref[own] = m_new
            a_refs[0][...] = jnp.exp2(m_old - m_new)
            mb = jnp.broadcast_to(m_new, (rows, t))
            for r in range(t // rows):
                p_refs[0][r * rows:(r + 1) * rows, :] = jnp.exp2(
                    s_ref[r * rows:(r + 1) * rows, :] - mb).astype(BF16)
            pv(j, 0)

        exact(0, True)
        exact(1, True)
        lax.fori_loop(2, last + 1, lambda j, c: (exact(j, False), c)[1], 0)
        finish()


def _attention(q, k, vt, *, t=512):
    s = q.shape[0]
    nt = s // t
    assert nt % 2 == 0
    half = nt // 2
    out = jax.ShapeDtypeStruct((s // 2, MLA_HEADS * V_HEAD), BF16)
    vec = pltpu.VMEM((2, 1, t), F32)
    return pl.pallas_call(
        functools.partial(_attn_body, t=t, nt=nt),
        grid=(MLA_HEADS, half),
        in_specs=[pl.BlockSpec((t, QK_PAD), lambda hd, i: (i, hd)),
                  pl.BlockSpec((t, QK_PAD), lambda hd, i: (nt - 1 - i, hd)),
                  pl.BlockSpec((s, QK_PAD), lambda hd, i: (0, hd)),
                  pl.BlockSpec((V_ROWS, s), lambda hd, i: (hd, 0))],
        out_specs=[pl.BlockSpec((t, V_HEAD), lambda hd, i: (i, hd)),
                   pl.BlockSpec((t, V_HEAD), lambda hd, i: (half - 1 - i, hd))],
        out_shape=[out, out],
        scratch_shapes=[
            pltpu.VMEM((2, QK_PAD, t), BF16),
            vec,
            pltpu.VMEM((2, V_ROWS, t), F32),
            vec,
            vec,
            pltpu.VMEM((1, t), F32),
            pltpu.VMEM((1, t), F32),
            pltpu.VMEM((t, t), F32),
        ] + [pltpu.VMEM((t, t), BF16)] * ATTN_SLOTS
          + [pltpu.VMEM((1, t), F32)] * ATTN_SLOTS,
        compiler_params=_params("parallel", "parallel"),
        name="mla_attn",
    )(q, q, k, vt)


def _out_proj_body(h_ref, oa_ref, ob_ref, g_ref, wo_ref, wu_ref, wd_ref, o_ref, *, half):
    o = jnp.where(pl.program_id(0) < half, oa_ref[...], ob_ref[...])
    h = h_ref[...] + _rms(_dot(o, wo_ref[...]), g_ref[1:2, :])
    o_ref[...] = _mlp_tail(h, g_ref, wu_ref, wd_ref)


def _out_proj_mlp(h, oa, ob, g, w_o, w_up, w_down, layer, *, tm=512):
    s = h.shape[0]
    half = oa.shape[0] // tm
    return pl.pallas_call(
        functools.partial(_out_proj_body, half=half),
        grid=(s // tm,),
        in_specs=[_row_spec(tm, D_MODEL),
                  pl.BlockSpec((tm, D_MODEL), lambda i: (jnp.minimum(i, half - 1), 0)),
                  pl.BlockSpec((tm, D_MODEL), lambda i: (jnp.maximum(i - half, 0), 0)),
                  _const_spec(g.shape), _const_spec(w_o.shape)]
        + _mlp_weight_specs(w_up, w_down, layer),
        out_specs=_row_spec(tm, D_MODEL),
        out_shape=jax.ShapeDtypeStruct(h.shape, F32),
        compiler_params=_params("parallel"),
        name="mla_out_mlp",
    )(h, oa, ob, g, w_o, w_up, w_down)


def _s5_disc_body(lr_ref, li_ref, ldt_ref, br_ref, bi_ref, abr_ref, abi_ref, bbr_ref, bbi_ref):
    dt = jnp.exp(ldt_ref[...])
    lr = lr_ref[...]
    li = li_ref[...]
    mag = jnp.exp(lr * dt)
    ab_re = mag * jnp.cos(li * dt)
    ab_im = mag * jnp.sin(li * dt)
    den = lr * lr + li * li
    f_re = ((ab_re - 1.0) * lr + ab_im * li) / den
    f_im = (ab_im * lr - (ab_re - 1.0) * li) / den
    abr_ref[...] = ab_re
    abi_ref[...] = ab_im
    br = br_ref[...]
    bi = bi_ref[...]
    bbr_ref[...] = f_re[:, None, :] * br - f_im[:, None, :] * bi
    bbi_ref[...] = f_re[:, None, :] * bi + f_im[:, None, :] * br


def _s5_discretise(lam_re, lam_im, log_dt, b_re_t, b_im_t):
    g, p = lam_re.shape
    c = b_re_t.shape[1]
    return pl.pallas_call(
        _s5_disc_body,
        out_shape=[jax.ShapeDtypeStruct((g, p), F32), jax.ShapeDtypeStruct((g, p), F32),
                   jax.ShapeDtypeStruct((g, c, p), F32), jax.ShapeDtypeStruct((g, c, p), F32)],
        name="s5_disc",
    )(lam_re, lam_im, log_dt.reshape(g, 1), b_re_t, b_im_t)


def _s5_body(h_ref, g_ref, perm_ref, ar_ref, ai_ref, bblk_ref, cblk_ref, dskip_ref,
             wa_ref, wb_ref, o_ref, xr0_ref, xi0_ref, xr1_ref, xi1_ref, xc0_ref, xc1_ref,
             yp_ref, y_ref, pr_ref, pi_ref, cr_ref, ci_ref, *, tm):
    seg = tm // SUBLANES
    i0 = pl.program_id(0)
    nst = S5_BLOCK_STATE
    pack = 2 * SUBLANES

    @pl.when(i0 == 0)
    def _():
        cr_ref[...] = jnp.zeros(cr_ref.shape, F32)
        ci_ref[...] = jnp.zeros(ci_ref.shape, F32)
        ar = ar_ref[...]
        ai = ai_ref[...]
        bcast = lambda v: jnp.broadcast_to(v, (SUBLANES, v.shape[1]))
        pr_ref[0:SUBLANES, :] = bcast(ar)
        pi_ref[0:SUBLANES, :] = bcast(ai)

        def pw(j, carry):
            r, im = carry
            r, im = r * ar - im * ai, r * ai + im * ar
            rows = pl.ds(pl.multiple_of(j * SUBLANES, SUBLANES), SUBLANES)
            pr_ref[rows, :] = bcast(r)
            pi_ref[rows, :] = bcast(im)
            return r, im

        lax.fori_loop(1, seg, pw, (ar, ai))

    h = h_ref[...]
    z = _rms(h, g_ref[0:1, :])
    zp = _dot(perm_ref[...], z.astype(BF16)).astype(BF16)

    for cb in range(S5_BLOCKS):
        st = slice(cb * nst, (cb + 1) * nst)
        xr_ref, xi_ref, xc_ref = ((xr0_ref, xi0_ref, xc0_ref), (xr1_ref, xi1_ref, xc1_ref))[cb % 2]
        bu = _dot(zp[:, cb * LANES:(cb + 1) * LANES], bblk_ref[cb])
        ar = jnp.broadcast_to(ar_ref[:, st], (SUBLANES, nst))
        ai = jnp.broadcast_to(ai_ref[:, st], (SUBLANES, nst))
        xr = jnp.zeros((SUBLANES, nst), F32)
        xi = xr
        for i in range(seg):
            rows = slice(i * SUBLANES, (i + 1) * SUBLANES)
            xr, xi = (ar * xr - ai * xi + bu[rows, :nst], ar * xi + ai * xr + bu[rows, nst:])
            xr_ref[rows, :] = xr
            xi_ref[rows, :] = xi
        alr = pr_ref[tm - 1:tm, st]
        ali = pi_ref[tm - 1:tm, st]
        c_r = cr_ref[:, st]
        c_i = ci_ref[:, st]
        rows_r, rows_i = [], []
        for sgm in range(SUBLANES):
            rows_r.append(c_r)
            rows_i.append(c_i)
            e_r = xr[sgm:sgm + 1, :]
            e_i = xi[sgm:sgm + 1, :]
            c_r, c_i = alr * c_r - ali * c_i + e_r, alr * c_i + ali * c_r + e_i
        cr_ref[:, st] = c_r
        ci_ref[:, st] = c_i
        cin_r = jnp.concatenate(rows_r, axis=0)
        cin_i = jnp.concatenate(rows_i, axis=0)
        for i2 in range(seg // 2):
            fr, fi = [], []
            for i in (2 * i2, 2 * i2 + 1):
                rows = slice(i * SUBLANES, (i + 1) * SUBLANES)
                p_r = pr_ref[rows, st]
                p_i = pi_ref[rows, st]
                fr.append(xr_ref[rows, :] + (p_r * cin_r - p_i * cin_i))
                fi.append(xi_ref[rows, :] + (p_r * cin_i + p_i * cin_r))
            rows2 = slice(i2 * pack, (i2 + 1) * pack)
            xc_ref[rows2, 0:nst] = jnp.concatenate(fr, axis=0).astype(BF16)
            xc_ref[rows2, nst:2 * nst] = jnp.concatenate(fi, axis=0).astype(BF16)
        yp_ref[cb] = _dot(xc_ref[...], cblk_ref[cb])

    for sgm in range(SUBLANES):
        for j in range(seg // SUBLANES):
            dst = slice(sgm * seg + j * SUBLANES, sgm * seg + (j + 1) * SUBLANES)
            src = pl.ds(j * SUBLANES * SUBLANES + sgm, SUBLANES, stride=SUBLANES)
            for cb in range(S5_BLOCKS):
                y_ref[dst, cb * LANES:(cb + 1) * LANES] = yp_ref[cb, src, :]
    y = _gelu(y_ref[...] + dskip_ref[...] * z).astype(BF16)
    out = _dot(y, wa_ref[...]) * jax.nn.sigmoid(_dot(y, wb_ref[...]))
    o_ref[...] = h + _rms(out, g_ref[1:2, :])


def _s5_permutation(tm):
    seg = tm // SUBLANES
    r = jnp.arange(tm)
    src = (r % SUBLANES) * seg + r // SUBLANES
    return (src[:, None] == jnp.arange(tm)[None, :]).astype(BF16)


def _s5(h, g, ab_re, ab_im, bblk, cblk, dskip, w_a, w_b, *, tm=512):
    s = h.shape[0]
    seg = tm // SUBLANES
    nstate = S5_GROUPS * S5_STATE
    perm = _s5_permutation(tm)
    consts = (g, perm, ab_re, ab_im, bblk, cblk, dskip, w_a, w_b)
    state = pltpu.VMEM((tm, S5_BLOCK_STATE), F32)
    return pl.pallas_call(
        functools.partial(_s5_body, tm=tm),
        grid=(s // tm,),
        in_specs=[_row_spec(tm, D_MODEL)] + [_const_spec(a.shape) for a in consts],
        out_specs=_row_spec(tm, D_MODEL),
        out_shape=jax.ShapeDtypeStruct(h.shape, F32),
        scratch_shapes=[
            state, state, state, state,
            pltpu.VMEM((tm, 2 * S5_BLOCK_STATE), BF16),
            pltpu.VMEM((tm, 2 * S5_BLOCK_STATE), BF16),
            pltpu.VMEM((S5_BLOCKS, tm, LANES), F32),
            pltpu.VMEM((tm, D_MODEL), F32),
            pltpu.VMEM((tm, nstate), F32),
            pltpu.VMEM((tm, nstate), F32),
            pltpu.VMEM((1, nstate), F32),
            pltpu.VMEM((1, nstate), F32),
        ],
        compiler_params=_params("arbitrary"),
        name="s5",
    )(h, *consts)


def _s5_block_matrices(bb_re, bb_im, c_re, c_im):
    nb, gb, c, p = S5_BLOCKS, S5_BLOCK_GROUPS, S5_GROUP_CH, S5_STATE
    eye = jnp.eye(gb, dtype=F32)

    def in_map(bb):
        t = bb.reshape(nb, gb, c, p)
        return jnp.einsum('bgcp,gk->bgckp', t, eye).reshape(nb, gb * c, gb * p)

    def out_map(cc):
        t = cc.reshape(nb, gb, c, p)
        return jnp.einsum('bgcp,gk->bgpkc', t, eye).reshape(nb, gb * p, gb * c)

    bblk = jnp.concatenate([in_map(bb_re), in_map(bb_im)], axis=2).astype(BF16)
    cblk = jnp.concatenate([out_map(c_re), out_map(-c_im)], axis=1).astype(BF16)
    return bblk, cblk


def _mla_weights(w_uq, w_dkv):
    wq = w_uq.reshape(Q_LORA, MLA_HEADS, QK_NOPE + QK_ROPE)
    w_n = wq[:, :, :QK_NOPE].reshape(Q_LORA, MLA_HEADS * QK_NOPE)
    w_r = jnp.pad(wq[:, :, QK_NOPE:], ((0, 0), (0, 0), (0, LANES - QK_ROPE)))
    w_r = w_r.reshape(Q_LORA, MLA_HEADS * LANES)
    w_dkv_p = jnp.pad(w_dkv, ((0, 0), (0, LANES - QK_ROPE)))
    return w_n.astype(BF16), w_r.astype(BF16), w_dkv_p.astype(BF16)


def kernel(x, positions, norm_g, w_up, w_down, a_w_in, a_b_in, a_g_v, a_b_v, a_w_s, a_b_s, a_w_out, b_w_grp, b_scale, c_w_dq, c_g_q, c_w_uq, c_w_dkv, c_g_kv, c_w_uk, c_w_uv, c_w_o, d_lam_re, d_lam_im, d_log_dt, d_b_re, d_b_im, d_c_re, d_c_im, d_skip, d_w_glu_a, d_w_glu_b):
    bsz, s, d = x.shape
    assert bsz == 1 and d == D_MODEL
    h = x.reshape(s, d)
    row = lambda a: a.reshape(1, -1)
    w_up_b = w_up.astype(BF16)
    w_down_b = w_down.astype(BF16)

    b_s_full = jnp.repeat(a_b_s[0].T, D_MODEL // GMLP_HEADS, axis=1)
    h = _gmlp(h, norm_g[0], a_w_in[0].astype(BF16), row(a_b_in[0]), row(a_g_v[0]), row(a_b_v[0]),
              a_w_s[0], b_s_full, a_w_out[0].astype(BF16))
    h = _mlp(h, norm_g[0], w_up_b, w_down_b, 0)

    h = _pool_mlp(h, norm_g[1], b_w_grp[0].astype(BF16), row(b_scale[0]), w_up_b, w_down_b, 1)

    w_uq_n, w_uq_r, w_dkv_p = _mla_weights(c_w_uq[0], c_w_dkv[0])
    inv_freq = ROPE_THETA ** (-jnp.arange(0, QK_ROPE, 2, dtype=F32) / QK_ROPE)
    invf = jnp.concatenate([inv_freq, inv_freq, jnp.zeros((LANES - QK_ROPE,), F32)]).reshape(1, LANES)
    q, k, v = _mla_proj(h, positions.reshape(s, 1), norm_g[2], c_w_dq[0].astype(BF16), row(c_g_q[0]),
                        w_uq_n, w_uq_r, w_dkv_p, row(c_g_kv[0]), c_w_uk[0].astype(BF16),
                        c_w_uv[0].T.astype(BF16), invf)
    oa, ob = _attention(q, k, v)
    h = _out_proj_mlp(h, oa, ob, norm_g[2], c_w_o[0].astype(BF16), w_up_b, w_down_b, 2)

    tr = lambda a: jnp.swapaxes(a, 1, 2)
    ab_re, ab_im, bb_re, bb_im = _s5_discretise(d_lam_re[0], d_lam_im[0], d_log_dt[0],
                                                tr(d_b_re[0]), tr(d_b_im[0]))
    bblk, cblk = _s5_block_matrices(bb_re, bb_im, d_c_re[0], d_c_im[0])
    h = _s5(h, norm_g[3], row(ab_re), row(ab_im), bblk, cblk, row(d_skip[0]),
            d_w_glu_a[0].astype(BF16), d_w_glu_b[0].astype(BF16))
    h = _mlp(h, norm_g[3], w_up_b, w_down_b, 3)
    return h.reshape(bsz, s, d)
```

```python
import functools
import math

import jax
import jax.numpy as jnp
from jax import lax
from jax.experimental import pallas as pl
from jax.experimental.pallas import tpu as pltpu

F32 = jnp.float32
BF16 = jnp.bfloat16

D_MODEL = 1024
D_FF = 4 * D_MODEL
RMS_EPS = 1e-6
LANES = 128
SUBLANES = 8
VMEM_LIMIT = 56 * 1024 * 1024

CHUNK = 128
GMLP_HEADS = 8
POOL_WINDOWS = (2, 4, 8, 16)
POOL_GROUP = D_MODEL // len(POOL_WINDOWS)
POOL_HALO = 32
assert POOL_WINDOWS == tuple(2 << k for k in range(len(POOL_WINDOWS)))
assert POOL_HALO == SUBLANES * len(POOL_WINDOWS)
MLA_HEADS = 8
Q_LORA = 384
KV_LORA = 256
QK_NOPE = 128
QK_ROPE = 64
V_HEAD = 128
ROPE_THETA = 10000.0
QK_PAD = 256
S5_GROUP_CH = 16
S5_GROUPS = D_MODEL // S5_GROUP_CH
S5_STATE = 64
S5_BLOCK_GROUPS = LANES // S5_GROUP_CH
S5_BLOCKS = S5_GROUPS // S5_BLOCK_GROUPS
S5_BLOCK_STATE = S5_BLOCK_GROUPS * S5_STATE


def _rms(x, g):
    return x * lax.rsqrt(jnp.mean(x * x, axis=-1, keepdims=True) + RMS_EPS) * g


def _gelu(x):
    c = math.sqrt(2.0 / math.pi)
    return 0.5 * x * (1.0 + jnp.tanh(c * (x + 0.044715 * (x * x * x))))


def _dot(a, b):
    return jnp.dot(a, b, preferred_element_type=F32)


def _const_spec(shape):
    nd = len(shape)
    return pl.BlockSpec(shape, lambda *_: (0,) * nd, pipeline_mode=pl.Buffered(1))


def _row_spec(tm, width):
    return pl.BlockSpec((tm, width), lambda i: (i, 0))


def _params(*sem, flags=None):
    return pltpu.CompilerParams(dimension_semantics=sem, vmem_limit_bytes=VMEM_LIMIT, flags=flags)


MLP_TF = 1024


def _mlp_tail(h, g_ref, wu_ref, wd_ref):
    z = _rms(h, g_ref[2:3, :]).astype(BF16)
    acc = jnp.zeros(h.shape, F32)
    for c in range(D_FF // MLP_TF):
        a = _dot(z, wu_ref[:, c * MLP_TF:(c + 1) * MLP_TF])
        a = jnp.square(jnp.maximum(a, 0.0)).astype(BF16)
        acc = acc + _dot(a, wd_ref[c * MLP_TF:(c + 1) * MLP_TF, :])
    return h + _rms(acc, g_ref[3:4, :])


def _mlp_weight_specs(w_up, w_down, layer):
    return [pl.BlockSpec((None,) + w.shape[1:], lambda *_: (layer, 0, 0), pipeline_mode=pl.Buffered(1))
            for w in (w_up, w_down)]


def _mlp_body(h_ref, g_ref, wu_ref, wd_ref, o_ref):
    o_ref[...] = _mlp_tail(h_ref[...], g_ref, wu_ref, wd_ref)


def _mlp(h, g, w_up, w_down, layer, *, tm=512):
    s = h.shape[0]
    return pl.pallas_call(
        _mlp_body,
        grid=(s // tm,),
        in_specs=[_row_spec(tm, D_MODEL), _const_spec(g.shape)] + _mlp_weight_specs(w_up, w_down, layer),
        out_specs=_row_spec(tm, D_MODEL),
        out_shape=jax.ShapeDtypeStruct(h.shape, F32),
        compiler_params=_params("parallel"),
        name="mlp",
    )(h, g, w_up, w_down)


def _gmlp_body(h_ref, g_ref, win_ref, bin_ref, gv_ref, bv_ref, ws_ref, bs_ref, wout_ref, o_ref,
               sv_ref, *, tm):
    h = h_ref[...]
    z = _rms(h, g_ref[0:1, :]).astype(BF16)
    uv = _gelu(_dot(z, win_ref[...]) + bin_ref[...])
    u = uv[:, :D_MODEL]
    v = uv[:, D_MODEL:]
    mu = jnp.mean(v, axis=-1, keepdims=True)
    vc = v - mu
    v = vc * lax.rsqrt(jnp.mean(vc * vc, axis=-1, keepdims=True) + RMS_EPS)
    v = (v * gv_ref[...] + bv_ref[...]).astype(BF16)
    row = lax.broadcasted_iota(jnp.int32, (CHUNK, CHUNK), 0)
    col = lax.broadcasted_iota(jnp.int32, (CHUNK, CHUNK), 1)
    nchunk = tm // CHUNK
    for hd in range(GMLP_HEADS):
        ws = jnp.where(col <= row, ws_ref[hd], 0.0).astype(BF16)
        cols = slice(hd * LANES, (hd + 1) * LANES)
        rhs = jnp.concatenate([v[c * CHUNK:(c + 1) * CHUNK, cols] for c in range(nchunk)], axis=1)
        sv = _dot(ws, rhs)
        for c in range(nchunk):
            sv_ref[c * CHUNK:(c + 1) * CHUNK, cols] = (
                sv[:, c * LANES:(c + 1) * LANES] + bs_ref[:, cols])
    y = (u * sv_ref[...]).astype(BF16)
    o_ref[...] = h + _rms(_dot(y, wout_ref[...]), g_ref[1:2, :])


def _gmlp(h, g, w_in, b_in, g_v, b_v, w_s, b_s_full, w_out, *, tm=512):
    s = h.shape[0]
    return pl.pallas_call(
        functools.partial(_gmlp_body, tm=tm),
        grid=(s // tm,),
        in_specs=[_row_spec(tm, D_MODEL), _const_spec(g.shape), _const_spec(w_in.shape),
                  _const_spec(b_in.shape), _const_spec(g_v.shape), _const_spec(b_v.shape),
                  _const_spec(w_s.shape), _const_spec(b_s_full.shape), _const_spec(w_out.shape)],
        out_specs=_row_spec(tm, D_MODEL),
        out_shape=jax.ShapeDtypeStruct(h.shape, F32),
        scratch_shapes=[pltpu.VMEM((tm, D_MODEL), F32)],
        compiler_params=_params("parallel"),
        name="gmlp",
    )(h, g, w_in, b_in, g_v, b_v, w_s, b_s_full, w_out)


def _pool_body(h_ref, halo_ref, g_ref, w_ref, scale_ref, wu_ref, wd_ref, o_ref, zz_ref, ws_ref,
               *, tm):
    i = pl.program_id(0)
    h = h_ref[...]
    z = _rms(h, g_ref[0:1, :])
    zh = _rms(halo_ref[...], g_ref[0:1, :])
    zz_ref[0:POOL_HALO, :] = jnp.where(i > 0, zh, 0.0)
    zz_ref[POOL_HALO:, :] = z
    nrows = tm + POOL_HALO
    for k in range(1, len(POOL_WINDOWS) + 1):
        d = 1 << (k - 1)
        lo = SUBLANES * k
        cols = slice((k - 1) * POOL_GROUP, D_MODEL)
        src = zz_ref if k == 1 else ws_ref
        ws_ref[lo:, cols] = src[lo:, cols] + src[lo - d:nrows - d, cols]
    pos = i * tm + lax.broadcasted_iota(jnp.int32, (tm, 1), 0)
    outs = []
    for gi, w in enumerate(POOL_WINDOWS):
        cols = slice(gi * POOL_GROUP, (gi + 1) * POOL_GROUP)
        xg = z[:, cols]
        win = ws_ref[POOL_HALO:, cols]
        count = jnp.minimum(pos + 1, w).astype(F32)
        diff = (win / count - xg).astype(BF16)
        outs.append(_dot(diff, w_ref[gi]))
    out = jnp.concatenate(outs, axis=-1) * scale_ref[...]
    o_ref[...] = _mlp_tail(h + _rms(out, g_ref[1:2, :]), g_ref, wu_ref, wd_ref)


def _pool_mlp(h, g, w_grp, scale, w_up, w_down, layer, *, tm=512):
    s = h.shape[0]
    ratio = tm // POOL_HALO
    return pl.pallas_call(
        functools.partial(_pool_body, tm=tm),
        grid=(s // tm,),
        in_specs=[_row_spec(tm, D_MODEL),
                  pl.BlockSpec((POOL_HALO, D_MODEL), lambda i: (jnp.maximum(i * ratio - 1, 0), 0)),
                  _const_spec(g.shape), _const_spec(w_grp.shape), _const_spec(scale.shape)]
        + _mlp_weight_specs(w_up, w_down, layer),
        out_specs=_row_spec(tm, D_MODEL),
        out_shape=jax.ShapeDtypeStruct(h.shape, F32),
        scratch_shapes=[pltpu.VMEM((tm + POOL_HALO, D_MODEL), F32)] * 2,
        compiler_params=_params("parallel"),
        name="pool_mlp",
    )(h, h, g, w_grp, scale, w_up, w_down)


def _rope_tile(x, cos, sin_lo, sin_hi):
    n = x.shape[1] // LANES
    outs = []
    for t in range(n):
        xt = x[:, t * LANES:(t + 1) * LANES]
        up = pltpu.roll(xt, 32, axis=1)
        down = pltpu.roll(xt, LANES - 32, axis=1)
        outs.append(xt * cos + down * sin_lo + up * sin_hi)
    return outs


def _mla_proj_body(h_ref, pos_ref, g_ref, wdq_ref, gq_ref, wuqn_ref, wuqr_ref, wdkv_ref, gkv_ref,
                   wuk_ref, wuv_ref, invf_ref, q_ref, k_ref, v_ref, *, scale):
    h = h_ref[...]
    z = _rms(h, g_ref[0:1, :]).astype(BF16)
    ang = pos_ref[...].astype(F32) * invf_ref[...]
    lane = lax.broadcasted_iota(jnp.int32, ang.shape, 1)
    cos = jnp.where(lane < QK_ROPE, jnp.cos(ang), 0.0)
    sin = jnp.sin(ang)
    sin_lo = jnp.where(lane < QK_ROPE // 2, -sin, 0.0)
    sin_hi = jnp.where((lane >= QK_ROPE // 2) & (lane < QK_ROPE), sin, 0.0)

    ql = _rms(_dot(z, wdq_ref[...]), gq_ref[...]).astype(BF16)
    qn = _dot(ql, wuqn_ref[...]) * scale
    qr = _rope_tile(_dot(ql, wuqr_ref[...]) * scale, cos, sin_lo, sin_hi)
    ckv = _dot(z, wdkv_ref[...])
    c = _rms(ckv[:, :KV_LORA], gkv_ref[...]).astype(BF16)
    kr = _rope_tile(ckv[:, KV_LORA:], cos, sin_lo, sin_hi)[0].astype(BF16)
    kn = _dot(c, wuk_ref[...])
    for hd in range(MLA_HEADS):
        cols = slice(hd * LANES, (hd + 1) * LANES)
        q_ref[:, hd * QK_PAD:hd * QK_PAD + LANES] = qn[:, cols].astype(BF16)
        q_ref[:, hd * QK_PAD + LANES:(hd + 1) * QK_PAD] = qr[hd].astype(BF16)
        k_ref[:, hd * QK_PAD:hd * QK_PAD + LANES] = kn[:, cols].astype(BF16)
        k_ref[:, hd * QK_PAD + LANES:(hd + 1) * QK_PAD] = kr
    v_ref[...] = lax.dot_general(wuv_ref[...], c, (((1,), (1,)), ((), ())),
                                 preferred_element_type=F32).astype(BF16)


def _mla_proj(h, pos, g, w_dq, g_q, w_uq_n, w_uq_r, w_dkv, g_kv, w_uk, w_uv, invf, *, tm=512):
    s = h.shape[0]
    scale = (QK_NOPE + QK_ROPE) ** -0.5 * math.log2(math.e)
    consts = (g, w_dq, g_q, w_uq_n, w_uq_r, w_dkv, g_kv, w_uk, w_uv, invf)
    return pl.pallas_call(
        functools.partial(_mla_proj_body, scale=scale),
        grid=(s // tm,),
        in_specs=[_row_spec(tm, D_MODEL), _row_spec(tm, 1)] + [_const_spec(a.shape) for a in consts],
        out_specs=[_row_spec(tm, MLA_HEADS * QK_PAD), _row_spec(tm, MLA_HEADS * QK_PAD),
                   pl.BlockSpec((MLA_HEADS * V_HEAD, tm), lambda i: (0, i))],
        out_shape=[jax.ShapeDtypeStruct((s, MLA_HEADS * QK_PAD), BF16),
                   jax.ShapeDtypeStruct((s, MLA_HEADS * QK_PAD), BF16),
                   jax.ShapeDtypeStruct((MLA_HEADS * V_HEAD, s), BF16)],
        compiler_params=_params("parallel"),
        name="mla_proj",
    )(h, pos, *consts)


ATTN_SLOTS = 4
ATTN_UNROLL = 16
ATTN_MAX_JUMP = 64.0


def _attn_body(qa_ref, qb_ref, k_ref, vt_ref, oa_ref, ob_ref, qt_ref, m_ref, acc_ref, l_ref, lag_ref,
               rl_ref, jump_ref, low_ref, s_ref, *slot_refs, t, nt):
    qa = pl.program_id(1)
    qb = nt - 1 - qa
    ns = ATTN_SLOTS
    p_refs, a_refs = slot_refs[:ns], slot_refs[ns:]
    neg = -1e30
    rows = 2 * SUBLANES
    last = nt

    qt_ref[0] = qa_ref[...].astype(F32).T.astype(BF16)
    qt_ref[1] = qb_ref[...].astype(F32).T.astype(BF16)
    for state in (m_ref, acc_ref, l_ref, lag_ref, rl_ref, jump_ref, low_ref):
        state[...] = jnp.zeros(state.shape, F32)

    def owner(j):
        return jnp.where(j < 2, j, (j >= qa + 2).astype(jnp.int32))

    def key_tile(j):
        full = jnp.where(j >= qa + 2, j - 2 - qa, j - 2)
        return jnp.where(j == 0, qa, jnp.where(j == 1, qb, full))

    def scores(j, masked):
        start = pl.multiple_of(key_tile(j) * t, t)
        s = _dot(k_ref[pl.ds(start, t), :], qt_ref[owner(j)])
        if masked:
            kpos = lax.broadcasted_iota(jnp.int32, (t, t), 0)
            qpos = lax.broadcasted_iota(jnp.int32, (t, t), 1)
            s = jnp.where(kpos <= qpos, s, neg)
        return s

    def fused(j, slot, masked=False):
        own = owner(j)
        ref = lag_ref[own]
        alpha = jnp.exp2(rl_ref[own] - ref)
        a_refs[slot][...] = alpha
        rl_ref[own] = ref
        s = scores(j, masked)
        rb = jnp.broadcast_to(ref, (SUBLANES, t))
        p_ref = p_refs[slot]
        mx = [None, None]
        ps = [None, None]
        for r in range(t // rows):
            lo = s[r * rows:r * rows + SUBLANES, :]
            hi = s[r * rows + SUBLANES:(r + 1) * rows, :]
            mx[0] = lo if mx[0] is None else jnp.maximum(mx[0], lo)
            mx[1] = hi if mx[1] is None else jnp.maximum(mx[1], hi)
            plo = jnp.exp2(lo - rb)
            phi = jnp.exp2(hi - rb)
            ps[0] = plo if ps[0] is None else ps[0] + plo
            ps[1] = phi if ps[1] is None else ps[1] + phi
            p_ref[r * rows:(r + 1) * rows, :] = jnp.concatenate([plo, phi], axis=0).astype(BF16)
        l_ref[own] = l_ref[own] * alpha + jnp.sum(ps[0] + ps[1], axis=0, keepdims=True)
        m_tile = jnp.max(jnp.maximum(mx[0], mx[1]), axis=0, keepdims=True)
        m_old = m_ref[own]
        lag_ref[own] = m_old
        m_ref[own] = jnp.maximum(m_old, m_tile)
        jump_ref[...] = jnp.maximum(jump_ref[...], m_tile - ref)
        if masked:
            low_ref[...] = jnp.minimum(low_ref[...], m_tile - ref)

    def pv(j, slot):
        start = pl.multiple_of(key_tile(j) * t, t)
        own = owner(j)
        acc_ref[own] = acc_ref[own] * a_refs[slot][...] + _dot(vt_ref[:, pl.ds(start, t)],
                                                                p_refs[slot][...])

    def stage(k, slot):
        fused(k + 2, (slot + 2) % ns)
        pv(k, slot)

    fused(0, 0, masked=True)
    fused(1, 1, masked=True)
    nstage = nt - 1

    def body(jq, _):
        for u in range(ATTN_UNROLL):
            stage(ATTN_UNROLL * jq + u, u % ns)
        return 0

    lax.fori_loop(0, nstage // ATTN_UNROLL, body, 0)
    for k in range(nstage - nstage % ATTN_UNROLL, nstage):
        stage(k, k % ns)
    pv(last - 1, (last - 1) % ns)
    pv(last, last % ns)

    def finish():
        for own, o_ref in ((0, oa_ref), (1, ob_ref)):
            o_t = acc_ref[own] * (1.0 / l_ref[own])
            o_ref[...] = o_t.T.astype(o_ref.dtype)

    finish()

    @pl.when((jnp.max(jump_ref[...]) > ATTN_MAX_JUMP) | (jnp.min(low_ref[...]) < -ATTN_MAX_JUMP))
    def _():
        m_ref[...] = jnp.full(m_ref.shape, neg, F32)
        acc_ref[...] = jnp.zeros(acc_ref.shape, F32)
        l_ref[...] = jnp.zeros(l_ref.shape, F32)

        def exact(j, masked):
            own = owner(j)
            s_ref[...] = scores(j, masked)
            mx = s_ref[0:SUBLANES, :]
            for r in range(1, t // SUBLANES):
                mx = jnp.maximum(mx, s_ref[r * SUBLANES:(r + 1) * SUBLANES, :])
            m_old = m_ref[own]
            m_new = jnp.maximum(m_old, jnp.max(mx, axis=0, keepdims=True))
            m_ref[own] = m_new
            alpha = jnp.exp2(m_old - m_new)
            a_refs[0][...] = alpha
            mb = jnp.broadcast_to(m_new, (rows, t))
            psum = jnp.zeros((rows, t), F32)
            for r in range(t // rows):
                p = jnp.exp2(s_ref[r * rows:(r + 1) * rows, :] - mb)
                psum = psum + p
                p_refs[0][r * rows:(r + 1) * rows, :] = p.astype(BF16)
            l_ref[own] = l_ref[own] * alpha + jnp.sum(psum, axis=0, keepdims=True)
            pv(j, 0)

        exact(0, True)
        exact(1, True)
        lax.fori_loop(2, last + 1, lambda j, c: (exact(j, False), c)[1], 0)
        finish()


def _attention(q, k, vt, *, t=512):
    s = q.shape[0]
    nt = s // t
    assert nt % 2 == 0
    half = nt // 2
    out = jax.ShapeDtypeStruct((s // 2, MLA_HEADS * V_HEAD), BF16)
    vec = pltpu.VMEM((2, 1, t), F32)
    return pl.pallas_call(
        functools.partial(_attn_body, t=t, nt=nt),
        grid=(MLA_HEADS, half),
        in_specs=[pl.BlockSpec((t, QK_PAD), lambda hd, i: (i, hd)),
                  pl.BlockSpec((t, QK_PAD), lambda hd, i: (nt - 1 - i, hd)),
                  pl.BlockSpec((s, QK_PAD), lambda hd, i: (0, hd)),
                  pl.BlockSpec((V_HEAD, s), lambda hd, i: (hd, 0))],
        out_specs=[pl.BlockSpec((t, V_HEAD), lambda hd, i: (i, hd)),
                   pl.BlockSpec((t, V_HEAD), lambda hd, i: (half - 1 - i, hd))],
        out_shape=[out, out],
        scratch_shapes=[
            pltpu.VMEM((2, QK_PAD, t), BF16),
            vec,
            pltpu.VMEM((2, V_HEAD, t), F32),
            vec,
            vec,
            vec,
            pltpu.VMEM((1, t), F32),
            pltpu.VMEM((1, t), F32),
            pltpu.VMEM((t, t), F32),
        ] + [pltpu.VMEM((t, t), BF16)] * ATTN_SLOTS
          + [pltpu.VMEM((1, t), F32)] * ATTN_SLOTS,
        compiler_params=_params("parallel", "parallel"),
        name="mla_attn",
    )(q, q, k, vt)


def _out_proj_body(h_ref, oa_ref, ob_ref, g_ref, wo_ref, wu_ref, wd_ref, o_ref, *, half):
    o = jnp.where(pl.program_id(0) < half, oa_ref[...], ob_ref[...])
    h = h_ref[...] + _rms(_dot(o, wo_ref[...]), g_ref[1:2, :])
    o_ref[...] = _mlp_tail(h, g_ref, wu_ref, wd_ref)


def _out_proj_mlp(h, oa, ob, g, w_o, w_up, w_down, layer, *, tm=512):
    s = h.shape[0]
    half = oa.shape[0] // tm
    return pl.pallas_call(
        functools.partial(_out_proj_body, half=half),
        grid=(s // tm,),
        in_specs=[_row_spec(tm, D_MODEL),
                  pl.BlockSpec((tm, D_MODEL), lambda i: (jnp.minimum(i, half - 1), 0)),
                  pl.BlockSpec((tm, D_MODEL), lambda i: (jnp.maximum(i - half, 0), 0)),
                  _const_spec(g.shape), _const_spec(w_o.shape)]
        + _mlp_weight_specs(w_up, w_down, layer),
        out_specs=_row_spec(tm, D_MODEL),
        out_shape=jax.ShapeDtypeStruct(h.shape, F32),
        compiler_params=_params("parallel"),
        name="mla_out_mlp",
    )(h, oa, ob, g, w_o, w_up, w_down)


def _s5_disc_body(lr_ref, li_ref, ldt_ref, br_ref, bi_ref, abr_ref, abi_ref, bbr_ref, bbi_ref):
    dt = jnp.exp(ldt_ref[...])
    lr = lr_ref[...]
    li = li_ref[...]
    mag = jnp.exp(lr * dt)
    ab_re = mag * jnp.cos(li * dt)
    ab_im = mag * jnp.sin(li * dt)
    den = lr * lr + li * li
    f_re = ((ab_re - 1.0) * lr + ab_im * li) / den
    f_im = (ab_im * lr - (ab_re - 1.0) * li) / den
    abr_ref[...] = ab_re
    abi_ref[...] = ab_im
    br = br_ref[...]
    bi = bi_ref[...]
    bbr_ref[...] = f_re[:, None, :] * br - f_im[:, None, :] * bi
    bbi_ref[...] = f_re[:, None, :] * bi + f_im[:, None, :] * br


def _s5_discretise(lam_re, lam_im, log_dt, b_re_t, b_im_t):
    g, p = lam_re.shape
    c = b_re_t.shape[1]
    return pl.pallas_call(
        _s5_disc_body,
        out_shape=[jax.ShapeDtypeStruct((g, p), F32), jax.ShapeDtypeStruct((g, p), F32),
                   jax.ShapeDtypeStruct((g, c, p), F32), jax.ShapeDtypeStruct((g, c, p), F32)],
        name="s5_disc",
    )(lam_re, lam_im, log_dt.reshape(g, 1), b_re_t, b_im_t)


def _s5_body(h_ref, g_ref, perm_ref, ar_ref, ai_ref, bblk_ref, cblk_ref, dskip_ref,
             wa_ref, wb_ref, o_ref, xr0_ref, xi0_ref, xr1_ref, xi1_ref, xc0_ref, xc1_ref,
             yp_ref, y_ref, pr_ref, pi_ref, cr_ref, ci_ref, *, tm):
    seg = tm // SUBLANES
    i0 = pl.program_id(0)
    nst = S5_BLOCK_STATE
    pack = 2 * SUBLANES

    @pl.when(i0 == 0)
    def _():
        cr_ref[...] = jnp.zeros(cr_ref.shape, F32)
        ci_ref[...] = jnp.zeros(ci_ref.shape, F32)
        ar = ar_ref[...]
        ai = ai_ref[...]
        bcast = lambda v: jnp.broadcast_to(v, (SUBLANES, v.shape[1]))
        pr_ref[0:SUBLANES, :] = bcast(ar)
        pi_ref[0:SUBLANES, :] = bcast(ai)

        def pw(j, carry):
            r, im = carry
            r, im = r * ar - im * ai, r * ai + im * ar
            rows = pl.ds(pl.multiple_of(j * SUBLANES, SUBLANES), SUBLANES)
            pr_ref[rows, :] = bcast(r)
            pi_ref[rows, :] = bcast(im)
            return r, im

        lax.fori_loop(1, seg, pw, (ar, ai))

    h = h_ref[...]
    z = _rms(h, g_ref[0:1, :])
    zp = _dot(perm_ref[...], z.astype(BF16)).astype(BF16)

    for cb in range(S5_BLOCKS):
        st = slice(cb * nst, (cb + 1) * nst)
        xr_ref, xi_ref, xc_ref = ((xr0_ref, xi0_ref, xc0_ref), (xr1_ref, xi1_ref, xc1_ref))[cb % 2]
        bu = _dot(zp[:, cb * LANES:(cb + 1) * LANES], bblk_ref[cb])
        ar = jnp.broadcast_to(ar_ref[:, st], (SUBLANES, nst))
        ai = jnp.broadcast_to(ai_ref[:, st], (SUBLANES, nst))
        xr = jnp.zeros((SUBLANES, nst), F32)
        xi = xr
        for i in range(seg):
            rows = slice(i * SUBLANES, (i + 1) * SUBLANES)
            xr, xi = (ar * xr - ai * xi + bu[rows, :nst], ar * xi + ai * xr + bu[rows, nst:])
            xr_ref[rows, :] = xr
            xi_ref[rows, :] = xi
        alr = pr_ref[tm - 1:tm, st]
        ali = pi_ref[tm - 1:tm, st]
        c_r = cr_ref[:, st]
        c_i = ci_ref[:, st]
        rows_r, rows_i = [], []
        for sgm in range(SUBLANES):
            rows_r.append(c_r)
            rows_i.append(c_i)
            e_r = xr[sgm:sgm + 1, :]
            e_i = xi[sgm:sgm + 1, :]
            c_r, c_i = alr * c_r - ali * c_i + e_r, alr * c_i + ali * c_r + e_i
        cr_ref[:, st] = c_r
        ci_ref[:, st] = c_i
        cin_r = jnp.concatenate(rows_r, axis=0)
        cin_i = jnp.concatenate(rows_i, axis=0)
        for i2 in range(seg // 2):
            fr, fi = [], []
            for i in (2 * i2, 2 * i2 + 1):
                rows = slice(i * SUBLANES, (i + 1) * SUBLANES)
                p_r = pr_ref[rows, st]
                p_i = pi_ref[rows, st]
                fr.append(xr_ref[rows, :] + (p_r * cin_r - p_i * cin_i))
                fi.append(xi_ref[rows, :] + (p_r * cin_i + p_i * cin_r))
            rows2 = slice(i2 * pack, (i2 + 1) * pack)
            xc_ref[rows2, 0:nst] = jnp.concatenate(fr, axis=0).astype(BF16)
            xc_ref[rows2, nst:2 * nst] = jnp.concatenate(fi, axis=0).astype(BF16)
        yp_ref[cb] = _dot(xc_ref[...], cblk_ref[cb])

    for sgm in range(SUBLANES):
        for j in range(seg // SUBLANES):
            dst = slice(sgm * seg + j * SUBLANES, sgm * seg + (j + 1) * SUBLANES)
            src = pl.ds(j * SUBLANES * SUBLANES + sgm, SUBLANES, stride=SUBLANES)
            for cb in range(S5_BLOCKS):
                y_ref[dst, cb * LANES:(cb + 1) * LANES] = yp_ref[cb, src, :]
    y = _gelu(y_ref[...] + dskip_ref[...] * z).astype(BF16)
    out = _dot(y, wa_ref[...]) * jax.nn.sigmoid(_dot(y, wb_ref[...]))
    o_ref[...] = h + _rms(out, g_ref[1:2, :])


def _s5_permutation(tm):
    seg = tm // SUBLANES
    r = jnp.arange(tm)
    src = (r % SUBLANES) * seg + r // SUBLANES
    return (src[:, None] == jnp.arange(tm)[None, :]).astype(BF16)


def _s5(h, g, ab_re, ab_im, bblk, cblk, dskip, w_a, w_b, *, tm=512):
    s = h.shape[0]
    seg = tm // SUBLANES
    nstate = S5_GROUPS * S5_STATE
    perm = _s5_permutation(tm)
    consts = (g, perm, ab_re, ab_im, bblk, cblk, dskip, w_a, w_b)
    state = pltpu.VMEM((tm, S5_BLOCK_STATE), F32)
    return pl.pallas_call(
        functools.partial(_s5_body, tm=tm),
        grid=(s // tm,),
        in_specs=[_row_spec(tm, D_MODEL)] + [_const_spec(a.shape) for a in consts],
        out_specs=_row_spec(tm, D_MODEL),
        out_shape=jax.ShapeDtypeStruct(h.shape, F32),
        scratch_shapes=[
            state, state, state, state,
            pltpu.VMEM((tm, 2 * S5_BLOCK_STATE), BF16),
            pltpu.VMEM((tm, 2 * S5_BLOCK_STATE), BF16),
            pltpu.VMEM((S5_BLOCKS, tm, LANES), F32),
            pltpu.VMEM((tm, D_MODEL), F32),
            pltpu.VMEM((tm, nstate), F32),
            pltpu.VMEM((tm, nstate), F32),
            pltpu.VMEM((1, nstate), F32),
            pltpu.VMEM((1, nstate), F32),
        ],
        compiler_params=_params("arbitrary"),
        name="s5",
    )(h, *consts)


def _s5_block_matrices(bb_re, bb_im, c_re, c_im):
    nb, gb, c, p = S5_BLOCKS, S5_BLOCK_GROUPS, S5_GROUP_CH, S5_STATE
    eye = jnp.eye(gb, dtype=F32)

    def in_map(bb):
        t = bb.reshape(nb, gb, c, p)
        return jnp.einsum('bgcp,gk->bgckp', t, eye).reshape(nb, gb * c, gb * p)

    def out_map(cc):
        t = cc.reshape(nb, gb, c, p)
        return jnp.einsum('bgcp,gk->bgpkc', t, eye).reshape(nb, gb * p, gb * c)

    bblk = jnp.concatenate([in_map(bb_re), in_map(bb_im)], axis=2).astype(BF16)
    cblk = jnp.concatenate([out_map(c_re), out_map(-c_im)], axis=1).astype(BF16)
    return bblk, cblk


def _mla_weights(w_uq, w_dkv):
    wq = w_uq.reshape(Q_LORA, MLA_HEADS, QK_NOPE + QK_ROPE)
    w_n = wq[:, :, :QK_NOPE].reshape(Q_LORA, MLA_HEADS * QK_NOPE)
    w_r = jnp.pad(wq[:, :, QK_NOPE:], ((0, 0), (0, 0), (0, LANES - QK_ROPE)))
    w_r = w_r.reshape(Q_LORA, MLA_HEADS * LANES)
    w_dkv_p = jnp.pad(w_dkv, ((0, 0), (0, LANES - QK_ROPE)))
    return w_n.astype(BF16), w_r.astype(BF16), w_dkv_p.astype(BF16)


def kernel(x, positions, norm_g, w_up, w_down, a_w_in, a_b_in, a_g_v, a_b_v, a_w_s, a_b_s, a_w_out, b_w_grp, b_scale, c_w_dq, c_g_q, c_w_uq, c_w_dkv, c_g_kv, c_w_uk, c_w_uv, c_w_o, d_lam_re, d_lam_im, d_log_dt, d_b_re, d_b_im, d_c_re, d_c_im, d_skip, d_w_glu_a, d_w_glu_b):
    bsz, s, d = x.shape
    assert bsz == 1 and d == D_MODEL
    h = x.reshape(s, d)
    row = lambda a: a.reshape(1, -1)
    w_up_b = w_up.astype(BF16)
    w_down_b = w_down.astype(BF16)

    b_s_full = jnp.repeat(a_b_s[0].T, D_MODEL // GMLP_HEADS, axis=1)
    h = _gmlp(h, norm_g[0], a_w_in[0].astype(BF16), row(a_b_in[0]), row(a_g_v[0]), row(a_b_v[0]),
              a_w_s[0], b_s_full, a_w_out[0].astype(BF16))
    h = _mlp(h, norm_g[0], w_up_b, w_down_b, 0)

    h = _pool_mlp(h, norm_g[1], b_w_grp[0].astype(BF16), row(b_scale[0]), w_up_b, w_down_b, 1)

    w_uq_n, w_uq_r, w_dkv_p = _mla_weights(c_w_uq[0], c_w_dkv[0])
    inv_freq = ROPE_THETA ** (-jnp.arange(0, QK_ROPE, 2, dtype=F32) / QK_ROPE)
    invf = jnp.concatenate([inv_freq, inv_freq, jnp.zeros((LANES - QK_ROPE,), F32)]).reshape(1, LANES)
    q, k, v = _mla_proj(h, positions.reshape(s, 1), norm_g[2], c_w_dq[0].astype(BF16), row(c_g_q[0]),
                        w_uq_n, w_uq_r, w_dkv_p, row(c_g_kv[0]), c_w_uk[0].astype(BF16),
                        c_w_uv[0].T.astype(BF16), invf)
    oa, ob = _attention(q, k, v)
    h = _out_proj_mlp(h, oa, ob, norm_g[2], c_w_o[0].astype(BF16), w_up_b, w_down_b, 2)

    tr = lambda a: jnp.swapaxes(a, 1, 2)
    ab_re, ab_im, bb_re, bb_im = _s5_discretise(d_lam_re[0], d_lam_im[0], d_log_dt[0],
                                                tr(d_b_re[0]), tr(d_b_im[0]))
    bblk, cblk = _s5_block_matrices(bb_re, bb_im, d_c_re[0], d_c_im[0])
    h = _s5(h, norm_g[3], row(ab_re), row(ab_im), bblk, cblk, row(d_skip[0]),
            d_w_glu_a[0].astype(BF16), d_w_glu_b[0].astype(BF16))
    h = _mlp(h, norm_g[3], w_up_b, w_down_b, 3)
    return h.reshape(bsz, s, d)
```

```python
import functools
import math

import jax
import jax.numpy as jnp
from jax import lax
from jax.experimental import pallas as pl
from jax.experimental.pallas import tpu as pltpu

F32 = jnp.float32
BF16 = jnp.bfloat16

D_MODEL = 1024
D_FF = 4 * D_MODEL
RMS_EPS = 1e-6
LANES = 128
SUBLANES = 8
VMEM_LIMIT = 56 * 1024 * 1024

CHUNK = 128
GMLP_HEADS = 8
POOL_WINDOWS = (2, 4, 8, 16)
POOL_GROUP = D_MODEL // len(POOL_WINDOWS)
POOL_HALO = 32
assert POOL_WINDOWS == tuple(2 << k for k in range(len(POOL_WINDOWS)))
assert POOL_HALO == SUBLANES * len(POOL_WINDOWS)
MLA_HEADS = 8
Q_LORA = 384
KV_LORA = 256
QK_NOPE = 128
QK_ROPE = 64
V_HEAD = 128
ROPE_THETA = 10000.0
QK_PAD = 256
S5_GROUP_CH = 16
S5_GROUPS = D_MODEL // S5_GROUP_CH
S5_STATE = 64
S5_BLOCK_GROUPS = LANES // S5_GROUP_CH
S5_BLOCKS = S5_GROUPS // S5_BLOCK_GROUPS
S5_BLOCK_STATE = S5_BLOCK_GROUPS * S5_STATE


def _rms(x, g):
    return x * lax.rsqrt(jnp.mean(x * x, axis=-1, keepdims=True) + RMS_EPS) * g


def _gelu(x):
    c = math.sqrt(2.0 / math.pi)
    return 0.5 * x * (1.0 + jnp.tanh(c * (x + 0.044715 * (x * x * x))))


def _dot(a, b):
    return jnp.dot(a, b, preferred_element_type=F32)


def _const_spec(shape):
    nd = len(shape)
    return pl.BlockSpec(shape, lambda *_: (0,) * nd, pipeline_mode=pl.Buffered(1))


def _row_spec(tm, width):
    return pl.BlockSpec((tm, width), lambda i: (i, 0))


def _params(*sem, flags=None):
    return pltpu.CompilerParams(dimension_semantics=sem, vmem_limit_bytes=VMEM_LIMIT, flags=flags)


MLP_TF = 1024


def _mlp_tail(h, g_ref, wu_ref, wd_ref):
    z = _rms(h, g_ref[2:3, :]).astype(BF16)
    acc = jnp.zeros(h.shape, F32)
    for c in range(D_FF // MLP_TF):
        a = _dot(z, wu_ref[:, c * MLP_TF:(c + 1) * MLP_TF])
        a = jnp.square(jnp.maximum(a, 0.0)).astype(BF16)
        acc = acc + _dot(a, wd_ref[c * MLP_TF:(c + 1) * MLP_TF, :])
    return h + _rms(acc, g_ref[3:4, :])


def _mlp_weight_specs(w_up, w_down, layer):
    return [pl.BlockSpec((None,) + w.shape[1:], lambda *_: (layer, 0, 0), pipeline_mode=pl.Buffered(1))
            for w in (w_up, w_down)]


def _mlp_body(h_ref, g_ref, wu_ref, wd_ref, o_ref):
    o_ref[...] = _mlp_tail(h_ref[...], g_ref, wu_ref, wd_ref)


def _mlp(h, g, w_up, w_down, layer, *, tm=512):
    s = h.shape[0]
    return pl.pallas_call(
        _mlp_body,
        grid=(s // tm,),
        in_specs=[_row_spec(tm, D_MODEL), _const_spec(g.shape)] + _mlp_weight_specs(w_up, w_down, layer),
        out_specs=_row_spec(tm, D_MODEL),
        out_shape=jax.ShapeDtypeStruct(h.shape, F32),
        compiler_params=_params("parallel"),
        name="mlp",
    )(h, g, w_up, w_down)


def _gmlp_body(h_ref, g_ref, win_ref, bin_ref, gv_ref, bv_ref, ws_ref, bs_ref, wout_ref, o_ref,
               sv_ref, *, tm):
    h = h_ref[...]
    z = _rms(h, g_ref[0:1, :]).astype(BF16)
    uv = _gelu(_dot(z, win_ref[...]) + bin_ref[...])
    u = uv[:, :D_MODEL]
    v = uv[:, D_MODEL:]
    mu = jnp.mean(v, axis=-1, keepdims=True)
    vc = v - mu
    v = vc * lax.rsqrt(jnp.mean(vc * vc, axis=-1, keepdims=True) + RMS_EPS)
    v = (v * gv_ref[...] + bv_ref[...]).astype(BF16)
    row = lax.broadcasted_iota(jnp.int32, (CHUNK, CHUNK), 0)
    col = lax.broadcasted_iota(jnp.int32, (CHUNK, CHUNK), 1)
    nchunk = tm // CHUNK
    for hd in range(GMLP_HEADS):
        ws = jnp.where(col <= row, ws_ref[hd], 0.0).astype(BF16)
        cols = slice(hd * LANES, (hd + 1) * LANES)
        rhs = jnp.concatenate([v[c * CHUNK:(c + 1) * CHUNK, cols] for c in range(nchunk)], axis=1)
        sv = _dot(ws, rhs)
        for c in range(nchunk):
            sv_ref[c * CHUNK:(c + 1) * CHUNK, cols] = (
                sv[:, c * LANES:(c + 1) * LANES] + bs_ref[:, cols])
    y = (u * sv_ref[...]).astype(BF16)
    o_ref[...] = h + _rms(_dot(y, wout_ref[...]), g_ref[1:2, :])


def _gmlp(h, g, w_in, b_in, g_v, b_v, w_s, b_s_full, w_out, *, tm=1024):
    s = h.shape[0]
    return pl.pallas_call(
        functools.partial(_gmlp_body, tm=tm),
        grid=(s // tm,),
        in_specs=[_row_spec(tm, D_MODEL), _const_spec(g.shape), _const_spec(w_in.shape),
                  _const_spec(b_in.shape), _const_spec(g_v.shape), _const_spec(b_v.shape),
                  _const_spec(w_s.shape), _const_spec(b_s_full.shape), _const_spec(w_out.shape)],
        out_specs=_row_spec(tm, D_MODEL),
        out_shape=jax.ShapeDtypeStruct(h.shape, F32),
        scratch_shapes=[pltpu.VMEM((tm, D_MODEL), F32)],
        compiler_params=_params("parallel"),
        name="gmlp",
    )(h, g, w_in, b_in, g_v, b_v, w_s, b_s_full, w_out)


def _pool_body(h_ref, halo_ref, g_ref, w_ref, scale_ref, wu_ref, wd_ref, o_ref, zz_ref, ws_ref,
               *, tm):
    i = pl.program_id(0)
    h = h_ref[...]
    z = _rms(h, g_ref[0:1, :])
    zh = _rms(halo_ref[...], g_ref[0:1, :])
    zz_ref[0:POOL_HALO, :] = jnp.where(i > 0, zh, 0.0)
    zz_ref[POOL_HALO:, :] = z
    nrows = tm + POOL_HALO
    for k in range(1, len(POOL_WINDOWS) + 1):
        d = 1 << (k - 1)
        lo = SUBLANES * k
        cols = slice((k - 1) * POOL_GROUP, D_MODEL)
        src = zz_ref if k == 1 else ws_ref
        ws_ref[lo:, cols] = src[lo:, cols] + src[lo - d:nrows - d, cols]
    pos = i * tm + lax.broadcasted_iota(jnp.int32, (tm, 1), 0)
    outs = []
    for gi, w in enumerate(POOL_WINDOWS):
        cols = slice(gi * POOL_GROUP, (gi + 1) * POOL_GROUP)
        xg = z[:, cols]
        win = ws_ref[POOL_HALO:, cols]
        count = jnp.minimum(pos + 1, w).astype(F32)
        diff = (win / count - xg).astype(BF16)
        outs.append(_dot(diff, w_ref[gi]))
    out = jnp.concatenate(outs, axis=-1) * scale_ref[...]
    o_ref[...] = _mlp_tail(h + _rms(out, g_ref[1:2, :]), g_ref, wu_ref, wd_ref)


def _pool_mlp(h, g, w_grp, scale, w_up, w_down, layer, *, tm=512):
    s = h.shape[0]
    ratio = tm // POOL_HALO
    return pl.pallas_call(
        functools.partial(_pool_body, tm=tm),
        grid=(s // tm,),
        in_specs=[_row_spec(tm, D_MODEL),
                  pl.BlockSpec((POOL_HALO, D_MODEL), lambda i: (jnp.maximum(i * ratio - 1, 0), 0)),
                  _const_spec(g.shape), _const_spec(w_grp.shape), _const_spec(scale.shape)]
        + _mlp_weight_specs(w_up, w_down, layer),
        out_specs=_row_spec(tm, D_MODEL),
        out_shape=jax.ShapeDtypeStruct(h.shape, F32),
        scratch_shapes=[pltpu.VMEM((tm + POOL_HALO, D_MODEL), F32)] * 2,
        compiler_params=_params("parallel"),
        name="pool_mlp",
    )(h, h, g, w_grp, scale, w_up, w_down)


def _rope_tile(x, cos, sin_lo, sin_hi):
    n = x.shape[1] // LANES
    outs = []
    for t in range(n):
        xt = x[:, t * LANES:(t + 1) * LANES]
        up = pltpu.roll(xt, 32, axis=1)
        down = pltpu.roll(xt, LANES - 32, axis=1)
        outs.append(xt * cos + down * sin_lo + up * sin_hi)
    return outs


def _mla_proj_body(h_ref, pos_ref, g_ref, wdq_ref, gq_ref, wuqn_ref, wuqr_ref, wdkv_ref, gkv_ref,
                   wuk_ref, wuv_ref, invf_ref, q_ref, k_ref, v_ref, *, scale):
    h = h_ref[...]
    z = _rms(h, g_ref[0:1, :]).astype(BF16)
    ang = pos_ref[...].astype(F32) * invf_ref[...]
    lane = lax.broadcasted_iota(jnp.int32, ang.shape, 1)
    cos = jnp.where(lane < QK_ROPE, jnp.cos(ang), 0.0)
    sin = jnp.sin(ang)
    sin_lo = jnp.where(lane < QK_ROPE // 2, -sin, 0.0)
    sin_hi = jnp.where((lane >= QK_ROPE // 2) & (lane < QK_ROPE), sin, 0.0)

    ql = _rms(_dot(z, wdq_ref[...]), gq_ref[...]).astype(BF16)
    qn = _dot(ql, wuqn_ref[...]) * scale
    qr = _rope_tile(_dot(ql, wuqr_ref[...]) * scale, cos, sin_lo, sin_hi)
    ckv = _dot(z, wdkv_ref[...])
    c = _rms(ckv[:, :KV_LORA], gkv_ref[...]).astype(BF16)
    kr = _rope_tile(ckv[:, KV_LORA:], cos, sin_lo, sin_hi)[0].astype(BF16)
    kn = _dot(c, wuk_ref[...])
    for hd in range(MLA_HEADS):
        cols = slice(hd * LANES, (hd + 1) * LANES)
        q_ref[:, hd * QK_PAD:hd * QK_PAD + LANES] = qn[:, cols].astype(BF16)
        q_ref[:, hd * QK_PAD + LANES:(hd + 1) * QK_PAD] = qr[hd].astype(BF16)
        k_ref[:, hd * QK_PAD:hd * QK_PAD + LANES] = kn[:, cols].astype(BF16)
        k_ref[:, hd * QK_PAD + LANES:(hd + 1) * QK_PAD] = kr
    v_ref[...] = lax.dot_general(wuv_ref[...], c, (((1,), (1,)), ((), ())),
                                 preferred_element_type=F32).astype(BF16)


def _mla_proj(h, pos, g, w_dq, g_q, w_uq_n, w_uq_r, w_dkv, g_kv, w_uk, w_uv, invf, *, tm=1024):
    s = h.shape[0]
    scale = (QK_NOPE + QK_ROPE) ** -0.5 * math.log2(math.e)
    consts = (g, w_dq, g_q, w_uq_n, w_uq_r, w_dkv, g_kv, w_uk, w_uv, invf)
    return pl.pallas_call(
        functools.partial(_mla_proj_body, scale=scale),
        grid=(s // tm,),
        in_specs=[_row_spec(tm, D_MODEL), _row_spec(tm, 1)] + [_const_spec(a.shape) for a in consts],
        out_specs=[_row_spec(tm, MLA_HEADS * QK_PAD), _row_spec(tm, MLA_HEADS * QK_PAD),
                   pl.BlockSpec((MLA_HEADS * V_HEAD, tm), lambda i: (0, i))],
        out_shape=[jax.ShapeDtypeStruct((s, MLA_HEADS * QK_PAD), BF16),
                   jax.ShapeDtypeStruct((s, MLA_HEADS * QK_PAD), BF16),
                   jax.ShapeDtypeStruct((MLA_HEADS * V_HEAD, s), BF16)],
        compiler_params=_params("parallel"),
        name="mla_proj",
    )(h, pos, *consts)


ATTN_SLOTS = 4
ATTN_UNROLL = 16
ATTN_MAX_JUMP = 64.0


def _attn_body(qa_ref, qb_ref, k_ref, vt_ref, oa_ref, ob_ref, qt_ref, m_ref, acc_ref, l_ref, lag_ref,
               rl_ref, jump_ref, low_ref, s_ref, *slot_refs, t, nt):
    qa = pl.program_id(1)
    qb = nt - 1 - qa
    ns = ATTN_SLOTS
    p_refs, a_refs = slot_refs[:ns], slot_refs[ns:]
    neg = -1e30
    rows = 2 * SUBLANES
    last = nt

    qt_ref[0] = qa_ref[...].astype(F32).T.astype(BF16)
    qt_ref[1] = qb_ref[...].astype(F32).T.astype(BF16)
    for state in (m_ref, acc_ref, l_ref, lag_ref, rl_ref, jump_ref, low_ref):
        state[...] = jnp.zeros(state.shape, F32)

    def owner(j):
        return jnp.where(j < 2, j, (j >= qa + 2).astype(jnp.int32))

    def key_tile(j):
        full = jnp.where(j >= qa + 2, j - 2 - qa, j - 2)
        return jnp.where(j == 0, qa, jnp.where(j == 1, qb, full))

    def scores(j, masked):
        start = pl.multiple_of(key_tile(j) * t, t)
        s = _dot(k_ref[pl.ds(start, t), :], qt_ref[owner(j)])
        if masked:
            kpos = lax.broadcasted_iota(jnp.int32, (t, t), 0)
            qpos = lax.broadcasted_iota(jnp.int32, (t, t), 1)
            s = jnp.where(kpos <= qpos, s, neg)
        return s

    def fused(j, slot, masked=False):
        own = owner(j)
        ref = lag_ref[own]
        alpha = jnp.exp2(rl_ref[own] - ref)
        a_refs[slot][...] = alpha
        rl_ref[own] = ref
        s = scores(j, masked)
        rb = jnp.broadcast_to(ref, (SUBLANES, t))
        p_ref = p_refs[slot]
        mx = [None, None]
        ps = [None, None]
        for r in range(t // rows):
            lo = s[r * rows:r * rows + SUBLANES, :]
            hi = s[r * rows + SUBLANES:(r + 1) * rows, :]
            mx[0] = lo if mx[0] is None else jnp.maximum(mx[0], lo)
            mx[1] = hi if mx[1] is None else jnp.maximum(mx[1], hi)
            plo = jnp.exp2(lo - rb)
            phi = jnp.exp2(hi - rb)
            ps[0] = plo if ps[0] is None else ps[0] + plo
            ps[1] = phi if ps[1] is None else ps[1] + phi
            p_ref[r * rows:(r + 1) * rows, :] = jnp.concatenate([plo, phi], axis=0).astype(BF16)
        l_ref[own] = l_ref[own] * alpha + jnp.sum(ps[0] + ps[1], axis=0, keepdims=True)
        m_tile = jnp.max(jnp.maximum(mx[0], mx[1]), axis=0, keepdims=True)
        m_old = m_ref[own]
        lag_ref[own] = m_old
        m_ref[own] = jnp.maximum(m_old, m_tile)
        jump_ref[...] = jnp.maximum(jump_ref[...], m_tile - ref)
        if masked:
            low_ref[...] = jnp.minimum(low_ref[...], m_tile - ref)

    def pv(j, slot):
        start = pl.multiple_of(key_tile(j) * t, t)
        own = owner(j)
        acc_ref[own] = acc_ref[own] * a_refs[slot][...] + _dot(vt_ref[:, pl.ds(start, t)],
                                                                p_refs[slot][...])

    def stage(k, slot):
        fused(k + 2, (slot + 2) % ns)
        pv(k, slot)

    fused(0, 0, masked=True)
    fused(1, 1, masked=True)
    nstage = nt - 1

    def body(jq, _):
        for u in range(ATTN_UNROLL):
            stage(ATTN_UNROLL * jq + u, u % ns)
        return 0

    lax.fori_loop(0, nstage // ATTN_UNROLL, body, 0)
    for k in range(nstage - nstage % ATTN_UNROLL, nstage):
        stage(k, k % ns)
    pv(last - 1, (last - 1) % ns)
    pv(last, last % ns)

    def finish():
        for own, o_ref in ((0, oa_ref), (1, ob_ref)):
            o_t = acc_ref[own] * (1.0 / l_ref[own])
            o_ref[...] = o_t.T.astype(o_ref.dtype)

    finish()

    @pl.when((jnp.max(jump_ref[...]) > ATTN_MAX_JUMP) | (jnp.min(low_ref[...]) < -ATTN_MAX_JUMP))
    def _():
        m_ref[...] = jnp.full(m_ref.shape, neg, F32)
        acc_ref[...] = jnp.zeros(acc_ref.shape, F32)
        l_ref[...] = jnp.zeros(l_ref.shape, F32)

        def exact(j, masked):
            own = owner(j)
            s_ref[...] = scores(j, masked)
            mx = s_ref[0:SUBLANES, :]
            for r in range(1, t // SUBLANES):
                mx = jnp.maximum(mx, s_ref[r * SUBLANES:(r + 1) * SUBLANES, :])
            m_old = m_ref[own]
            m_new = jnp.maximum(m_old, jnp.max(mx, axis=0, keepdims=True))
            m_ref[own] = m_new
            alpha = jnp.exp2(m_old - m_new)
            a_refs[0][...] = alpha
            mb = jnp.broadcast_to(m_new, (rows, t))
            psum = jnp.zeros((rows, t), F32)
            for r in range(t // rows):
                p = jnp.exp2(s_ref[r * rows:(r + 1) * rows, :] - mb)
                psum = psum + p
                p_refs[0][r * rows:(r + 1) * rows, :] = p.astype(BF16)
            l_ref[own] = l_ref[own] * alpha + jnp.sum(psum, axis=0, keepdims=True)
            pv(j, 0)

        exact(0, True)
        exact(1, True)
        lax.fori_loop(2, last + 1, lambda j, c: (exact(j, False), c)[1], 0)
        finish()


def _attention(q, k, vt, *, t=512):
    s = q.shape[0]
    nt = s // t
    assert nt % 2 == 0
    half = nt // 2
    out = jax.ShapeDtypeStruct((s // 2, MLA_HEADS * V_HEAD), BF16)
    vec = pltpu.VMEM((2, 1, t), F32)
    return pl.pallas_call(
        functools.partial(_attn_body, t=t, nt=nt),
        grid=(MLA_HEADS, half),
        in_specs=[pl.BlockSpec((t, QK_PAD), lambda hd, i: (i, hd)),
                  pl.BlockSpec((t, QK_PAD), lambda hd, i: (nt - 1 - i, hd)),
                  pl.BlockSpec((s, QK_PAD), lambda hd, i: (0, hd)),
                  pl.BlockSpec((V_HEAD, s), lambda hd, i: (hd, 0))],
        out_specs=[pl.BlockSpec((t, V_HEAD), lambda hd, i: (i, hd)),
                   pl.BlockSpec((t, V_HEAD), lambda hd, i: (half - 1 - i, hd))],
        out_shape=[out, out],
        scratch_shapes=[
            pltpu.VMEM((2, QK_PAD, t), BF16),
            vec,
            pltpu.VMEM((2, V_HEAD, t), F32),
            vec,
            vec,
            vec,
            pltpu.VMEM((1, t), F32),
            pltpu.VMEM((1, t), F32),
            pltpu.VMEM((t, t), F32),
        ] + [pltpu.VMEM((t, t), BF16)] * ATTN_SLOTS
          + [pltpu.VMEM((1, t), F32)] * ATTN_SLOTS,
        compiler_params=_params("parallel", "parallel"),
        name="mla_attn",
    )(q, q, k, vt)


def _out_proj_body(h_ref, oa_ref, ob_ref, g_ref, wo_ref, wu_ref, wd_ref, o_ref, *, half):
    o = jnp.where(pl.program_id(0) < half, oa_ref[...], ob_ref[...])
    h = h_ref[...] + _rms(_dot(o, wo_ref[...]), g_ref[1:2, :])
    o_ref[...] = _mlp_tail(h, g_ref, wu_ref, wd_ref)


def _out_proj_mlp(h, oa, ob, g, w_o, w_up, w_down, layer, *, tm=512):
    s = h.shape[0]
    half = oa.shape[0] // tm
    return pl.pallas_call(
        functools.partial(_out_proj_body, half=half),
        grid=(s // tm,),
        in_specs=[_row_spec(tm, D_MODEL),
                  pl.BlockSpec((tm, D_MODEL), lambda i: (jnp.minimum(i, half - 1), 0)),
                  pl.BlockSpec((tm, D_MODEL), lambda i: (jnp.maximum(i - half, 0), 0)),
                  _const_spec(g.shape), _const_spec(w_o.shape)]
        + _mlp_weight_specs(w_up, w_down, layer),
        out_specs=_row_spec(tm, D_MODEL),
        out_shape=jax.ShapeDtypeStruct(h.shape, F32),
        compiler_params=_params("parallel"),
        name="mla_out_mlp",
    )(h, oa, ob, g, w_o, w_up, w_down)


def _s5_disc_body(lr_ref, li_ref, ldt_ref, br_ref, bi_ref, abr_ref, abi_ref, bbr_ref, bbi_ref):
    dt = jnp.exp(ldt_ref[...])
    lr = lr_ref[...]
    li = li_ref[...]
    mag = jnp.exp(lr * dt)
    ab_re = mag * jnp.cos(li * dt)
    ab_im = mag * jnp.sin(li * dt)
    den = lr * lr + li * li
    f_re = ((ab_re - 1.0) * lr + ab_im * li) / den
    f_im = (ab_im * lr - (ab_re - 1.0) * li) / den
    abr_ref[...] = ab_re
    abi_ref[...] = ab_im
    br = br_ref[...]
    bi = bi_ref[...]
    bbr_ref[...] = f_re[:, None, :] * br - f_im[:, None, :] * bi
    bbi_ref[...] = f_re[:, None, :] * bi + f_im[:, None, :] * br


def _s5_discretise(lam_re, lam_im, log_dt, b_re_t, b_im_t):
    g, p = lam_re.shape
    c = b_re_t.shape[1]
    return pl.pallas_call(
        _s5_disc_body,
        out_shape=[jax.ShapeDtypeStruct((g, p), F32), jax.ShapeDtypeStruct((g, p), F32),
                   jax.ShapeDtypeStruct((g, c, p), F32), jax.ShapeDtypeStruct((g, c, p), F32)],
        name="s5_disc",
    )(lam_re, lam_im, log_dt.reshape(g, 1), b_re_t, b_im_t)


def _s5_body(h_ref, g_ref, perm_ref, ar_ref, ai_ref, bblk_ref, cblk_ref, dskip_ref,
             wa_ref, wb_ref, o_ref, xr0_ref, xi0_ref, xr1_ref, xi1_ref, xc0_ref, xc1_ref,
             yp_ref, y_ref, pr_ref, pi_ref, cr_ref, ci_ref, *, tm):
    seg = tm // SUBLANES
    i0 = pl.program_id(0)
    nst = S5_BLOCK_STATE
    pack = 2 * SUBLANES

    @pl.when(i0 == 0)
    def _():
        cr_ref[...] = jnp.zeros(cr_ref.shape, F32)
        ci_ref[...] = jnp.zeros(ci_ref.shape, F32)
        ar = ar_ref[...]
        ai = ai_ref[...]
        bcast = lambda v: jnp.broadcast_to(v, (SUBLANES, v.shape[1]))
        pr_ref[0:SUBLANES, :] = bcast(ar)
        pi_ref[0:SUBLANES, :] = bcast(ai)

        def pw(j, carry):
            r, im = carry
            r, im = r * ar - im * ai, r * ai + im * ar
            rows = pl.ds(pl.multiple_of(j * SUBLANES, SUBLANES), SUBLANES)
            pr_ref[rows, :] = bcast(r)
            pi_ref[rows, :] = bcast(im)
            return r, im

        lax.fori_loop(1, seg, pw, (ar, ai))

    h = h_ref[...]
    z = _rms(h, g_ref[0:1, :])
    zp = _dot(perm_ref[...], z.astype(BF16)).astype(BF16)

    for cb in range(S5_BLOCKS):
        st = slice(cb * nst, (cb + 1) * nst)
        xr_ref, xi_ref, xc_ref = ((xr0_ref, xi0_ref, xc0_ref), (xr1_ref, xi1_ref, xc1_ref))[cb % 2]
        bu = _dot(zp[:, cb * LANES:(cb + 1) * LANES], bblk_ref[cb])
        ar = jnp.broadcast_to(ar_ref[:, st], (SUBLANES, nst))
        ai = jnp.broadcast_to(ai_ref[:, st], (SUBLANES, nst))
        xr = jnp.zeros((SUBLANES, nst), F32)
        xi = xr
        for i in range(seg):
            rows = slice(i * SUBLANES, (i + 1) * SUBLANES)
            xr, xi = (ar * xr - ai * xi + bu[rows, :nst], ar * xi + ai * xr + bu[rows, nst:])
            xr_ref[rows, :] = xr
            xi_ref[rows, :] = xi
        alr = pr_ref[tm - 1:tm, st]
        ali = pi_ref[tm - 1:tm, st]
        c_r = cr_ref[:, st]
        c_i = ci_ref[:, st]
        rows_r, rows_i = [], []
        for sgm in range(SUBLANES):
            rows_r.append(c_r)
            rows_i.append(c_i)
            e_r = xr[sgm:sgm + 1, :]
            e_i = xi[sgm:sgm + 1, :]
            c_r, c_i = alr * c_r - ali * c_i + e_r, alr * c_i + ali * c_r + e_i
        cr_ref[:, st] = c_r
        ci_ref[:, st] = c_i
        cin_r = jnp.concatenate(rows_r, axis=0)
        cin_i = jnp.concatenate(rows_i, axis=0)
        for i2 in range(seg // 2):
            fr, fi = [], []
            for i in (2 * i2, 2 * i2 + 1):
                rows = slice(i * SUBLANES, (i + 1) * SUBLANES)
                p_r = pr_ref[rows, st]
                p_i = pi_ref[rows, st]
                fr.append(xr_ref[rows, :] + (p_r * cin_r - p_i * cin_i))
                fi.append(xi_ref[rows, :] + (p_r * cin_i + p_i * cin_r))
            rows2 = slice(i2 * pack, (i2 + 1) * pack)
            xc_ref[rows2, 0:nst] = jnp.concatenate(fr, axis=0).astype(BF16)
            xc_ref[rows2, nst:2 * nst] = jnp.concatenate(fi, axis=0).astype(BF16)
        yp_ref[cb] = _dot(xc_ref[...], cblk_ref[cb])

    for sgm in range(SUBLANES):
        for j in range(seg // SUBLANES):
            dst = slice(sgm * seg + j * SUBLANES, sgm * seg + (j + 1) * SUBLANES)
            src = pl.ds(j * SUBLANES * SUBLANES + sgm, SUBLANES, stride=SUBLANES)
            for cb in range(S5_BLOCKS):
                y_ref[dst, cb * LANES:(cb + 1) * LANES] = yp_ref[cb, src, :]
    y = _gelu(y_ref[...] + dskip_ref[...] * z).astype(BF16)
    out = _dot(y, wa_ref[...]) * jax.nn.sigmoid(_dot(y, wb_ref[...]))
    o_ref[...] = h + _rms(out, g_ref[1:2, :])


def _s5_permutation(tm):
    seg = tm // SUBLANES
    r = jnp.arange(tm)
    src = (r % SUBLANES) * seg + r // SUBLANES
    return (src[:, None] == jnp.arange(tm)[None, :]).astype(BF16)


def _s5(h, g, ab_re, ab_im, bblk, cblk, dskip, w_a, w_b, *, tm=512):
    s = h.shape[0]
    seg = tm // SUBLANES
    nstate = S5_GROUPS * S5_STATE
    perm = _s5_permutation(tm)
    consts = (g, perm, ab_re, ab_im, bblk, cblk, dskip, w_a, w_b)
    state = pltpu.VMEM((tm, S5_BLOCK_STATE), F32)
    return pl.pallas_call(
        functools.partial(_s5_body, tm=tm),
        grid=(s // tm,),
        in_specs=[_row_spec(tm, D_MODEL)] + [_const_spec(a.shape) for a in consts],
        out_specs=_row_spec(tm, D_MODEL),
        out_shape=jax.ShapeDtypeStruct(h.shape, F32),
        scratch_shapes=[
            state, state, state, state,
            pltpu.VMEM((tm, 2 * S5_BLOCK_STATE), BF16),
            pltpu.VMEM((tm, 2 * S5_BLOCK_STATE), BF16),
            pltpu.VMEM((S5_BLOCKS, tm, LANES), F32),
            pltpu.VMEM((tm, D_MODEL), F32),
            pltpu.VMEM((tm, nstate), F32),
            pltpu.VMEM((tm, nstate), F32),
            pltpu.VMEM((1, nstate), F32),
            pltpu.VMEM((1, nstate), F32),
        ],
        compiler_params=_params("arbitrary"),
        name="s5",
    )(h, *consts)


def _s5_block_matrices(bb_re, bb_im, c_re, c_im):
    nb, gb, c, p = S5_BLOCKS, S5_BLOCK_GROUPS, S5_GROUP_CH, S5_STATE
    eye = jnp.eye(gb, dtype=F32)

    def in_map(bb):
        t = bb.reshape(nb, gb, c, p)
        return jnp.einsum('bgcp,gk->bgckp', t, eye).reshape(nb, gb * c, gb * p)

    def out_map(cc):
        t = cc.reshape(nb, gb, c, p)
        return jnp.einsum('bgcp,gk->bgpkc', t, eye).reshape(nb, gb * p, gb * c)

    bblk = jnp.concatenate([in_map(bb_re), in_map(bb_im)], axis=2).astype(BF16)
    cblk = jnp.concatenate([out_map(c_re), out_map(-c_im)], axis=1).astype(BF16)
    return bblk, cblk


def _mla_weights(w_uq, w_dkv):
    wq = w_uq.reshape(Q_LORA, MLA_HEADS, QK_NOPE + QK_ROPE)
    w_n = wq[:, :, :QK_NOPE].reshape(Q_LORA, MLA_HEADS * QK_NOPE)
    w_r = jnp.pad(wq[:, :, QK_NOPE:], ((0, 0), (0, 0), (0, LANES - QK_ROPE)))
    w_r = w_r.reshape(Q_LORA, MLA_HEADS * LANES)
    w_dkv_p = jnp.pad(w_dkv, ((0, 0), (0, LANES - QK_ROPE)))
    return w_n.astype(BF16), w_r.astype(BF16), w_dkv_p.astype(BF16)


def kernel(x, positions, norm_g, w_up, w_down, a_w_in, a_b_in, a_g_v, a_b_v, a_w_s, a_b_s, a_w_out, b_w_grp, b_scale, c_w_dq, c_g_q, c_w_uq, c_w_dkv, c_g_kv, c_w_uk, c_w_uv, c_w_o, d_lam_re, d_lam_im, d_log_dt, d_b_re, d_b_im, d_c_re, d_c_im, d_skip, d_w_glu_a, d_w_glu_b):
    bsz, s, d = x.shape
    assert bsz == 1 and d == D_MODEL
    h = x.reshape(s, d)
    row = lambda a: a.reshape(1, -1)
    w_up_b = w_up.astype(BF16)
    w_down_b = w_down.astype(BF16)

    b_s_full = jnp.repeat(a_b_s[0].T, D_MODEL // GMLP_HEADS, axis=1)
    h = _gmlp(h, norm_g[0], a_w_in[0].astype(BF16), row(a_b_in[0]), row(a_g_v[0]), row(a_b_v[0]),
              a_w_s[0], b_s_full, a_w_out[0].astype(BF16))
    h = _mlp(h, norm_g[0], w_up_b, w_down_b, 0)

    h = _pool_mlp(h, norm_g[1], b_w_grp[0].astype(BF16), row(b_scale[0]), w_up_b, w_down_b, 1)

    w_uq_n, w_uq_r, w_dkv_p = _mla_weights(c_w_uq[0], c_w_dkv[0])
    inv_freq = ROPE_THETA ** (-jnp.arange(0, QK_ROPE, 2, dtype=F32) / QK_ROPE)
    invf = jnp.concatenate([inv_freq, inv_freq, jnp.zeros((LANES - QK_ROPE,), F32)]).reshape(1, LANES)
    q, k, v = _mla_proj(h, positions.reshape(s, 1), norm_g[2], c_w_dq[0].astype(BF16), row(c_g_q[0]),
                        w_uq_n, w_uq_r, w_dkv_p, row(c_g_kv[0]), c_w_uk[0].astype(BF16),
                        c_w_uv[0].T.astype(BF16), invf)
    oa, ob = _attention(q, k, v)
    h = _out_proj_mlp(h, oa, ob, norm_g[2], c_w_o[0].astype(BF16), w_up_b, w_down_b, 2)

    tr = lambda a: jnp.swapaxes(a, 1, 2)
    ab_re, ab_im, bb_re, bb_im = _s5_discretise(d_lam_re[0], d_lam_im[0], d_log_dt[0],
                                                tr(d_b_re[0]), tr(d_b_im[0]))
    bblk, cblk = _s5_block_matrices(bb_re, bb_im, d_c_re[0], d_c_im[0])
    h = _s5(h, norm_g[3], row(ab_re), row(ab_im), bblk, cblk, row(d_skip[0]),
            d_w_glu_a[0].astype(BF16), d_w_glu_b[0].astype(BF16))
    h = _mlp(h, norm_g[3], w_up_b, w_down_b, 3)
    return h.reshape(bsz, s, d)
```

```python
import functools
import math

import jax
import jax.numpy as jnp
from jax import lax
from jax.experimental import pallas as pl
from jax.experimental.pallas import tpu as pltpu

F32 = jnp.float32
BF16 = jnp.bfloat16

D_MODEL = 1024
D_FF = 4 * D_MODEL
RMS_EPS = 1e-6
LANES = 128
SUBLANES = 8
VMEM_LIMIT = 56 * 1024 * 1024

CHUNK = 128
GMLP_HEADS = 8
POOL_WINDOWS = (2, 4, 8, 16)
POOL_GROUP = D_MODEL // len(POOL_WINDOWS)
POOL_HALO = 32
assert POOL_WINDOWS == tuple(2 << k for k in range(len(POOL_WINDOWS)))
assert POOL_HALO == SUBLANES * len(POOL_WINDOWS)
MLA_HEADS = 8
Q_LORA = 384
KV_LORA = 256
QK_NOPE = 128
QK_ROPE = 64
V_HEAD = 128
ROPE_THETA = 10000.0
QK_PAD = 256
S5_GROUP_CH = 16
S5_GROUPS = D_MODEL // S5_GROUP_CH
S5_STATE = 64
S5_BLOCK_GROUPS = LANES // S5_GROUP_CH
S5_BLOCKS = S5_GROUPS // S5_BLOCK_GROUPS
S5_BLOCK_STATE = S5_BLOCK_GROUPS * S5_STATE


def _rms(x, g):
    return x * lax.rsqrt(jnp.mean(x * x, axis=-1, keepdims=True) + RMS_EPS) * g


def _gelu(x):
    c = math.sqrt(2.0 / math.pi)
    return 0.5 * x * (1.0 + jnp.tanh(c * (x + 0.044715 * (x * x * x))))


def _dot(a, b):
    return jnp.dot(a, b, preferred_element_type=F32)


def _const_spec(shape):
    nd = len(shape)
    return pl.BlockSpec(shape, lambda *_: (0,) * nd, pipeline_mode=pl.Buffered(1))


def _row_spec(tm, width):
    return pl.BlockSpec((tm, width), lambda i: (i, 0))


def _params(*sem, flags=None):
    return pltpu.CompilerParams(dimension_semantics=sem, vmem_limit_bytes=VMEM_LIMIT, flags=flags)


MLP_TF = 1024


def _mlp_tail(h, g_ref, wu_ref, wd_ref):
    z = _rms(h, g_ref[2:3, :]).astype(BF16)
    acc = jnp.zeros(h.shape, F32)
    for c in range(D_FF // MLP_TF):
        a = _dot(z, wu_ref[:, c * MLP_TF:(c + 1) * MLP_TF])
        a = jnp.square(jnp.maximum(a, 0.0)).astype(BF16)
        acc = acc + _dot(a, wd_ref[c * MLP_TF:(c + 1) * MLP_TF, :])
    return h + _rms(acc, g_ref[3:4, :])


def _mlp_weight_specs(w_up, w_down, layer):
    return [pl.BlockSpec((None,) + w.shape[1:], lambda *_: (layer, 0, 0), pipeline_mode=pl.Buffered(1))
            for w in (w_up, w_down)]


def _mlp_body(h_ref, g_ref, wu_ref, wd_ref, o_ref):
    o_ref[...] = _mlp_tail(h_ref[...], g_ref, wu_ref, wd_ref)


def _mlp(h, g, w_up, w_down, layer, *, tm=1024):
    s = h.shape[0]
    return pl.pallas_call(
        _mlp_body,
        grid=(s // tm,),
        in_specs=[_row_spec(tm, D_MODEL), _const_spec(g.shape)] + _mlp_weight_specs(w_up, w_down, layer),
        out_specs=_row_spec(tm, D_MODEL),
        out_shape=jax.ShapeDtypeStruct(h.shape, F32),
        compiler_params=_params("parallel"),
        name="mlp",
    )(h, g, w_up, w_down)


def _gmlp_body(h_ref, g_ref, win_ref, bin_ref, gv_ref, bv_ref, ws_ref, bs_ref, wout_ref, o_ref,
               sv_ref, *, tm):
    h = h_ref[...]
    z = _rms(h, g_ref[0:1, :]).astype(BF16)
    uv = _gelu(_dot(z, win_ref[...]) + bin_ref[...])
    u = uv[:, :D_MODEL]
    v = uv[:, D_MODEL:]
    mu = jnp.mean(v, axis=-1, keepdims=True)
    vc = v - mu
    v = vc * lax.rsqrt(jnp.mean(vc * vc, axis=-1, keepdims=True) + RMS_EPS)
    v = (v * gv_ref[...] + bv_ref[...]).astype(BF16)
    row = lax.broadcasted_iota(jnp.int32, (CHUNK, CHUNK), 0)
    col = lax.broadcasted_iota(jnp.int32, (CHUNK, CHUNK), 1)
    nchunk = tm // CHUNK
    for hd in range(GMLP_HEADS):
        ws = jnp.where(col <= row, ws_ref[hd], 0.0).astype(BF16)
        cols = slice(hd * LANES, (hd + 1) * LANES)
        rhs = jnp.concatenate([v[c * CHUNK:(c + 1) * CHUNK, cols] for c in range(nchunk)], axis=1)
        sv = _dot(ws, rhs)
        for c in range(nchunk):
            sv_ref[c * CHUNK:(c + 1) * CHUNK, cols] = (
                sv[:, c * LANES:(c + 1) * LANES] + bs_ref[:, cols])
    y = (u * sv_ref[...]).astype(BF16)
    o_ref[...] = h + _rms(_dot(y, wout_ref[...]), g_ref[1:2, :])


def _gmlp(h, g, w_in, b_in, g_v, b_v, w_s, b_s_full, w_out, *, tm=1024):
    s = h.shape[0]
    return pl.pallas_call(
        functools.partial(_gmlp_body, tm=tm),
        grid=(s // tm,),
        in_specs=[_row_spec(tm, D_MODEL), _const_spec(g.shape), _const_spec(w_in.shape),
                  _const_spec(b_in.shape), _const_spec(g_v.shape), _const_spec(b_v.shape),
                  _const_spec(w_s.shape), _const_spec(b_s_full.shape), _const_spec(w_out.shape)],
        out_specs=_row_spec(tm, D_MODEL),
        out_shape=jax.ShapeDtypeStruct(h.shape, F32),
        scratch_shapes=[pltpu.VMEM((tm, D_MODEL), F32)],
        compiler_params=_params("parallel"),
        name="gmlp",
    )(h, g, w_in, b_in, g_v, b_v, w_s, b_s_full, w_out)


def _pool_body(h_ref, halo_ref, g_ref, w_ref, scale_ref, wu_ref, wd_ref, o_ref, zz_ref, ws_ref,
               *, tm):
    i = pl.program_id(0)
    h = h_ref[...]
    z = _rms(h, g_ref[0:1, :])
    zh = _rms(halo_ref[...], g_ref[0:1, :])
    zz_ref[0:POOL_HALO, :] = jnp.where(i > 0, zh, 0.0)
    zz_ref[POOL_HALO:, :] = z
    nrows = tm + POOL_HALO
    for k in range(1, len(POOL_WINDOWS) + 1):
        d = 1 << (k - 1)
        lo = SUBLANES * k
        cols = slice((k - 1) * POOL_GROUP, D_MODEL)
        src = zz_ref if k == 1 else ws_ref
        ws_ref[lo:, cols] = src[lo:, cols] + src[lo - d:nrows - d, cols]
    pos = i * tm + lax.broadcasted_iota(jnp.int32, (tm, 1), 0)
    outs = []
    for gi, w in enumerate(POOL_WINDOWS):
        cols = slice(gi * POOL_GROUP, (gi + 1) * POOL_GROUP)
        xg = z[:, cols]
        win = ws_ref[POOL_HALO:, cols]
        count = jnp.minimum(pos + 1, w).astype(F32)
        diff = (win / count - xg).astype(BF16)
        outs.append(_dot(diff, w_ref[gi]))
    out = jnp.concatenate(outs, axis=-1) * scale_ref[...]
    o_ref[...] = _mlp_tail(h + _rms(out, g_ref[1:2, :]), g_ref, wu_ref, wd_ref)


def _pool_mlp(h, g, w_grp, scale, w_up, w_down, layer, *, tm=1024):
    s = h.shape[0]
    ratio = tm // POOL_HALO
    return pl.pallas_call(
        functools.partial(_pool_body, tm=tm),
        grid=(s // tm,),
        in_specs=[_row_spec(tm, D_MODEL),
                  pl.BlockSpec((POOL_HALO, D_MODEL), lambda i: (jnp.maximum(i * ratio - 1, 0), 0)),
                  _const_spec(g.shape), _const_spec(w_grp.shape), _const_spec(scale.shape)]
        + _mlp_weight_specs(w_up, w_down, layer),
        out_specs=_row_spec(tm, D_MODEL),
        out_shape=jax.ShapeDtypeStruct(h.shape, F32),
        scratch_shapes=[pltpu.VMEM((tm + POOL_HALO, D_MODEL), F32)] * 2,
        compiler_params=_params("parallel"),
        name="pool_mlp",
    )(h, h, g, w_grp, scale, w_up, w_down)


def _rope_tile(x, cos, sin_lo, sin_hi):
    n = x.shape[1] // LANES
    outs = []
    for t in range(n):
        xt = x[:, t * LANES:(t + 1) * LANES]
        up = pltpu.roll(xt, 32, axis=1)
        down = pltpu.roll(xt, LANES - 32, axis=1)
        outs.append(xt * cos + down * sin_lo + up * sin_hi)
    return outs


def _mla_proj_body(h_ref, pos_ref, g_ref, wdq_ref, gq_ref, wuqn_ref, wuqr_ref, wdkv_ref, gkv_ref,
                   wuk_ref, wuv_ref, invf_ref, q_ref, k_ref, v_ref, *, scale):
    h = h_ref[...]
    z = _rms(h, g_ref[0:1, :]).astype(BF16)
    ang = pos_ref[...].astype(F32) * invf_ref[...]
    lane = lax.broadcasted_iota(jnp.int32, ang.shape, 1)
    cos = jnp.where(lane < QK_ROPE, jnp.cos(ang), 0.0)
    sin = jnp.sin(ang)
    sin_lo = jnp.where(lane < QK_ROPE // 2, -sin, 0.0)
    sin_hi = jnp.where((lane >= QK_ROPE // 2) & (lane < QK_ROPE), sin, 0.0)

    ql = _rms(_dot(z, wdq_ref[...]), gq_ref[...]).astype(BF16)
    qn = _dot(ql, wuqn_ref[...]) * scale
    qr = _rope_tile(_dot(ql, wuqr_ref[...]) * scale, cos, sin_lo, sin_hi)
    ckv = _dot(z, wdkv_ref[...])
    c = _rms(ckv[:, :KV_LORA], gkv_ref[...]).astype(BF16)
    kr = _rope_tile(ckv[:, KV_LORA:], cos, sin_lo, sin_hi)[0].astype(BF16)
    kn = _dot(c, wuk_ref[...])
    for hd in range(MLA_HEADS):
        cols = slice(hd * LANES, (hd + 1) * LANES)
        q_ref[:, hd * QK_PAD:hd * QK_PAD + LANES] = qn[:, cols].astype(BF16)
        q_ref[:, hd * QK_PAD + LANES:(hd + 1) * QK_PAD] = qr[hd].astype(BF16)
        k_ref[:, hd * QK_PAD:hd * QK_PAD + LANES] = kn[:, cols].astype(BF16)
        k_ref[:, hd * QK_PAD + LANES:(hd + 1) * QK_PAD] = kr
    v_ref[...] = lax.dot_general(wuv_ref[...], c, (((1,), (1,)), ((), ())),
                                 preferred_element_type=F32).astype(BF16)


def _mla_proj(h, pos, g, w_dq, g_q, w_uq_n, w_uq_r, w_dkv, g_kv, w_uk, w_uv, invf, *, tm=1024):
    s = h.shape[0]
    scale = (QK_NOPE + QK_ROPE) ** -0.5 * math.log2(math.e)
    consts = (g, w_dq, g_q, w_uq_n, w_uq_r, w_dkv, g_kv, w_uk, w_uv, invf)
    return pl.pallas_call(
        functools.partial(_mla_proj_body, scale=scale),
        grid=(s // tm,),
        in_specs=[_row_spec(tm, D_MODEL), _row_spec(tm, 1)] + [_const_spec(a.shape) for a in consts],
        out_specs=[_row_spec(tm, MLA_HEADS * QK_PAD), _row_spec(tm, MLA_HEADS * QK_PAD),
                   pl.BlockSpec((MLA_HEADS * V_HEAD, tm), lambda i: (0, i))],
        out_shape=[jax.ShapeDtypeStruct((s, MLA_HEADS * QK_PAD), BF16),
                   jax.ShapeDtypeStruct((s, MLA_HEADS * QK_PAD), BF16),
                   jax.ShapeDtypeStruct((MLA_HEADS * V_HEAD, s), BF16)],
        compiler_params=_params("parallel"),
        name="mla_proj",
    )(h, pos, *consts)


ATTN_SLOTS = 4
ATTN_UNROLL = 16
ATTN_MAX_JUMP = 64.0


def _attn_body(qa_ref, qb_ref, k_ref, vt_ref, oa_ref, ob_ref, qt_ref, m_ref, acc_ref, l_ref, lag_ref,
               rl_ref, jump_ref, low_ref, s_ref, *slot_refs, t, nt):
    qa = pl.program_id(1)
    qb = nt - 1 - qa
    ns = ATTN_SLOTS
    p_refs, a_refs = slot_refs[:ns], slot_refs[ns:]
    neg = -1e30
    rows = 2 * SUBLANES
    last = nt

    qt_ref[0] = qa_ref[...].astype(F32).T.astype(BF16)
    qt_ref[1] = qb_ref[...].astype(F32).T.astype(BF16)
    for state in (m_ref, acc_ref, l_ref, lag_ref, rl_ref, jump_ref, low_ref):
        state[...] = jnp.zeros(state.shape, F32)

    def owner(j):
        return jnp.where(j < 2, j, (j >= qa + 2).astype(jnp.int32))

    def key_tile(j):
        full = jnp.where(j >= qa + 2, j - 2 - qa, j - 2)
        return jnp.where(j == 0, qa, jnp.where(j == 1, qb, full))

    def scores(j, masked):
        start = pl.multiple_of(key_tile(j) * t, t)
        s = _dot(k_ref[pl.ds(start, t), :], qt_ref[owner(j)])
        if masked:
            kpos = lax.broadcasted_iota(jnp.int32, (t, t), 0)
            qpos = lax.broadcasted_iota(jnp.int32, (t, t), 1)
            s = jnp.where(kpos <= qpos, s, neg)
        return s

    def fused(j, slot, masked=False):
        own = owner(j)
        ref = lag_ref[own]
        alpha = jnp.exp2(rl_ref[own] - ref)
        a_refs[slot][...] = alpha
        rl_ref[own] = ref
        s = scores(j, masked)
        rb = jnp.broadcast_to(ref, (SUBLANES, t))
        p_ref = p_refs[slot]
        mx = [None, None]
        ps = [None, None]
        for r in range(t // rows):
            lo = s[r * rows:r * rows + SUBLANES, :]
            hi = s[r * rows + SUBLANES:(r + 1) * rows, :]
            mx[0] = lo if mx[0] is None else jnp.maximum(mx[0], lo)
            mx[1] = hi if mx[1] is None else jnp.maximum(mx[1], hi)
            plo = jnp.exp2(lo - rb)
            phi = jnp.exp2(hi - rb)
            ps[0] = plo if ps[0] is None else ps[0] + plo
            ps[1] = phi if ps[1] is None else ps[1] + phi
            p_ref[r * rows:(r + 1) * rows, :] = jnp.concatenate([plo, phi], axis=0).astype(BF16)
        l_ref[own] = l_ref[own] * alpha + jnp.sum(ps[0] + ps[1], axis=0, keepdims=True)
        m_tile = jnp.max(jnp.maximum(mx[0], mx[1]), axis=0, keepdims=True)
        m_old = m_ref[own]
        lag_ref[own] = m_old
        m_ref[own] = jnp.maximum(m_old, m_tile)
        jump_ref[...] = jnp.maximum(jump_ref[...], m_tile - ref)
        if masked:
            low_ref[...] = jnp.minimum(low_ref[...], m_tile - ref)

    def pv(j, slot):
        start = pl.multiple_of(key_tile(j) * t, t)
        own = owner(j)
        acc_ref[own] = acc_ref[own] * a_refs[slot][...] + _dot(vt_ref[:, pl.ds(start, t)],
                                                                p_refs[slot][...])

    def stage(k, slot):
        fused(k + 2, (slot + 2) % ns)
        pv(k, slot)

    fused(0, 0, masked=True)
    fused(1, 1, masked=True)
    nstage = nt - 1

    def body(jq, _):
        for u in range(ATTN_UNROLL):
            stage(ATTN_UNROLL * jq + u, u % ns)
        return 0

    lax.fori_loop(0, nstage // ATTN_UNROLL, body, 0)
    for k in range(nstage - nstage % ATTN_UNROLL, nstage):
        stage(k, k % ns)
    pv(last - 1, (last - 1) % ns)
    pv(last, last % ns)

    def finish():
        for own, o_ref in ((0, oa_ref), (1, ob_ref)):
            o_t = acc_ref[own] * (1.0 / l_ref[own])
            o_ref[...] = o_t.T.astype(o_ref.dtype)

    finish()

    @pl.when((jnp.max(jump_ref[...]) > ATTN_MAX_JUMP) | (jnp.min(low_ref[...]) < -ATTN_MAX_JUMP))
    def _():
        m_ref[...] = jnp.full(m_ref.shape, neg, F32)
        acc_ref[...] = jnp.zeros(acc_ref.shape, F32)
        l_ref[...] = jnp.zeros(l_ref.shape, F32)

        def exact(j, masked):
            own = owner(j)
            s_ref[...] = scores(j, masked)
            mx = s_ref[0:SUBLANES, :]
            for r in range(1, t // SUBLANES):
                mx = jnp.maximum(mx, s_ref[r * SUBLANES:(r + 1) * SUBLANES, :])
            m_old = m_ref[own]
            m_new = jnp.maximum(m_old, jnp.max(mx, axis=0, keepdims=True))
            m_ref[own] = m_new
            alpha = jnp.exp2(m_old - m_new)
            a_refs[0][...] = alpha
            mb = jnp.broadcast_to(m_new, (rows, t))
            psum = jnp.zeros((rows, t), F32)
            for r in range(t // rows):
                p = jnp.exp2(s_ref[r * rows:(r + 1) * rows, :] - mb)
                psum = psum + p
                p_refs[0][r * rows:(r + 1) * rows, :] = p.astype(BF16)
            l_ref[own] = l_ref[own] * alpha + jnp.sum(psum, axis=0, keepdims=True)
            pv(j, 0)

        exact(0, True)
        exact(1, True)
        lax.fori_loop(2, last + 1, lambda j, c: (exact(j, False), c)[1], 0)
        finish()


def _attention(q, k, vt, *, t=512):
    s = q.shape[0]
    nt = s // t
    assert nt % 2 == 0
    half = nt // 2
    out = jax.ShapeDtypeStruct((s // 2, MLA_HEADS * V_HEAD), BF16)
    vec = pltpu.VMEM((2, 1, t), F32)
    return pl.pallas_call(
        functools.partial(_attn_body, t=t, nt=nt),
        grid=(MLA_HEADS, half),
        in_specs=[pl.BlockSpec((t, QK_PAD), lambda hd, i: (i, hd)),
                  pl.BlockSpec((t, QK_PAD), lambda hd, i: (nt - 1 - i, hd)),
                  pl.BlockSpec((s, QK_PAD), lambda hd, i: (0, hd)),
                  pl.BlockSpec((V_HEAD, s), lambda hd, i: (hd, 0))],
        out_specs=[pl.BlockSpec((t, V_HEAD), lambda hd, i: (i, hd)),
                   pl.BlockSpec((t, V_HEAD), lambda hd, i: (half - 1 - i, hd))],
        out_shape=[out, out],
        scratch_shapes=[
            pltpu.VMEM((2, QK_PAD, t), BF16),
            vec,
            pltpu.VMEM((2, V_HEAD, t), F32),
            vec,
            vec,
            vec,
            pltpu.VMEM((1, t), F32),
            pltpu.VMEM((1, t), F32),
            pltpu.VMEM((t, t), F32),
        ] + [pltpu.VMEM((t, t), BF16)] * ATTN_SLOTS
          + [pltpu.VMEM((1, t), F32)] * ATTN_SLOTS,
        compiler_params=_params("parallel", "parallel"),
        name="mla_attn",
    )(q, q, k, vt)


def _out_proj_body(h_ref, oa_ref, ob_ref, g_ref, wo_ref, wu_ref, wd_ref, o_ref, *, half):
    o = jnp.where(pl.program_id(0) < half, oa_ref[...], ob_ref[...])
    h = h_ref[...] + _rms(_dot(o, wo_ref[...]), g_ref[1:2, :])
    o_ref[...] = _mlp_tail(h, g_ref, wu_ref, wd_ref)


def _out_proj_mlp(h, oa, ob, g, w_o, w_up, w_down, layer, *, tm=1024):
    s = h.shape[0]
    half = oa.shape[0] // tm
    return pl.pallas_call(
        functools.partial(_out_proj_body, half=half),
        grid=(s // tm,),
        in_specs=[_row_spec(tm, D_MODEL),
                  pl.BlockSpec((tm, D_MODEL), lambda i: (jnp.minimum(i, half - 1), 0)),
                  pl.BlockSpec((tm, D_MODEL), lambda i: (jnp.maximum(i - half, 0), 0)),
                  _const_spec(g.shape), _const_spec(w_o.shape)]
        + _mlp_weight_specs(w_up, w_down, layer),
        out_specs=_row_spec(tm, D_MODEL),
        out_shape=jax.ShapeDtypeStruct(h.shape, F32),
        compiler_params=_params("parallel"),
        name="mla_out_mlp",
    )(h, oa, ob, g, w_o, w_up, w_down)


def _s5_disc_body(lr_ref, li_ref, ldt_ref, br_ref, bi_ref, abr_ref, abi_ref, bbr_ref, bbi_ref):
    dt = jnp.exp(ldt_ref[...])
    lr = lr_ref[...]
    li = li_ref[...]
    mag = jnp.exp(lr * dt)
    ab_re = mag * jnp.cos(li * dt)
    ab_im = mag * jnp.sin(li * dt)
    den = lr * lr + li * li
    f_re = ((ab_re - 1.0) * lr + ab_im * li) / den
    f_im = (ab_im * lr - (ab_re - 1.0) * li) / den
    abr_ref[...] = ab_re
    abi_ref[...] = ab_im
    br = br_ref[...]
    bi = bi_ref[...]
    bbr_ref[...] = f_re[:, None, :] * br - f_im[:, None, :] * bi
    bbi_ref[...] = f_re[:, None, :] * bi + f_im[:, None, :] * br


def _s5_discretise(lam_re, lam_im, log_dt, b_re_t, b_im_t):
    g, p = lam_re.shape
    c = b_re_t.shape[1]
    return pl.pallas_call(
        _s5_disc_body,
        out_shape=[jax.ShapeDtypeStruct((g, p), F32), jax.ShapeDtypeStruct((g, p), F32),
                   jax.ShapeDtypeStruct((g, c, p), F32), jax.ShapeDtypeStruct((g, c, p), F32)],
        name="s5_disc",
    )(lam_re, lam_im, log_dt.reshape(g, 1), b_re_t, b_im_t)


def _s5_body(h_ref, g_ref, perm_ref, ar_ref, ai_ref, bblk_ref, cblk_ref, dskip_ref,
             wa_ref, wb_ref, o_ref, xr0_ref, xi0_ref, xr1_ref, xi1_ref, xc0_ref, xc1_ref,
             yp_ref, y_ref, pr_ref, pi_ref, cr_ref, ci_ref, *, tm):
    seg = tm // SUBLANES
    i0 = pl.program_id(0)
    nst = S5_BLOCK_STATE
    pack = 2 * SUBLANES

    @pl.when(i0 == 0)
    def _():
        cr_ref[...] = jnp.zeros(cr_ref.shape, F32)
        ci_ref[...] = jnp.zeros(ci_ref.shape, F32)
        ar = ar_ref[...]
        ai = ai_ref[...]
        bcast = lambda v: jnp.broadcast_to(v, (SUBLANES, v.shape[1]))
        pr_ref[0:SUBLANES, :] = bcast(ar)
        pi_ref[0:SUBLANES, :] = bcast(ai)

        def pw(j, carry):
            r, im = carry
            r, im = r * ar - im * ai, r * ai + im * ar
            rows = pl.ds(pl.multiple_of(j * SUBLANES, SUBLANES), SUBLANES)
            pr_ref[rows, :] = bcast(r)
            pi_ref[rows, :] = bcast(im)
            return r, im

        lax.fori_loop(1, seg, pw, (ar, ai))

    h = h_ref[...]
    z = _rms(h, g_ref[0:1, :])
    zp = _dot(perm_ref[...], z.astype(BF16)).astype(BF16)

    for cb in range(S5_BLOCKS):
        st = slice(cb * nst, (cb + 1) * nst)
        xr_ref, xi_ref, xc_ref = ((xr0_ref, xi0_ref, xc0_ref), (xr1_ref, xi1_ref, xc1_ref))[cb % 2]
        bu = _dot(zp[:, cb * LANES:(cb + 1) * LANES], bblk_ref[cb])
        ar = jnp.broadcast_to(ar_ref[:, st], (SUBLANES, nst))
        ai = jnp.broadcast_to(ai_ref[:, st], (SUBLANES, nst))
        xr = jnp.zeros((SUBLANES, nst), F32)
        xi = xr
        for i in range(seg):
            rows = slice(i * SUBLANES, (i + 1) * SUBLANES)
            xr, xi = (ar * xr - ai * xi + bu[rows, :nst], ar * xi + ai * xr + bu[rows, nst:])
            xr_ref[rows, :] = xr
            xi_ref[rows, :] = xi
        alr = pr_ref[tm - 1:tm, st]
        ali = pi_ref[tm - 1:tm, st]
        c_r = cr_ref[:, st]
        c_i = ci_ref[:, st]
        rows_r, rows_i = [], []
        for sgm in range(SUBLANES):
            rows_r.append(c_r)
            rows_i.append(c_i)
            e_r = xr[sgm:sgm + 1, :]
            e_i = xi[sgm:sgm + 1, :]
            c_r, c_i = alr * c_r - ali * c_i + e_r, alr * c_i + ali * c_r + e_i
        cr_ref[:, st] = c_r
        ci_ref[:, st] = c_i
        cin_r = jnp.concatenate(rows_r, axis=0)
        cin_i = jnp.concatenate(rows_i, axis=0)
        for i2 in range(seg // 2):
            fr, fi = [], []
            for i in (2 * i2, 2 * i2 + 1):
                rows = slice(i * SUBLANES, (i + 1) * SUBLANES)
                p_r = pr_ref[rows, st]
                p_i = pi_ref[rows, st]
                fr.append(xr_ref[rows, :] + (p_r * cin_r - p_i * cin_i))
                fi.append(xi_ref[rows, :] + (p_r * cin_i + p_i * cin_r))
            rows2 = slice(i2 * pack, (i2 + 1) * pack)
            xc_ref[rows2, 0:nst] = jnp.concatenate(fr, axis=0).astype(BF16)
            xc_ref[rows2, nst:2 * nst] = jnp.concatenate(fi, axis=0).astype(BF16)
        yp_ref[cb] = _dot(xc_ref[...], cblk_ref[cb])

    for sgm in range(SUBLANES):
        for j in range(seg // SUBLANES):
            dst = slice(sgm * seg + j * SUBLANES, sgm * seg + (j + 1) * SUBLANES)
            src = pl.ds(j * SUBLANES * SUBLANES + sgm, SUBLANES, stride=SUBLANES)
            for cb in range(S5_BLOCKS):
                y_ref[dst, cb * LANES:(cb + 1) * LANES] = yp_ref[cb, src, :]
    y = _gelu(y_ref[...] + dskip_ref[...] * z).astype(BF16)
    out = _dot(y, wa_ref[...]) * jax.nn.sigmoid(_dot(y, wb_ref[...]))
    o_ref[...] = h + _rms(out, g_ref[1:2, :])


def _s5_permutation(tm):
    seg = tm // SUBLANES
    r = jnp.arange(tm)
    src = (r % SUBLANES) * seg + r // SUBLANES
    return (src[:, None] == jnp.arange(tm)[None, :]).astype(BF16)


def _s5(h, g, ab_re, ab_im, bblk, cblk, dskip, w_a, w_b, *, tm=512):
    s = h.shape[0]
    seg = tm // SUBLANES
    nstate = S5_GROUPS * S5_STATE
    perm = _s5_permutation(tm)
    consts = (g, perm, ab_re, ab_im, bblk, cblk, dskip, w_a, w_b)
    state = pltpu.VMEM((tm, S5_BLOCK_STATE), F32)
    return pl.pallas_call(
        functools.partial(_s5_body, tm=tm),
        grid=(s // tm,),
        in_specs=[_row_spec(tm, D_MODEL)] + [_const_spec(a.shape) for a in consts],
        out_specs=_row_spec(tm, D_MODEL),
        out_shape=jax.ShapeDtypeStruct(h.shape, F32),
        scratch_shapes=[
            state, state, state, state,
            pltpu.VMEM((tm, 2 * S5_BLOCK_STATE), BF16),
            pltpu.VMEM((tm, 2 * S5_BLOCK_STATE), BF16),
            pltpu.VMEM((S5_BLOCKS, tm, LANES), F32),
            pltpu.VMEM((tm, D_MODEL), F32),
            pltpu.VMEM((tm, nstate), F32),
            pltpu.VMEM((tm, nstate), F32),
            pltpu.VMEM((1, nstate), F32),
            pltpu.VMEM((1, nstate), F32),
        ],
        compiler_params=_params("arbitrary"),
        name="s5",
    )(h, *consts)


def _s5_block_matrices(bb_re, bb_im, c_re, c_im):
    nb, gb, c, p = S5_BLOCKS, S5_BLOCK_GROUPS, S5_GROUP_CH, S5_STATE
    eye = jnp.eye(gb, dtype=F32)

    def in_map(bb):
        t = bb.reshape(nb, gb, c, p)
        return jnp.einsum('bgcp,gk->bgckp', t, eye).reshape(nb, gb * c, gb * p)

    def out_map(cc):
        t = cc.reshape(nb, gb, c, p)
        return jnp.einsum('bgcp,gk->bgpkc', t, eye).reshape(nb, gb * p, gb * c)

    bblk = jnp.concatenate([in_map(bb_re), in_map(bb_im)], axis=2).astype(BF16)
    cblk = jnp.concatenate([out_map(c_re), out_map(-c_im)], axis=1).astype(BF16)
    return bblk, cblk


def _mla_weights(w_uq, w_dkv):
    wq = w_uq.reshape(Q_LORA, MLA_HEADS, QK_NOPE + QK_ROPE)
    w_n = wq[:, :, :QK_NOPE].reshape(Q_LORA, MLA_HEADS * QK_NOPE)
    w_r = jnp.pad(wq[:, :, QK_NOPE:], ((0, 0), (0, 0), (0, LANES - QK_ROPE)))
    w_r = w_r.reshape(Q_LORA, MLA_HEADS * LANES)
    w_dkv_p = jnp.pad(w_dkv, ((0, 0), (0, LANES - QK_ROPE)))
    return w_n.astype(BF16), w_r.astype(BF16), w_dkv_p.astype(BF16)


def kernel(x, positions, norm_g, w_up, w_down, a_w_in, a_b_in, a_g_v, a_b_v, a_w_s, a_b_s, a_w_out, b_w_grp, b_scale, c_w_dq, c_g_q, c_w_uq, c_w_dkv, c_g_kv, c_w_uk, c_w_uv, c_w_o, d_lam_re, d_lam_im, d_log_dt, d_b_re, d_b_im, d_c_re, d_c_im, d_skip, d_w_glu_a, d_w_glu_b):
    bsz, s, d = x.shape
    assert bsz == 1 and d == D_MODEL
    h = x.reshape(s, d)
    row = lambda a: a.reshape(1, -1)
    w_up_b = w_up.astype(BF16)
    w_down_b = w_down.astype(BF16)

    b_s_full = jnp.repeat(a_b_s[0].T, D_MODEL // GMLP_HEADS, axis=1)
    h = _gmlp(h, norm_g[0], a_w_in[0].astype(BF16), row(a_b_in[0]), row(a_g_v[0]), row(a_b_v[0]),
              a_w_s[0], b_s_full, a_w_out[0].astype(BF16))
    h = _mlp(h, norm_g[0], w_up_b, w_down_b, 0)

    h = _pool_mlp(h, norm_g[1], b_w_grp[0].astype(BF16), row(b_scale[0]), w_up_b, w_down_b, 1)

    w_uq_n, w_uq_r, w_dkv_p = _mla_weights(c_w_uq[0], c_w_dkv[0])
    inv_freq = ROPE_THETA ** (-jnp.arange(0, QK_ROPE, 2, dtype=F32) / QK_ROPE)
    invf = jnp.concatenate([inv_freq, inv_freq, jnp.zeros((LANES - QK_ROPE,), F32)]).reshape(1, LANES)
    q, k, v = _mla_proj(h, positions.reshape(s, 1), norm_g[2], c_w_dq[0].astype(BF16), row(c_g_q[0]),
                        w_uq_n, w_uq_r, w_dkv_p, row(c_g_kv[0]), c_w_uk[0].astype(BF16),
                        c_w_uv[0].T.astype(BF16), invf)
    oa, ob = _attention(q, k, v)
    h = _out_proj_mlp(h, oa, ob, norm_g[2], c_w_o[0].astype(BF16), w_up_b, w_down_b, 2)

    tr = lambda a: jnp.swapaxes(a, 1, 2)
    ab_re, ab_im, bb_re, bb_im = _s5_discretise(d_lam_re[0], d_lam_im[0], d_log_dt[0],
                                                tr(d_b_re[0]), tr(d_b_im[0]))
    bblk, cblk = _s5_block_matrices(bb_re, bb_im, d_c_re[0], d_c_im[0])
    h = _s5(h, norm_g[3], row(ab_re), row(ab_im), bblk, cblk, row(d_skip[0]),
            d_w_glu_a[0].astype(BF16), d_w_glu_b[0].astype(BF16))
    h = _mlp(h, norm_g[3], w_up_b, w_down_b, 3)
    return h.reshape(bsz, s, d)
```

```python
import functools
import math

import jax
import jax.numpy as jnp
from jax import lax
from jax.experimental import pallas as pl
from jax.experimental.pallas import tpu as pltpu

F32 = jnp.float32
BF16 = jnp.bfloat16

D_MODEL = 1024
D_FF = 4 * D_MODEL
RMS_EPS = 1e-6
LANES = 128
SUBLANES = 8
VMEM_LIMIT = 56 * 1024 * 1024

CHUNK = 128
GMLP_HEADS = 8
POOL_WINDOWS = (2, 4, 8, 16)
POOL_GROUP = D_MODEL // len(POOL_WINDOWS)
POOL_HALO = 32
assert POOL_WINDOWS == tuple(2 << k for k in range(len(POOL_WINDOWS)))
assert POOL_HALO == SUBLANES * len(POOL_WINDOWS)
MLA_HEADS = 8
Q_LORA = 384
KV_LORA = 256
QK_NOPE = 128
QK_ROPE = 64
V_HEAD = 128
ROPE_THETA = 10000.0
QK_PAD = 256
S5_GROUP_CH = 16
S5_GROUPS = D_MODEL // S5_GROUP_CH
S5_STATE = 64
S5_BLOCK_GROUPS = LANES // S5_GROUP_CH
S5_BLOCKS = S5_GROUPS // S5_BLOCK_GROUPS
S5_BLOCK_STATE = S5_BLOCK_GROUPS * S5_STATE


def _rms(x, g):
    return x * lax.rsqrt(jnp.mean(x * x, axis=-1, keepdims=True) + RMS_EPS) * g


def _gelu(x):
    c = math.sqrt(2.0 / math.pi)
    return 0.5 * x * (1.0 + jnp.tanh(c * (x + 0.044715 * (x * x * x))))


def _dot(a, b):
    return jnp.dot(a, b, preferred_element_type=F32)


def _const_spec(shape):
    nd = len(shape)
    return pl.BlockSpec(shape, lambda *_: (0,) * nd, pipeline_mode=pl.Buffered(1))


def _row_spec(tm, width):
    return pl.BlockSpec((tm, width), lambda i: (i, 0))


def _params(*sem):
    return pltpu.CompilerParams(dimension_semantics=sem, vmem_limit_bytes=VMEM_LIMIT)


MLP_TF = 1024


def _mlp_tail(h, g_ref, wu_ref, wd_ref):
    z = _rms(h, g_ref[2:3, :]).astype(BF16)
    acc = jnp.zeros(h.shape, F32)
    for c in range(D_FF // MLP_TF):
        a = _dot(z, wu_ref[:, c * MLP_TF:(c + 1) * MLP_TF])
        a = jnp.square(jnp.maximum(a, 0.0)).astype(BF16)
        acc = acc + _dot(a, wd_ref[c * MLP_TF:(c + 1) * MLP_TF, :])
    return h + _rms(acc, g_ref[3:4, :])


def _mlp_weight_specs(w_up, w_down, layer):
    return [pl.BlockSpec((None,) + w.shape[1:], lambda *_: (layer, 0, 0), pipeline_mode=pl.Buffered(1))
            for w in (w_up, w_down)]


def _mlp_body(h_ref, g_ref, wu_ref, wd_ref, o_ref):
    o_ref[...] = _mlp_tail(h_ref[...], g_ref, wu_ref, wd_ref)


def _mlp(h, g, w_up, w_down, layer, *, tm=1024):
    s = h.shape[0]
    return pl.pallas_call(
        _mlp_body,
        grid=(s // tm,),
        in_specs=[_row_spec(tm, D_MODEL), _const_spec(g.shape)] + _mlp_weight_specs(w_up, w_down, layer),
        out_specs=_row_spec(tm, D_MODEL),
        out_shape=jax.ShapeDtypeStruct(h.shape, F32),
        compiler_params=_params("parallel"),
        name="mlp",
    )(h, g, w_up, w_down)


def _gmlp_body(h_ref, g_ref, win_ref, bin_ref, gv_ref, bv_ref, ws_ref, bs_ref, wout_ref, o_ref,
               sv_ref, *, tm):
    h = h_ref[...]
    z = _rms(h, g_ref[0:1, :]).astype(BF16)
    uv = _gelu(_dot(z, win_ref[...]) + bin_ref[...])
    u = uv[:, :D_MODEL]
    v = uv[:, D_MODEL:]
    mu = jnp.mean(v, axis=-1, keepdims=True)
    vc = v - mu
    v = vc * lax.rsqrt(jnp.mean(vc * vc, axis=-1, keepdims=True) + RMS_EPS)
    v = (v * gv_ref[...] + bv_ref[...]).astype(BF16)
    row = lax.broadcasted_iota(jnp.int32, (CHUNK, CHUNK), 0)
    col = lax.broadcasted_iota(jnp.int32, (CHUNK, CHUNK), 1)
    nchunk = tm // CHUNK
    for hd in range(GMLP_HEADS):
        ws = jnp.where(col <= row, ws_ref[hd], 0.0).astype(BF16)
        cols = slice(hd * LANES, (hd + 1) * LANES)
        rhs = jnp.concatenate([v[c * CHUNK:(c + 1) * CHUNK, cols] for c in range(nchunk)], axis=1)
        sv = _dot(ws, rhs)
        for c in range(nchunk):
            sv_ref[c * CHUNK:(c + 1) * CHUNK, cols] = (
                sv[:, c * LANES:(c + 1) * LANES] + bs_ref[:, cols])
    y = (u * sv_ref[...]).astype(BF16)
    o_ref[...] = h + _rms(_dot(y, wout_ref[...]), g_ref[1:2, :])


def _gmlp(h, g, w_in, b_in, g_v, b_v, w_s, b_s_full, w_out, *, tm=1024):
    s = h.shape[0]
    return pl.pallas_call(
        functools.partial(_gmlp_body, tm=tm),
        grid=(s // tm,),
        in_specs=[_row_spec(tm, D_MODEL), _const_spec(g.shape), _const_spec(w_in.shape),
                  _const_spec(b_in.shape), _const_spec(g_v.shape), _const_spec(b_v.shape),
                  _const_spec(w_s.shape), _const_spec(b_s_full.shape), _const_spec(w_out.shape)],
        out_specs=_row_spec(tm, D_MODEL),
        out_shape=jax.ShapeDtypeStruct(h.shape, F32),
        scratch_shapes=[pltpu.VMEM((tm, D_MODEL), F32)],
        compiler_params=_params("parallel"),
        name="gmlp",
    )(h, g, w_in, b_in, g_v, b_v, w_s, b_s_full, w_out)


def _pool_body(h_ref, halo_ref, g_ref, w_ref, scale_ref, wu_ref, wd_ref, o_ref, zz_ref, ws_ref,
               *, tm):
    i = pl.program_id(0)
    h = h_ref[...]
    z = _rms(h, g_ref[0:1, :])
    zh = _rms(halo_ref[...], g_ref[0:1, :])
    zz_ref[0:POOL_HALO, :] = jnp.where(i > 0, zh, 0.0)
    zz_ref[POOL_HALO:, :] = z
    nrows = tm + POOL_HALO
    for k in range(1, len(POOL_WINDOWS) + 1):
        d = 1 << (k - 1)
        lo = SUBLANES * k
        cols = slice((k - 1) * POOL_GROUP, D_MODEL)
        src = zz_ref if k == 1 else ws_ref
        ws_ref[lo:, cols] = src[lo:, cols] + src[lo - d:nrows - d, cols]
    pos = i * tm + lax.broadcasted_iota(jnp.int32, (tm, 1), 0)
    outs = []
    for gi, w in enumerate(POOL_WINDOWS):
        cols = slice(gi * POOL_GROUP, (gi + 1) * POOL_GROUP)
        xg = z[:, cols]
        win = ws_ref[POOL_HALO:, cols]
        count = jnp.minimum(pos + 1, w).astype(F32)
        diff = (win / count - xg).astype(BF16)
        outs.append(_dot(diff, w_ref[gi]))
    out = jnp.concatenate(outs, axis=-1) * scale_ref[...]
    o_ref[...] = _mlp_tail(h + _rms(out, g_ref[1:2, :]), g_ref, wu_ref, wd_ref)


def _pool_mlp(h, g, w_grp, scale, w_up, w_down, layer, *, tm=512):
    s = h.shape[0]
    ratio = tm // POOL_HALO
    return pl.pallas_call(
        functools.partial(_pool_body, tm=tm),
        grid=(s // tm,),
        in_specs=[_row_spec(tm, D_MODEL),
                  pl.BlockSpec((POOL_HALO, D_MODEL), lambda i: (jnp.maximum(i * ratio - 1, 0), 0)),
                  _const_spec(g.shape), _const_spec(w_grp.shape), _const_spec(scale.shape)]
        + _mlp_weight_specs(w_up, w_down, layer),
        out_specs=_row_spec(tm, D_MODEL),
        out_shape=jax.ShapeDtypeStruct(h.shape, F32),
        scratch_shapes=[pltpu.VMEM((tm + POOL_HALO, D_MODEL), F32)] * 2,
        compiler_params=_params("parallel"),
        name="pool_mlp",
    )(h, h, g, w_grp, scale, w_up, w_down)


def _rope_tile(x, cos, sin_lo, sin_hi):
    n = x.shape[1] // LANES
    outs = []
    for t in range(n):
        xt = x[:, t * LANES:(t + 1) * LANES]
        up = pltpu.roll(xt, 32, axis=1)
        down = pltpu.roll(xt, LANES - 32, axis=1)
        outs.append(xt * cos + down * sin_lo + up * sin_hi)
    return outs


def _mla_proj_body(h_ref, pos_ref, g_ref, wdq_ref, gq_ref, wuqn_ref, wuqr_ref, wdkv_ref, gkv_ref,
                   wuk_ref, wuv_ref, invf_ref, q_ref, k_ref, v_ref, *, scale):
    h = h_ref[...]
    z = _rms(h, g_ref[0:1, :]).astype(BF16)
    ang = pos_ref[...].astype(F32) * invf_ref[...]
    lane = lax.broadcasted_iota(jnp.int32, ang.shape, 1)
    cos = jnp.where(lane < QK_ROPE, jnp.cos(ang), 0.0)
    sin = jnp.sin(ang)
    sin_lo = jnp.where(lane < QK_ROPE // 2, -sin, 0.0)
    sin_hi = jnp.where((lane >= QK_ROPE // 2) & (lane < QK_ROPE), sin, 0.0)

    ql = _rms(_dot(z, wdq_ref[...]), gq_ref[...]).astype(BF16)
    qn = _dot(ql, wuqn_ref[...]) * scale
    qr = _rope_tile(_dot(ql, wuqr_ref[...]) * scale, cos, sin_lo, sin_hi)
    ckv = _dot(z, wdkv_ref[...])
    c = _rms(ckv[:, :KV_LORA], gkv_ref[...]).astype(BF16)
    kr = _rope_tile(ckv[:, KV_LORA:], cos, sin_lo, sin_hi)[0].astype(BF16)
    kn = _dot(c, wuk_ref[...])
    for hd in range(MLA_HEADS):
        cols = slice(hd * LANES, (hd + 1) * LANES)
        q_ref[:, hd * QK_PAD:hd * QK_PAD + LANES] = qn[:, cols].astype(BF16)
        q_ref[:, hd * QK_PAD + LANES:(hd + 1) * QK_PAD] = qr[hd].astype(BF16)
        k_ref[:, hd * QK_PAD:hd * QK_PAD + LANES] = kn[:, cols].astype(BF16)
        k_ref[:, hd * QK_PAD + LANES:(hd + 1) * QK_PAD] = kr
    v_ref[...] = lax.dot_general(wuv_ref[...], c, (((1,), (1,)), ((), ())),
                                 preferred_element_type=F32).astype(BF16)


def _mla_proj(h, pos, g, w_dq, g_q, w_uq_n, w_uq_r, w_dkv, g_kv, w_uk, w_uv, invf, *, tm=1024):
    s = h.shape[0]
    scale = (QK_NOPE + QK_ROPE) ** -0.5 * math.log2(math.e)
    consts = (g, w_dq, g_q, w_uq_n, w_uq_r, w_dkv, g_kv, w_uk, w_uv, invf)
    return pl.pallas_call(
        functools.partial(_mla_proj_body, scale=scale),
        grid=(s // tm,),
        in_specs=[_row_spec(tm, D_MODEL), _row_spec(tm, 1)] + [_const_spec(a.shape) for a in consts],
        out_specs=[_row_spec(tm, MLA_HEADS * QK_PAD), _row_spec(tm, MLA_HEADS * QK_PAD),
                   pl.BlockSpec((MLA_HEADS * V_HEAD, tm), lambda i: (0, i))],
        out_shape=[jax.ShapeDtypeStruct((s, MLA_HEADS * QK_PAD), BF16),
                   jax.ShapeDtypeStruct((s, MLA_HEADS * QK_PAD), BF16),
                   jax.ShapeDtypeStruct((MLA_HEADS * V_HEAD, s), BF16)],
        compiler_params=_params("parallel"),
        name="mla_proj",
    )(h, pos, *consts)


ATTN_SLOTS = 4
ATTN_UNROLL = 16
ATTN_MAX_JUMP = 64.0


def _attn_body(qa_ref, qb_ref, k_ref, vt_ref, oa_ref, ob_ref, qt_ref, m_ref, acc_ref, l_ref, lag_ref,
               rl_ref, jump_ref, low_ref, s_ref, *slot_refs, t, nt):
    qa = pl.program_id(1)
    qb = nt - 1 - qa
    ns = ATTN_SLOTS
    p_refs, a_refs = slot_refs[:ns], slot_refs[ns:]
    neg = -1e30
    rows = 2 * SUBLANES
    last = nt

    qt_ref[0] = qa_ref[...].astype(F32).T.astype(BF16)
    qt_ref[1] = qb_ref[...].astype(F32).T.astype(BF16)
    for state in (m_ref, acc_ref, l_ref, lag_ref, rl_ref, jump_ref, low_ref):
        state[...] = jnp.zeros(state.shape, F32)

    def owner(j):
        return jnp.where(j < 2, j, (j >= qa + 2).astype(jnp.int32))

    def key_tile(j):
        full = jnp.where(j >= qa + 2, j - 2 - qa, j - 2)
        return jnp.where(j == 0, qa, jnp.where(j == 1, qb, full))

    def scores(j, masked):
        start = pl.multiple_of(key_tile(j) * t, t)
        s = _dot(k_ref[pl.ds(start, t), :], qt_ref[owner(j)])
        if masked:
            kpos = lax.broadcasted_iota(jnp.int32, (t, t), 0)
            qpos = lax.broadcasted_iota(jnp.int32, (t, t), 1)
            s = jnp.where(kpos <= qpos, s, neg)
        return s

    def fused(j, slot, masked=False):
        own = owner(j)
        ref = lag_ref[own]
        alpha = jnp.exp2(rl_ref[own] - ref)
        a_refs[slot][...] = alpha
        rl_ref[own] = ref
        s = scores(j, masked)
        rb = jnp.broadcast_to(ref, (SUBLANES, t))
        p_ref = p_refs[slot]
        mx = [None, None]
        ps = [None, None]
        for r in range(t // rows):
            lo = s[r * rows:r * rows + SUBLANES, :]
            hi = s[r * rows + SUBLANES:(r + 1) * rows, :]
            mx[0] = lo if mx[0] is None else jnp.maximum(mx[0], lo)
            mx[1] = hi if mx[1] is None else jnp.maximum(mx[1], hi)
            plo = jnp.exp2(lo - rb)
            phi = jnp.exp2(hi - rb)
            ps[0] = plo if ps[0] is None else ps[0] + plo
            ps[1] = phi if ps[1] is None else ps[1] + phi
            p_ref[r * rows:(r + 1) * rows, :] = jnp.concatenate([plo, phi], axis=0).astype(BF16)
        l_ref[own] = l_ref[own] * alpha + jnp.sum(ps[0] + ps[1], axis=0, keepdims=True)
        m_tile = jnp.max(jnp.maximum(mx[0], mx[1]), axis=0, keepdims=True)
        m_old = m_ref[own]
        lag_ref[own] = m_old
        m_ref[own] = jnp.maximum(m_old, m_tile)
        jump_ref[...] = jnp.maximum(jump_ref[...], m_tile - ref)
        if masked:
            low_ref[...] = jnp.minimum(low_ref[...], m_tile - ref)

    def pv(j, slot):
        start = pl.multiple_of(key_tile(j) * t, t)
        own = owner(j)
        acc_ref[own] = acc_ref[own] * a_refs[slot][...] + _dot(vt_ref[:, pl.ds(start, t)],
                                                                p_refs[slot][...])

    def stage(k, slot):
        fused(k + 2, (slot + 2) % ns)
        pv(k, slot)

    fused(0, 0, masked=True)
    fused(1, 1, masked=True)
    nstage = nt - 1

    def body(jq, _):
        for u in range(ATTN_UNROLL):
            stage(ATTN_UNROLL * jq + u, u % ns)
        return 0

    lax.fori_loop(0, nstage // ATTN_UNROLL, body, 0)
    for k in range(nstage - nstage % ATTN_UNROLL, nstage):
        stage(k, k % ns)
    pv(last - 1, (last - 1) % ns)
    pv(last, last % ns)

    def finish():
        for own, o_ref in ((0, oa_ref), (1, ob_ref)):
            o_t = acc_ref[own] * (1.0 / l_ref[own])
            o_ref[...] = o_t.T.astype(o_ref.dtype)

    finish()

    @pl.when((jnp.max(jump_ref[...]) > ATTN_MAX_JUMP) | (jnp.min(low_ref[...]) < -ATTN_MAX_JUMP))
    def _():
        m_ref[...] = jnp.full(m_ref.shape, neg, F32)
        acc_ref[...] = jnp.zeros(acc_ref.shape, F32)
        l_ref[...] = jnp.zeros(l_ref.shape, F32)

        def exact(j, masked):
            own = owner(j)
            s_ref[...] = scores(j, masked)
            mx = s_ref[0:SUBLANES, :]
            for r in range(1, t // SUBLANES):
                mx = jnp.maximum(mx, s_ref[r * SUBLANES:(r + 1) * SUBLANES, :])
            m_old = m_ref[own]
            m_new = jnp.maximum(m_old, jnp.max(mx, axis=0, keepdims=True))
            m_ref[own] = m_new
            alpha = jnp.exp2(m_old - m_new)
            a_refs[0][...] = alpha
            mb = jnp.broadcast_to(m_new, (rows, t))
            psum = jnp.zeros((rows, t), F32)
            for r in range(t // rows):
                p = jnp.exp2(s_ref[r * rows:(r + 1) * rows, :] - mb)
                psum = psum + p
                p_refs[0][r * rows:(r + 1) * rows, :] = p.astype(BF16)
            l_ref[own] = l_ref[own] * alpha + jnp.sum(psum, axis=0, keepdims=True)
            pv(j, 0)

        exact(0, True)
        exact(1, True)
        lax.fori_loop(2, last + 1, lambda j, c: (exact(j, False), c)[1], 0)
        finish()


def _attention(q, k, vt, *, t=512):
    s = q.shape[0]
    nt = s // t
    assert nt % 2 == 0
    half = nt // 2
    out = jax.ShapeDtypeStruct((s // 2, MLA_HEADS * V_HEAD), BF16)
    vec = pltpu.VMEM((2, 1, t), F32)
    return pl.pallas_call(
        functools.partial(_attn_body, t=t, nt=nt),
        grid=(MLA_HEADS, half),
        in_specs=[pl.BlockSpec((t, QK_PAD), lambda hd, i: (i, hd)),
                  pl.BlockSpec((t, QK_PAD), lambda hd, i: (nt - 1 - i, hd)),
                  pl.BlockSpec((s, QK_PAD), lambda hd, i: (0, hd)),
                  pl.BlockSpec((V_HEAD, s), lambda hd, i: (hd, 0))],
        out_specs=[pl.BlockSpec((t, V_HEAD), lambda hd, i: (i, hd)),
                   pl.BlockSpec((t, V_HEAD), lambda hd, i: (half - 1 - i, hd))],
        out_shape=[out, out],
        scratch_shapes=[
            pltpu.VMEM((2, QK_PAD, t), BF16),
            vec,
            pltpu.VMEM((2, V_HEAD, t), F32),
            vec,
            vec,
            vec,
            pltpu.VMEM((1, t), F32),
            pltpu.VMEM((1, t), F32),
            pltpu.VMEM((t, t), F32),
        ] + [pltpu.VMEM((t, t), BF16)] * ATTN_SLOTS
          + [pltpu.VMEM((1, t), F32)] * ATTN_SLOTS,
        compiler_params=_params("parallel", "parallel"),
        name="mla_attn",
    )(q, q, k, vt)


def _out_proj_body(h_ref, oa_ref, ob_ref, g_ref, wo_ref, wu_ref, wd_ref, o_ref, *, half):
    o = jnp.where(pl.program_id(0) < half, oa_ref[...], ob_ref[...])
    h = h_ref[...] + _rms(_dot(o, wo_ref[...]), g_ref[1:2, :])
    o_ref[...] = _mlp_tail(h, g_ref, wu_ref, wd_ref)


def _out_proj_mlp(h, oa, ob, g, w_o, w_up, w_down, layer, *, tm=1024):
    s = h.shape[0]
    half = oa.shape[0] // tm
    return pl.pallas_call(
        functools.partial(_out_proj_body, half=half),
        grid=(s // tm,),
        in_specs=[_row_spec(tm, D_MODEL),
                  pl.BlockSpec((tm, D_MODEL), lambda i: (jnp.minimum(i, half - 1), 0)),
                  pl.BlockSpec((tm, D_MODEL), lambda i: (jnp.maximum(i - half, 0), 0)),
                  _const_spec(g.shape), _const_spec(w_o.shape)]
        + _mlp_weight_specs(w_up, w_down, layer),
        out_specs=_row_spec(tm, D_MODEL),
        out_shape=jax.ShapeDtypeStruct(h.shape, F32),
        compiler_params=_params("parallel"),
        name="mla_out_mlp",
    )(h, oa, ob, g, w_o, w_up, w_down)


def _s5_disc_body(lr_ref, li_ref, ldt_ref, br_ref, bi_ref, abr_ref, abi_ref, bbr_ref, bbi_ref):
    dt = jnp.exp(ldt_ref[...])
    lr = lr_ref[...]
    li = li_ref[...]
    mag = jnp.exp(lr * dt)
    ab_re = mag * jnp.cos(li * dt)
    ab_im = mag * jnp.sin(li * dt)
    den = lr * lr + li * li
    f_re = ((ab_re - 1.0) * lr + ab_im * li) / den
    f_im = (ab_im * lr - (ab_re - 1.0) * li) / den
    abr_ref[...] = ab_re
    abi_ref[...] = ab_im
    br = br_ref[...]
    bi = bi_ref[...]
    bbr_ref[...] = f_re[:, None, :] * br - f_im[:, None, :] * bi
    bbi_ref[...] = f_re[:, None, :] * bi + f_im[:, None, :] * br


def _s5_discretise(lam_re, lam_im, log_dt, b_re_t, b_im_t):
    g, p = lam_re.shape
    c = b_re_t.shape[1]
    return pl.pallas_call(
        _s5_disc_body,
        out_shape=[jax.ShapeDtypeStruct((g, p), F32), jax.ShapeDtypeStruct((g, p), F32),
                   jax.ShapeDtypeStruct((g, c, p), F32), jax.ShapeDtypeStruct((g, c, p), F32)],
        name="s5_disc",
    )(lam_re, lam_im, log_dt.reshape(g, 1), b_re_t, b_im_t)


def _s5_body(h_ref, g_ref, perm_ref, ar_ref, ai_ref, bblk_ref, cblk_ref, dskip_ref,
             wa_ref, wb_ref, o_ref, xr0_ref, xi0_ref, xr1_ref, xi1_ref, xc0_ref, xc1_ref,
             yp_ref, y_ref, pr_ref, pi_ref, cr_ref, ci_ref, *, tm):
    seg = tm // SUBLANES
    i0 = pl.program_id(0)
    nst = S5_BLOCK_STATE
    pack = 2 * SUBLANES

    @pl.when(i0 == 0)
    def _():
        cr_ref[...] = jnp.zeros(cr_ref.shape, F32)
        ci_ref[...] = jnp.zeros(ci_ref.shape, F32)
        ar = ar_ref[...]
        ai = ai_ref[...]
        bcast = lambda v: jnp.broadcast_to(v, (SUBLANES, v.shape[1]))
        pr_ref[0:SUBLANES, :] = bcast(ar)
        pi_ref[0:SUBLANES, :] = bcast(ai)

        def pw(j, carry):
            r, im = carry
            r, im = r * ar - im * ai, r * ai + im * ar
            rows = pl.ds(pl.multiple_of(j * SUBLANES, SUBLANES), SUBLANES)
            pr_ref[rows, :] = bcast(r)
            pi_ref[rows, :] = bcast(im)
            return r, im

        lax.fori_loop(1, seg, pw, (ar, ai))

    h = h_ref[...]
    z = _rms(h, g_ref[0:1, :])
    zp = _dot(perm_ref[...], z.astype(BF16)).astype(BF16)

    for cb in range(S5_BLOCKS):
        st = slice(cb * nst, (cb + 1) * nst)
        xr_ref, xi_ref, xc_ref = ((xr0_ref, xi0_ref, xc0_ref), (xr1_ref, xi1_ref, xc1_ref))[cb % 2]
        bu = _dot(zp[:, cb * LANES:(cb + 1) * LANES], bblk_ref[cb])
        ar = jnp.broadcast_to(ar_ref[:, st], (SUBLANES, nst))
        ai = jnp.broadcast_to(ai_ref[:, st], (SUBLANES, nst))
        xr = jnp.zeros((SUBLANES, nst), F32)
        xi = xr
        for i in range(seg):
            rows = slice(i * SUBLANES, (i + 1) * SUBLANES)
            xr, xi = (ar * xr - ai * xi + bu[rows, :nst], ar * xi + ai * xr + bu[rows, nst:])
            xr_ref[rows, :] = xr
            xi_ref[rows, :] = xi
        alr = pr_ref[tm - 1:tm, st]
        ali = pi_ref[tm - 1:tm, st]
        c_r = cr_ref[:, st]
        c_i = ci_ref[:, st]
        rows_r, rows_i = [], []
        for sgm in range(SUBLANES):
            rows_r.append(c_r)
            rows_i.append(c_i)
            e_r = xr[sgm:sgm + 1, :]
            e_i = xi[sgm:sgm + 1, :]
            c_r, c_i = alr * c_r - ali * c_i + e_r, alr * c_i + ali * c_r + e_i
        cr_ref[:, st] = c_r
        ci_ref[:, st] = c_i
        cin_r = jnp.concatenate(rows_r, axis=0)
        cin_i = jnp.concatenate(rows_i, axis=0)
        for i2 in range(seg // 2):
            fr, fi = [], []
            for i in (2 * i2, 2 * i2 + 1):
                rows = slice(i * SUBLANES, (i + 1) * SUBLANES)
                p_r = pr_ref[rows, st]
                p_i = pi_ref[rows, st]
                fr.append(xr_ref[rows, :] + (p_r * cin_r - p_i * cin_i))
                fi.append(xi_ref[rows, :] + (p_r * cin_i + p_i * cin_r))
            rows2 = slice(i2 * pack, (i2 + 1) * pack)
            xc_ref[rows2, 0:nst] = jnp.concatenate(fr, axis=0).astype(BF16)
            xc_ref[rows2, nst:2 * nst] = jnp.concatenate(fi, axis=0).astype(BF16)
        yp_ref[cb] = _dot(xc_ref[...], cblk_ref[cb])

    for sgm in range(SUBLANES):
        for j in range(seg // SUBLANES):
            dst = slice(sgm * seg + j * SUBLANES, sgm * seg + (j + 1) * SUBLANES)
            src = pl.ds(j * SUBLANES * SUBLANES + sgm, SUBLANES, stride=SUBLANES)
            for cb in range(S5_BLOCKS):
                y_ref[dst, cb * LANES:(cb + 1) * LANES] = yp_ref[cb, src, :]
    y = _gelu(y_ref[...] + dskip_ref[...] * z).astype(BF16)
    out = _dot(y, wa_ref[...]) * jax.nn.sigmoid(_dot(y, wb_ref[...]))
    o_ref[...] = h + _rms(out, g_ref[1:2, :])


def _s5_permutation(tm):
    seg = tm // SUBLANES
    r = jnp.arange(tm)
    src = (r % SUBLANES) * seg + r // SUBLANES
    return (src[:, None] == jnp.arange(tm)[None, :]).astype(BF16)


def _s5(h, g, ab_re, ab_im, bblk, cblk, dskip, w_a, w_b, *, tm=512):
    s = h.shape[0]
    seg = tm // SUBLANES
    nstate = S5_GROUPS * S5_STATE
    perm = _s5_permutation(tm)
    consts = (g, perm, ab_re, ab_im, bblk, cblk, dskip, w_a, w_b)
    state = pltpu.VMEM((tm, S5_BLOCK_STATE), F32)
    return pl.pallas_call(
        functools.partial(_s5_body, tm=tm),
        grid=(s // tm,),
        in_specs=[_row_spec(tm, D_MODEL)] + [_const_spec(a.shape) for a in consts],
        out_specs=_row_spec(tm, D_MODEL),
        out_shape=jax.ShapeDtypeStruct(h.shape, F32),
        scratch_shapes=[
            state, state, state, state,
            pltpu.VMEM((tm, 2 * S5_BLOCK_STATE), BF16),
            pltpu.VMEM((tm, 2 * S5_BLOCK_STATE), BF16),
            pltpu.VMEM((S5_BLOCKS, tm, LANES), F32),
            pltpu.VMEM((tm, D_MODEL), F32),
            pltpu.VMEM((tm, nstate), F32),
            pltpu.VMEM((tm, nstate), F32),
            pltpu.VMEM((1, nstate), F32),
            pltpu.VMEM((1, nstate), F32),
        ],
        compiler_params=_params("arbitrary"),
        name="s5",
    )(h, *consts)


def _s5_block_matrices(bb_re, bb_im, c_re, c_im):
    nb, gb, c, p = S5_BLOCKS, S5_BLOCK_GROUPS, S5_GROUP_CH, S5_STATE
    eye = jnp.eye(gb, dtype=F32)

    def in_map(bb):
        t = bb.reshape(nb, gb, c, p)
        return jnp.einsum('bgcp,gk->bgckp', t, eye).reshape(nb, gb * c, gb * p)

    def out_map(cc):
        t = cc.reshape(nb, gb, c, p)
        return jnp.einsum('bgcp,gk->bgpkc', t, eye).reshape(nb, gb * p, gb * c)

    bblk = jnp.concatenate([in_map(bb_re), in_map(bb_im)], axis=2).astype(BF16)
    cblk = jnp.concatenate([out_map(c_re), out_map(-c_im)], axis=1).astype(BF16)
    return bblk, cblk


def _mla_weights(w_uq, w_dkv):
    wq = w_uq.reshape(Q_LORA, MLA_HEADS, QK_NOPE + QK_ROPE)
    w_n = wq[:, :, :QK_NOPE].reshape(Q_LORA, MLA_HEADS * QK_NOPE)
    w_r = jnp.pad(wq[:, :, QK_NOPE:], ((0, 0), (0, 0), (0, LANES - QK_ROPE)))
    w_r = w_r.reshape(Q_LORA, MLA_HEADS * LANES)
    w_dkv_p = jnp.pad(w_dkv, ((0, 0), (0, LANES - QK_ROPE)))
    return w_n.astype(BF16), w_r.astype(BF16), w_dkv_p.astype(BF16)


def kernel(x, positions, norm_g, w_up, w_down, a_w_in, a_b_in, a_g_v, a_b_v, a_w_s, a_b_s, a_w_out, b_w_grp, b_scale, c_w_dq, c_g_q, c_w_uq, c_w_dkv, c_g_kv, c_w_uk, c_w_uv, c_w_o, d_lam_re, d_lam_im, d_log_dt, d_b_re, d_b_im, d_c_re, d_c_im, d_skip, d_w_glu_a, d_w_glu_b):
    bsz, s, d = x.shape
    assert bsz == 1 and d == D_MODEL
    h = x.reshape(s, d)
    row = lambda a: a.reshape(1, -1)
    w_up_b = w_up.astype(BF16)
    w_down_b = w_down.astype(BF16)

    b_s_full = jnp.repeat(a_b_s[0].T, D_MODEL // GMLP_HEADS, axis=1)
    h = _gmlp(h, norm_g[0], a_w_in[0].astype(BF16), row(a_b_in[0]), row(a_g_v[0]), row(a_b_v[0]),
              a_w_s[0], b_s_full, a_w_out[0].astype(BF16))
    h = _mlp(h, norm_g[0], w_up_b, w_down_b, 0)

    h = _pool_mlp(h, norm_g[1], b_w_grp[0].astype(BF16), row(b_scale[0]), w_up_b, w_down_b, 1)

    w_uq_n, w_uq_r, w_dkv_p = _mla_weights(c_w_uq[0], c_w_dkv[0])
    inv_freq = ROPE_THETA ** (-jnp.arange(0, QK_ROPE, 2, dtype=F32) / QK_ROPE)
    invf = jnp.concatenate([inv_freq, inv_freq, jnp.zeros((LANES - QK_ROPE,), F32)]).reshape(1, LANES)
    q, k, v = _mla_proj(h, positions.reshape(s, 1), norm_g[2], c_w_dq[0].astype(BF16), row(c_g_q[0]),
                        w_uq_n, w_uq_r, w_dkv_p, row(c_g_kv[0]), c_w_uk[0].astype(BF16),
                        c_w_uv[0].T.astype(BF16), invf)
    oa, ob = _attention(q, k, v)
    h = _out_proj_mlp(h, oa, ob, norm_g[2], c_w_o[0].astype(BF16), w_up_b, w_down_b, 2)

    tr = lambda a: jnp.swapaxes(a, 1, 2)
    ab_re, ab_im, bb_re, bb_im = _s5_discretise(d_lam_re[0], d_lam_im[0], d_log_dt[0],
                                                tr(d_b_re[0]), tr(d_b_im[0]))
    bblk, cblk = _s5_block_matrices(bb_re, bb_im, d_c_re[0], d_c_im[0])
    h = _s5(h, norm_g[3], row(ab_re), row(ab_im), bblk, cblk, row(d_skip[0]),
            d_w_glu_a[0].astype(BF16), d_w_glu_b[0].astype(BF16))
    h = _mlp(h, norm_g[3], w_up_b, w_down_b, 3)
    return h.reshape(bsz, s, d)
```

```python
import functools
import math

import jax
import jax.numpy as jnp
from jax import lax
from jax.experimental import pallas as pl
from jax.experimental.pallas import tpu as pltpu

F32 = jnp.float32
BF16 = jnp.bfloat16

D_MODEL = 1024
D_FF = 4 * D_MODEL
RMS_EPS = 1e-6
LANES = 128
SUBLANES = 8
VMEM_LIMIT = 56 * 1024 * 1024

CHUNK = 128
GMLP_HEADS = 8
POOL_WINDOWS = (2, 4, 8, 16)
POOL_GROUP = D_MODEL // len(POOL_WINDOWS)
POOL_HALO = 32
assert POOL_WINDOWS == tuple(2 << k for k in range(len(POOL_WINDOWS)))
assert POOL_HALO == SUBLANES * len(POOL_WINDOWS)
MLA_HEADS = 8
Q_LORA = 384
KV_LORA = 256
QK_NOPE = 128
QK_ROPE = 64
V_HEAD = 128
ROPE_THETA = 10000.0
QK_PAD = 256
S5_GROUP_CH = 16
S5_GROUPS = D_MODEL // S5_GROUP_CH
S5_STATE = 64
S5_BLOCK_GROUPS = LANES // S5_GROUP_CH
S5_BLOCKS = S5_GROUPS // S5_BLOCK_GROUPS
S5_BLOCK_STATE = S5_BLOCK_GROUPS * S5_STATE


def _rms(x, g):
    return x * lax.rsqrt(jnp.mean(x * x, axis=-1, keepdims=True) + RMS_EPS) * g


def _gelu(x):
    c = math.sqrt(2.0 / math.pi)
    return 0.5 * x * (1.0 + jnp.tanh(c * (x + 0.044715 * (x * x * x))))


def _dot(a, b):
    return jnp.dot(a, b, preferred_element_type=F32)


def _const_spec(shape):
    nd = len(shape)
    return pl.BlockSpec(shape, lambda *_: (0,) * nd, pipeline_mode=pl.Buffered(1))


def _row_spec(tm, width):
    return pl.BlockSpec((tm, width), lambda i: (i, 0))


def _params(*sem, flags=None):
    return pltpu.CompilerParams(dimension_semantics=sem, vmem_limit_bytes=VMEM_LIMIT, flags=flags)


MLP_TF = 1024


def _mlp_tail(h, g_ref, wu_ref, wd_ref):
    z = _rms(h, g_ref[2:3, :]).astype(BF16)
    acc = jnp.zeros(h.shape, F32)
    for c in range(D_FF // MLP_TF):
        a = _dot(z, wu_ref[:, c * MLP_TF:(c + 1) * MLP_TF])
        a = jnp.square(jnp.maximum(a, 0.0)).astype(BF16)
        acc = acc + _dot(a, wd_ref[c * MLP_TF:(c + 1) * MLP_TF, :])
    return h + _rms(acc, g_ref[3:4, :])


def _mlp_weight_specs(w_up, w_down, layer):
    return [pl.BlockSpec((None,) + w.shape[1:], lambda *_: (layer, 0, 0), pipeline_mode=pl.Buffered(1))
            for w in (w_up, w_down)]


def _mlp_body(h_ref, g_ref, wu_ref, wd_ref, o_ref):
    o_ref[...] = _mlp_tail(h_ref[...], g_ref, wu_ref, wd_ref)


def _mlp(h, g, w_up, w_down, layer, *, tm=1024):
    s = h.shape[0]
    return pl.pallas_call(
        _mlp_body,
        grid=(s // tm,),
        in_specs=[_row_spec(tm, D_MODEL), _const_spec(g.shape)] + _mlp_weight_specs(w_up, w_down, layer),
        out_specs=_row_spec(tm, D_MODEL),
        out_shape=jax.ShapeDtypeStruct(h.shape, F32),
        compiler_params=_params("parallel"),
        name="mlp",
    )(h, g, w_up, w_down)


def _gmlp_body(h_ref, g_ref, win_ref, bin_ref, gv_ref, bv_ref, ws_ref, bs_ref, wout_ref, wu_ref, wd_ref,
               o_ref, sv_ref, *, tm):
    h = h_ref[...]
    z = _rms(h, g_ref[0:1, :]).astype(BF16)
    uv = _gelu(_dot(z, win_ref[...]) + bin_ref[...])
    u = uv[:, :D_MODEL]
    v = uv[:, D_MODEL:]
    mu = jnp.mean(v, axis=-1, keepdims=True)
    vc = v - mu
    v = vc * lax.rsqrt(jnp.mean(vc * vc, axis=-1, keepdims=True) + RMS_EPS)
    v = (v * gv_ref[...] + bv_ref[...]).astype(BF16)
    row = lax.broadcasted_iota(jnp.int32, (CHUNK, CHUNK), 0)
    col = lax.broadcasted_iota(jnp.int32, (CHUNK, CHUNK), 1)
    nchunk = tm // CHUNK
    for hd in range(GMLP_HEADS):
        ws = jnp.where(col <= row, ws_ref[hd], 0.0).astype(BF16)
        cols = slice(hd * LANES, (hd + 1) * LANES)
        rhs = jnp.concatenate([v[c * CHUNK:(c + 1) * CHUNK, cols] for c in range(nchunk)], axis=1)
        sv = _dot(ws, rhs)
        for c in range(nchunk):
            sv_ref[c * CHUNK:(c + 1) * CHUNK, cols] = (
                sv[:, c * LANES:(c + 1) * LANES] + bs_ref[:, cols])
    y = (u * sv_ref[...]).astype(BF16)
    o_ref[...] = _mlp_tail(h + _rms(_dot(y, wout_ref[...]), g_ref[1:2, :]), g_ref, wu_ref, wd_ref)


def _gmlp_mlp(h, g, w_in, b_in, g_v, b_v, w_s, b_s_full, w_out, w_up, w_down, layer, *, tm=512):
    s = h.shape[0]
    return pl.pallas_call(
        functools.partial(_gmlp_body, tm=tm),
        grid=(s // tm,),
        in_specs=[_row_spec(tm, D_MODEL), _const_spec(g.shape), _const_spec(w_in.shape),
                  _const_spec(b_in.shape), _const_spec(g_v.shape), _const_spec(b_v.shape),
                  _const_spec(w_s.shape), _const_spec(b_s_full.shape), _const_spec(w_out.shape)]
        + _mlp_weight_specs(w_up, w_down, layer),
        out_specs=_row_spec(tm, D_MODEL),
        out_shape=jax.ShapeDtypeStruct(h.shape, F32),
        scratch_shapes=[pltpu.VMEM((tm, D_MODEL), F32)],
        compiler_params=_params("parallel"),
        name="gmlp_mlp",
    )(h, g, w_in, b_in, g_v, b_v, w_s, b_s_full, w_out, w_up, w_down)


def _pool_body(h_ref, halo_ref, g_ref, w_ref, scale_ref, wu_ref, wd_ref, o_ref, zz_ref, ws_ref,
               *, tm):
    i = pl.program_id(0)
    h = h_ref[...]
    z = _rms(h, g_ref[0:1, :])
    zh = _rms(halo_ref[...], g_ref[0:1, :])
    zz_ref[0:POOL_HALO, :] = jnp.where(i > 0, zh, 0.0)
    zz_ref[POOL_HALO:, :] = z
    nrows = tm + POOL_HALO
    for k in range(1, len(POOL_WINDOWS) + 1):
        d = 1 << (k - 1)
        lo = SUBLANES * k
        cols = slice((k - 1) * POOL_GROUP, D_MODEL)
        src = zz_ref if k == 1 else ws_ref
        ws_ref[lo:, cols] = src[lo:, cols] + src[lo - d:nrows - d, cols]
    pos = i * tm + lax.broadcasted_iota(jnp.int32, (tm, 1), 0)
    outs = []
    for gi, w in enumerate(POOL_WINDOWS):
        cols = slice(gi * POOL_GROUP, (gi + 1) * POOL_GROUP)
        xg = z[:, cols]
        win = ws_ref[POOL_HALO:, cols]
        count = jnp.minimum(pos + 1, w).astype(F32)
        diff = (win / count - xg).astype(BF16)
        outs.append(_dot(diff, w_ref[gi]))
    out = jnp.concatenate(outs, axis=-1) * scale_ref[...]
    o_ref[...] = _mlp_tail(h + _rms(out, g_ref[1:2, :]), g_ref, wu_ref, wd_ref)


def _pool_mlp(h, g, w_grp, scale, w_up, w_down, layer, *, tm=1024):
    s = h.shape[0]
    ratio = tm // POOL_HALO
    return pl.pallas_call(
        functools.partial(_pool_body, tm=tm),
        grid=(s // tm,),
        in_specs=[_row_spec(tm, D_MODEL),
                  pl.BlockSpec((POOL_HALO, D_MODEL), lambda i: (jnp.maximum(i * ratio - 1, 0), 0)),
                  _const_spec(g.shape), _const_spec(w_grp.shape), _const_spec(scale.shape)]
        + _mlp_weight_specs(w_up, w_down, layer),
        out_specs=_row_spec(tm, D_MODEL),
        out_shape=jax.ShapeDtypeStruct(h.shape, F32),
        scratch_shapes=[pltpu.VMEM((tm + POOL_HALO, D_MODEL), F32)] * 2,
        compiler_params=_params("parallel"),
        name="pool_mlp",
    )(h, h, g, w_grp, scale, w_up, w_down)


def _rope_tile(x, cos, sin_lo, sin_hi):
    n = x.shape[1] // LANES
    outs = []
    for t in range(n):
        xt = x[:, t * LANES:(t + 1) * LANES]
        up = pltpu.roll(xt, 32, axis=1)
        down = pltpu.roll(xt, LANES - 32, axis=1)
        outs.append(xt * cos + down * sin_lo + up * sin_hi)
    return outs


def _mla_proj_body(h_ref, pos_ref, g_ref, wdq_ref, gq_ref, wuqn_ref, wuqr_ref, wdkv_ref, gkv_ref,
                   wuk_ref, wuv_ref, invf_ref, q_ref, k_ref, v_ref, *, scale):
    h = h_ref[...]
    z = _rms(h, g_ref[0:1, :]).astype(BF16)
    ang = pos_ref[...].astype(F32) * invf_ref[...]
    lane = lax.broadcasted_iota(jnp.int32, ang.shape, 1)
    cos = jnp.where(lane < QK_ROPE, jnp.cos(ang), 0.0)
    sin = jnp.sin(ang)
    sin_lo = jnp.where(lane < QK_ROPE // 2, -sin, 0.0)
    sin_hi = jnp.where((lane >= QK_ROPE // 2) & (lane < QK_ROPE), sin, 0.0)

    ql = _rms(_dot(z, wdq_ref[...]), gq_ref[...]).astype(BF16)
    qn = _dot(ql, wuqn_ref[...]) * scale
    qr = _rope_tile(_dot(ql, wuqr_ref[...]) * scale, cos, sin_lo, sin_hi)
    ckv = _dot(z, wdkv_ref[...])
    c = _rms(ckv[:, :KV_LORA], gkv_ref[...]).astype(BF16)
    kr = _rope_tile(ckv[:, KV_LORA:], cos, sin_lo, sin_hi)[0].astype(BF16)
    kn = _dot(c, wuk_ref[...])
    for hd in range(MLA_HEADS):
        cols = slice(hd * LANES, (hd + 1) * LANES)
        q_ref[:, hd * QK_PAD:hd * QK_PAD + LANES] = qn[:, cols].astype(BF16)
        q_ref[:, hd * QK_PAD + LANES:(hd + 1) * QK_PAD] = qr[hd].astype(BF16)
        k_ref[:, hd * QK_PAD:hd * QK_PAD + LANES] = kn[:, cols].astype(BF16)
        k_ref[:, hd * QK_PAD + LANES:(hd + 1) * QK_PAD] = kr
    v_ref[...] = lax.dot_general(wuv_ref[...], c, (((1,), (1,)), ((), ())),
                                 preferred_element_type=F32).astype(BF16)


def _mla_proj(h, pos, g, w_dq, g_q, w_uq_n, w_uq_r, w_dkv, g_kv, w_uk, w_uv, invf, *, tm=1024):
    s = h.shape[0]
    scale = (QK_NOPE + QK_ROPE) ** -0.5 * math.log2(math.e)
    consts = (g, w_dq, g_q, w_uq_n, w_uq_r, w_dkv, g_kv, w_uk, w_uv, invf)
    return pl.pallas_call(
        functools.partial(_mla_proj_body, scale=scale),
        grid=(s // tm,),
        in_specs=[_row_spec(tm, D_MODEL), _row_spec(tm, 1)] + [_const_spec(a.shape) for a in consts],
        out_specs=[_row_spec(tm, MLA_HEADS * QK_PAD), _row_spec(tm, MLA_HEADS * QK_PAD),
                   pl.BlockSpec((MLA_HEADS * V_HEAD, tm), lambda i: (0, i))],
        out_shape=[jax.ShapeDtypeStruct((s, MLA_HEADS * QK_PAD), BF16),
                   jax.ShapeDtypeStruct((s, MLA_HEADS * QK_PAD), BF16),
                   jax.ShapeDtypeStruct((MLA_HEADS * V_HEAD, s), BF16)],
        compiler_params=_params("parallel"),
        name="mla_proj",
    )(h, pos, *consts)


ATTN_SLOTS = 4
ATTN_UNROLL = 16
ATTN_MAX_JUMP = 64.0


def _attn_body(qa_ref, qb_ref, k_ref, vt_ref, oa_ref, ob_ref, qt_ref, m_ref, acc_ref, l_ref, lag_ref,
               rl_ref, jump_ref, low_ref, s_ref, *slot_refs, t, nt):
    qa = pl.program_id(1)
    qb = nt - 1 - qa
    ns = ATTN_SLOTS
    p_refs, a_refs = slot_refs[:ns], slot_refs[ns:]
    neg = -1e30
    rows = 2 * SUBLANES
    last = nt

    qt_ref[0] = qa_ref[...].astype(F32).T.astype(BF16)
    qt_ref[1] = qb_ref[...].astype(F32).T.astype(BF16)
    for state in (m_ref, acc_ref, l_ref, lag_ref, rl_ref, jump_ref, low_ref):
        state[...] = jnp.zeros(state.shape, F32)

    def owner(j):
        return jnp.where(j < 2, j, (j >= qa + 2).astype(jnp.int32))

    def key_tile(j):
        full = jnp.where(j >= qa + 2, j - 2 - qa, j - 2)
        return jnp.where(j == 0, qa, jnp.where(j == 1, qb, full))

    def scores(j, masked):
        start = pl.multiple_of(key_tile(j) * t, t)
        s = _dot(k_ref[pl.ds(start, t), :], qt_ref[owner(j)])
        if masked:
            kpos = lax.broadcasted_iota(jnp.int32, (t, t), 0)
            qpos = lax.broadcasted_iota(jnp.int32, (t, t), 1)
            s = jnp.where(kpos <= qpos, s, neg)
        return s

    def fused(j, slot, masked=False):
        own = owner(j)
        ref = lag_ref[own]
        alpha = jnp.exp2(rl_ref[own] - ref)
        a_refs[slot][...] = alpha
        rl_ref[own] = ref
        s = scores(j, masked)
        rb = jnp.broadcast_to(ref, (SUBLANES, t))
        p_ref = p_refs[slot]
        mx = [None, None]
        ps = [None, None]
        for r in range(t // rows):
            lo = s[r * rows:r * rows + SUBLANES, :]
            hi = s[r * rows + SUBLANES:(r + 1) * rows, :]
            mx[0] = lo if mx[0] is None else jnp.maximum(mx[0], lo)
            mx[1] = hi if mx[1] is None else jnp.maximum(mx[1], hi)
            plo = jnp.exp2(lo - rb)
            phi = jnp.exp2(hi - rb)
            ps[0] = plo if ps[0] is None else ps[0] + plo
            ps[1] = phi if ps[1] is None else ps[1] + phi
            p_ref[r * rows:(r + 1) * rows, :] = jnp.concatenate([plo, phi], axis=0).astype(BF16)
        l_ref[own] = l_ref[own] * alpha + jnp.sum(ps[0] + ps[1], axis=0, keepdims=True)
        m_tile = jnp.max(jnp.maximum(mx[0], mx[1]), axis=0, keepdims=True)
        m_old = m_ref[own]
        lag_ref[own] = m_old
        m_ref[own] = jnp.maximum(m_old, m_tile)
        jump_ref[...] = jnp.maximum(jump_ref[...], m_tile - ref)
        if masked:
            low_ref[...] = jnp.minimum(low_ref[...], m_tile - ref)

    def pv(j, slot):
        start = pl.multiple_of(key_tile(j) * t, t)
        own = owner(j)
        acc_ref[own] = acc_ref[own] * a_refs[slot][...] + _dot(vt_ref[:, pl.ds(start, t)],
                                                                p_refs[slot][...])

    def stage(k, slot):
        fused(k + 2, (slot + 2) % ns)
        pv(k, slot)

    fused(0, 0, masked=True)
    fused(1, 1, masked=True)
    nstage = nt - 1

    def body(jq, _):
        for u in range(ATTN_UNROLL):
            stage(ATTN_UNROLL * jq + u, u % ns)
        return 0

    lax.fori_loop(0, nstage // ATTN_UNROLL, body, 0)
    for k in range(nstage - nstage % ATTN_UNROLL, nstage):
        stage(k, k % ns)
    pv(last - 1, (last - 1) % ns)
    pv(last, last % ns)

    def finish():
        for own, o_ref in ((0, oa_ref), (1, ob_ref)):
            o_t = acc_ref[own] * (1.0 / l_ref[own])
            o_ref[...] = o_t.T.astype(o_ref.dtype)

    finish()

    @pl.when((jnp.max(jump_ref[...]) > ATTN_MAX_JUMP) | (jnp.min(low_ref[...]) < -ATTN_MAX_JUMP))
    def _():
        m_ref[...] = jnp.full(m_ref.shape, neg, F32)
        acc_ref[...] = jnp.zeros(acc_ref.shape, F32)
        l_ref[...] = jnp.zeros(l_ref.shape, F32)

        def exact(j, masked):
            own = owner(j)
            s_ref[...] = scores(j, masked)
            mx = s_ref[0:SUBLANES, :]
            for r in range(1, t // SUBLANES):
                mx = jnp.maximum(mx, s_ref[r * SUBLANES:(r + 1) * SUBLANES, :])
            m_old = m_ref[own]
            m_new = jnp.maximum(m_old, jnp.max(mx, axis=0, keepdims=True))
            m_ref[own] = m_new
            alpha = jnp.exp2(m_old - m_new)
            a_refs[0][...] = alpha
            mb = jnp.broadcast_to(m_new, (rows, t))
            psum = jnp.zeros((rows, t), F32)
            for r in range(t // rows):
                p = jnp.exp2(s_ref[r * rows:(r + 1) * rows, :] - mb)
                psum = psum + p
                p_refs[0][r * rows:(r + 1) * rows, :] = p.astype(BF16)
            l_ref[own] = l_ref[own] * alpha + jnp.sum(psum, axis=0, keepdims=True)
            pv(j, 0)

        exact(0, True)
        exact(1, True)
        lax.fori_loop(2, last + 1, lambda j, c: (exact(j, False), c)[1], 0)
        finish()


def _attention(q, k, vt, *, t=512):
    s = q.shape[0]
    nt = s // t
    assert nt % 2 == 0
    half = nt // 2
    out = jax.ShapeDtypeStruct((s // 2, MLA_HEADS * V_HEAD), BF16)
    vec = pltpu.VMEM((2, 1, t), F32)
    return pl.pallas_call(
        functools.partial(_attn_body, t=t, nt=nt),
        grid=(MLA_HEADS, half),
        in_specs=[pl.BlockSpec((t, QK_PAD), lambda hd, i: (i, hd)),
                  pl.BlockSpec((t, QK_PAD), lambda hd, i: (nt - 1 - i, hd)),
                  pl.BlockSpec((s, QK_PAD), lambda hd, i: (0, hd)),
                  pl.BlockSpec((V_HEAD, s), lambda hd, i: (hd, 0))],
        out_specs=[pl.BlockSpec((t, V_HEAD), lambda hd, i: (i, hd)),
                   pl.BlockSpec((t, V_HEAD), lambda hd, i: (half - 1 - i, hd))],
        out_shape=[out, out],
        scratch_shapes=[
            pltpu.VMEM((2, QK_PAD, t), BF16),
            vec,
            pltpu.VMEM((2, V_HEAD, t), F32),
            vec,
            vec,
            vec,
            pltpu.VMEM((1, t), F32),
            pltpu.VMEM((1, t), F32),
            pltpu.VMEM((t, t), F32),
        ] + [pltpu.VMEM((t, t), BF16)] * ATTN_SLOTS
          + [pltpu.VMEM((1, t), F32)] * ATTN_SLOTS,
        compiler_params=_params("parallel", "parallel"),
        name="mla_attn",
    )(q, q, k, vt)


def _out_proj_body(h_ref, oa_ref, ob_ref, g_ref, wo_ref, wu_ref, wd_ref, o_ref, *, half):
    o = jnp.where(pl.program_id(0) < half, oa_ref[...], ob_ref[...])
    h = h_ref[...] + _rms(_dot(o, wo_ref[...]), g_ref[1:2, :])
    o_ref[...] = _mlp_tail(h, g_ref, wu_ref, wd_ref)


def _out_proj_mlp(h, oa, ob, g, w_o, w_up, w_down, layer, *, tm=1024):
    s = h.shape[0]
    half = oa.shape[0] // tm
    return pl.pallas_call(
        functools.partial(_out_proj_body, half=half),
        grid=(s // tm,),
        in_specs=[_row_spec(tm, D_MODEL),
                  pl.BlockSpec((tm, D_MODEL), lambda i: (jnp.minimum(i, half - 1), 0)),
                  pl.BlockSpec((tm, D_MODEL), lambda i: (jnp.maximum(i - half, 0), 0)),
                  _const_spec(g.shape), _const_spec(w_o.shape)]
        + _mlp_weight_specs(w_up, w_down, layer),
        out_specs=_row_spec(tm, D_MODEL),
        out_shape=jax.ShapeDtypeStruct(h.shape, F32),
        compiler_params=_params("parallel"),
        name="mla_out_mlp",
    )(h, oa, ob, g, w_o, w_up, w_down)


def _s5_disc_body(lr_ref, li_ref, ldt_ref, br_ref, bi_ref, abr_ref, abi_ref, bbr_ref, bbi_ref):
    dt = jnp.exp(ldt_ref[...])
    lr = lr_ref[...]
    li = li_ref[...]
    mag = jnp.exp(lr * dt)
    ab_re = mag * jnp.cos(li * dt)
    ab_im = mag * jnp.sin(li * dt)
    den = lr * lr + li * li
    f_re = ((ab_re - 1.0) * lr + ab_im * li) / den
    f_im = (ab_im * lr - (ab_re - 1.0) * li) / den
    abr_ref[...] = ab_re
    abi_ref[...] = ab_im
    br = br_ref[...]
    bi = bi_ref[...]
    bbr_ref[...] = f_re[:, None, :] * br - f_im[:, None, :] * bi
    bbi_ref[...] = f_re[:, None, :] * bi + f_im[:, None, :] * br


def _s5_discretise(lam_re, lam_im, log_dt, b_re_t, b_im_t):
    g, p = lam_re.shape
    c = b_re_t.shape[1]
    return pl.pallas_call(
        _s5_disc_body,
        out_shape=[jax.ShapeDtypeStruct((g, p), F32), jax.ShapeDtypeStruct((g, p), F32),
                   jax.ShapeDtypeStruct((g, c, p), F32), jax.ShapeDtypeStruct((g, c, p), F32)],
        name="s5_disc",
    )(lam_re, lam_im, log_dt.reshape(g, 1), b_re_t, b_im_t)


def _s5_body(h_ref, g_ref, perm_ref, ar_ref, ai_ref, bblk_ref, cblk_ref, dskip_ref,
             wa_ref, wb_ref, o_ref, xr0_ref, xi0_ref, xr1_ref, xi1_ref, xc0_ref, xc1_ref,
             yp_ref, y_ref, pr_ref, pi_ref, cr_ref, ci_ref, *, tm):
    seg = tm // SUBLANES
    i0 = pl.program_id(0)
    nst = S5_BLOCK_STATE
    pack = 2 * SUBLANES

    @pl.when(i0 == 0)
    def _():
        cr_ref[...] = jnp.zeros(cr_ref.shape, F32)
        ci_ref[...] = jnp.zeros(ci_ref.shape, F32)
        ar = ar_ref[...]
        ai = ai_ref[...]
        bcast = lambda v: jnp.broadcast_to(v, (SUBLANES, v.shape[1]))
        pr_ref[0:SUBLANES, :] = bcast(ar)
        pi_ref[0:SUBLANES, :] = bcast(ai)

        def pw(j, carry):
            r, im = carry
            r, im = r * ar - im * ai, r * ai + im * ar
            rows = pl.ds(pl.multiple_of(j * SUBLANES, SUBLANES), SUBLANES)
            pr_ref[rows, :] = bcast(r)
            pi_ref[rows, :] = bcast(im)
            return r, im

        lax.fori_loop(1, seg, pw, (ar, ai))

    h = h_ref[...]
    z = _rms(h, g_ref[0:1, :])
    zp = _dot(perm_ref[...], z.astype(BF16)).astype(BF16)

    for cb in range(S5_BLOCKS):
        st = slice(cb * nst, (cb + 1) * nst)
        xr_ref, xi_ref, xc_ref = ((xr0_ref, xi0_ref, xc0_ref), (xr1_ref, xi1_ref, xc1_ref))[cb % 2]
        bu = _dot(zp[:, cb * LANES:(cb + 1) * LANES], bblk_ref[cb])
        ar = jnp.broadcast_to(ar_ref[:, st], (SUBLANES, nst))
        ai = jnp.broadcast_to(ai_ref[:, st], (SUBLANES, nst))
        xr = jnp.zeros((SUBLANES, nst), F32)
        xi = xr
        for i in range(seg):
            rows = slice(i * SUBLANES, (i + 1) * SUBLANES)
            xr, xi = (ar * xr - ai * xi + bu[rows, :nst], ar * xi + ai * xr + bu[rows, nst:])
            xr_ref[rows, :] = xr
            xi_ref[rows, :] = xi
        alr = pr_ref[tm - 1:tm, st]
        ali = pi_ref[tm - 1:tm, st]
        c_r = cr_ref[:, st]
        c_i = ci_ref[:, st]
        rows_r, rows_i = [], []
        for sgm in range(SUBLANES):
            rows_r.append(c_r)
            rows_i.append(c_i)
            e_r = xr[sgm:sgm + 1, :]
            e_i = xi[sgm:sgm + 1, :]
            c_r, c_i = alr * c_r - ali * c_i + e_r, alr * c_i + ali * c_r + e_i
        cr_ref[:, st] = c_r
        ci_ref[:, st] = c_i
        cin_r = jnp.concatenate(rows_r, axis=0)
        cin_i = jnp.concatenate(rows_i, axis=0)
        for i2 in range(seg // 2):
            fr, fi = [], []
            for i in (2 * i2, 2 * i2 + 1):
                rows = slice(i * SUBLANES, (i + 1) * SUBLANES)
                p_r = pr_ref[rows, st]
                p_i = pi_ref[rows, st]
                fr.append(xr_ref[rows, :] + (p_r * cin_r - p_i * cin_i))
                fi.append(xi_ref[rows, :] + (p_r * cin_i + p_i * cin_r))
            rows2 = slice(i2 * pack, (i2 + 1) * pack)
            xc_ref[rows2, 0:nst] = jnp.concatenate(fr, axis=0).astype(BF16)
            xc_ref[rows2, nst:2 * nst] = jnp.concatenate(fi, axis=0).astype(BF16)
        yp_ref[cb] = _dot(xc_ref[...], cblk_ref[cb])

    for sgm in range(SUBLANES):
        for j in range(seg // SUBLANES):
            dst = slice(sgm * seg + j * SUBLANES, sgm * seg + (j + 1) * SUBLANES)
            src = pl.ds(j * SUBLANES * SUBLANES + sgm, SUBLANES, stride=SUBLANES)
            for cb in range(S5_BLOCKS):
                y_ref[dst, cb * LANES:(cb + 1) * LANES] = yp_ref[cb, src, :]
    y = _gelu(y_ref[...] + dskip_ref[...] * z).astype(BF16)
    out = _dot(y, wa_ref[...]) * jax.nn.sigmoid(_dot(y, wb_ref[...]))
    o_ref[...] = h + _rms(out, g_ref[1:2, :])


def _s5_permutation(tm):
    seg = tm // SUBLANES
    r = jnp.arange(tm)
    src = (r % SUBLANES) * seg + r // SUBLANES
    return (src[:, None] == jnp.arange(tm)[None, :]).astype(BF16)


def _s5(h, g, ab_re, ab_im, bblk, cblk, dskip, w_a, w_b, *, tm=512):
    s = h.shape[0]
    seg = tm // SUBLANES
    nstate = S5_GROUPS * S5_STATE
    perm = _s5_permutation(tm)
    consts = (g, perm, ab_re, ab_im, bblk, cblk, dskip, w_a, w_b)
    state = pltpu.VMEM((tm, S5_BLOCK_STATE), F32)
    return pl.pallas_call(
        functools.partial(_s5_body, tm=tm),
        grid=(s // tm,),
        in_specs=[_row_spec(tm, D_MODEL)] + [_const_spec(a.shape) for a in consts],
        out_specs=_row_spec(tm, D_MODEL),
        out_shape=jax.ShapeDtypeStruct(h.shape, F32),
        scratch_shapes=[
            state, state, state, state,
            pltpu.VMEM((tm, 2 * S5_BLOCK_STATE), BF16),
            pltpu.VMEM((tm, 2 * S5_BLOCK_STATE), BF16),
            pltpu.VMEM((S5_BLOCKS, tm, LANES), F32),
            pltpu.VMEM((tm, D_MODEL), F32),
            pltpu.VMEM((tm, nstate), F32),
            pltpu.VMEM((tm, nstate), F32),
            pltpu.VMEM((1, nstate), F32),
            pltpu.VMEM((1, nstate), F32),
        ],
        compiler_params=_params("arbitrary"),
        name="s5",
    )(h, *consts)


def _s5_block_matrices(bb_re, bb_im, c_re, c_im):
    nb, gb, c, p = S5_BLOCKS, S5_BLOCK_GROUPS, S5_GROUP_CH, S5_STATE
    eye = jnp.eye(gb, dtype=F32)

    def in_map(bb):
        t = bb.reshape(nb, gb, c, p)
        return jnp.einsum('bgcp,gk->bgckp', t, eye).reshape(nb, gb * c, gb * p)

    def out_map(cc):
        t = cc.reshape(nb, gb, c, p)
        return jnp.einsum('bgcp,gk->bgpkc', t, eye).reshape(nb, gb * p, gb * c)

    bblk = jnp.concatenate([in_map(bb_re), in_map(bb_im)], axis=2).astype(BF16)
    cblk = jnp.concatenate([out_map(c_re), out_map(-c_im)], axis=1).astype(BF16)
    return bblk, cblk


def _mla_weights(w_uq, w_dkv):
    wq = w_uq.reshape(Q_LORA, MLA_HEADS, QK_NOPE + QK_ROPE)
    w_n = wq[:, :, :QK_NOPE].reshape(Q_LORA, MLA_HEADS * QK_NOPE)
    w_r = jnp.pad(wq[:, :, QK_NOPE:], ((0, 0), (0, 0), (0, LANES - QK_ROPE)))
    w_r = w_r.reshape(Q_LORA, MLA_HEADS * LANES)
    w_dkv_p = jnp.pad(w_dkv, ((0, 0), (0, LANES - QK_ROPE)))
    return w_n.astype(BF16), w_r.astype(BF16), w_dkv_p.astype(BF16)


def kernel(x, positions, norm_g, w_up, w_down, a_w_in, a_b_in, a_g_v, a_b_v, a_w_s, a_b_s, a_w_out, b_w_grp, b_scale, c_w_dq, c_g_q, c_w_uq, c_w_dkv, c_g_kv, c_w_uk, c_w_uv, c_w_o, d_lam_re, d_lam_im, d_log_dt, d_b_re, d_b_im, d_c_re, d_c_im, d_skip, d_w_glu_a, d_w_glu_b):
    bsz, s, d = x.shape
    assert bsz == 1 and d == D_MODEL
    h = x.reshape(s, d)
    row = lambda a: a.reshape(1, -1)
    w_up_b = w_up.astype(BF16)
    w_down_b = w_down.astype(BF16)

    b_s_full = jnp.repeat(a_b_s[0].T, D_MODEL // GMLP_HEADS, axis=1)
    h = _gmlp_mlp(h, norm_g[0], a_w_in[0].astype(BF16), row(a_b_in[0]), row(a_g_v[0]), row(a_b_v[0]),
                  a_w_s[0], b_s_full, a_w_out[0].astype(BF16), w_up_b, w_down_b, 0)

    h = _pool_mlp(h, norm_g[1], b_w_grp[0].astype(BF16), row(b_scale[0]), w_up_b, w_down_b, 1)

    w_uq_n, w_uq_r, w_dkv_p = _mla_weights(c_w_uq[0], c_w_dkv[0])
    inv_freq = ROPE_THETA ** (-jnp.arange(0, QK_ROPE, 2, dtype=F32) / QK_ROPE)
    invf = jnp.concatenate([inv_freq, inv_freq, jnp.zeros((LANES - QK_ROPE,), F32)]).reshape(1, LANES)
    q, k, v = _mla_proj(h, positions.reshape(s, 1), norm_g[2], c_w_dq[0].astype(BF16), row(c_g_q[0]),
                        w_uq_n, w_uq_r, w_dkv_p, row(c_g_kv[0]), c_w_uk[0].astype(BF16),
                        c_w_uv[0].T.astype(BF16), invf)
    oa, ob = _attention(q, k, v)
    h = _out_proj_mlp(h, oa, ob, norm_g[2], c_w_o[0].astype(BF16), w_up_b, w_down_b, 2)

    tr = lambda a: jnp.swapaxes(a, 1, 2)
    ab_re, ab_im, bb_re, bb_im = _s5_discretise(d_lam_re[0], d_lam_im[0], d_log_dt[0],
                                                tr(d_b_re[0]), tr(d_b_im[0]))
    bblk, cblk = _s5_block_matrices(bb_re, bb_im, d_c_re[0], d_c_im[0])
    h = _s5(h, norm_g[3], row(ab_re), row(ab_im), bblk, cblk, row(d_skip[0]),
            d_w_glu_a[0].astype(BF16), d_w_glu_b[0].astype(BF16))
    h = _mlp(h, norm_g[3], w_up_b, w_down_b, 3)
    return h.reshape(bsz, s, d)
```
